```python
import jax, jax.numpy as jnp
from jax import lax
import numpy as np

D_MODEL = 4096
BATCH = 4
SEQ = 2048
DEPTH = 1
DEC_BATCH = 128
DEC_SEQ = 1
PAST_LEN = 16384
PAGE_SIZE = 128

D_MIX = D_MODEL
D_POOL = D_MIX // 4
POOL_WINDOWS = (2, 4, 8, 16)
N_POOL_GROUPS = len(POOL_WINDOWS)
POOL_GROUP = D_POOL // N_POOL_GROUPS
POOL_STATE = max(POOL_WINDOWS) - 1
GLA_HEADS = 4
D_GLA_V = D_MIX - D_POOL
DV_HEAD = D_GLA_V // GLA_HEADS
DK_HEAD = DV_HEAD // 2
D_GLA_K = GLA_HEADS * DK_HEAD
GATE_RANK = 16
GATE_NORM = 16.0
GLA_CHUNK = 64
D_IN = D_POOL + 2 * D_GLA_K + 2 * D_GLA_V + GATE_RANK
D_FF = ((8 * D_MODEL // 3 + 255) // 256) * 256
N_MOD = 6
EPS = 1e-6

kernel_name = "pool_gla_hybrid_adaln_step"


def rmsnorm(x, gain):
    xf = x.astype(jnp.float32)
    y = xf * lax.rsqrt(jnp.mean(xf * xf, axis=-1, keepdims=True) + EPS)
    return (y * gain.astype(jnp.float32)).astype(x.dtype)


def multiscale_pool(buf, pos):
    P = POOL_STATE
    B, L, C = buf.shape
    T = L - P
    bf = buf.astype(jnp.float32)
    cs = jnp.concatenate([jnp.zeros((B, 1, C), jnp.float32), jnp.cumsum(bf, axis=1)], axis=1)
    outs = []
    for gi, w in enumerate(POOL_WINDOWS):
        lo, hi = gi * POOL_GROUP, (gi + 1) * POOL_GROUP
        win_sum = cs[:, P + 1:P + T + 1, lo:hi] - cs[:, P + 1 - w:P + T + 1 - w, lo:hi]
        cnt = jnp.minimum(w, pos + 1).astype(jnp.float32)[None, :, None]
        outs.append(win_sum / cnt - bf[:, P:, lo:hi])
    return jnp.concatenate(outs, axis=-1).astype(buf.dtype)


def pool_mixer(u, prefix, pos, w_pool, pool_scale):
    buf = jnp.concatenate([prefix.astype(u.dtype), u], axis=1)
    d = multiscale_pool(buf, pos)
    B, T = d.shape[:2]
    y = jnp.einsum('btgc,gcd->btgd', d.reshape(B, T, N_POOL_GROUPS, POOL_GROUP), w_pool)
    y = y.reshape(B, T, D_POOL) * pool_scale
    return y, buf[:, -POOL_STATE:]


def gla_recurrence(q, k, v, log_a, s0):
    B, T, H, _ = q.shape
    DV = v.shape[-1]
    C = min(GLA_CHUNK, T)
    n = -(-T // C)
    pad = n * C - T

    def blocks(t):
        t = jnp.pad(t.astype(jnp.float32), ((0, 0), (0, pad), (0, 0), (0, 0)))
        return t.reshape(B, n, C, H, t.shape[-1]).transpose(1, 0, 3, 2, 4)

    mask = jnp.tril(jnp.ones((C, C), dtype=bool))[:, :, None]

    def step(S, inp):
        qc, kc, vc, ac = inp
        b = jnp.cumsum(ac, axis=2)
        o_inter = jnp.einsum('bhid,bhde->bhie', qc * jnp.exp(b), S)
        diff = b[:, :, :, None, :] - b[:, :, None, :, :]
        decay = jnp.where(mask, jnp.exp(jnp.minimum(diff, 0.0)), 0.0)
        scores = jnp.einsum('bhid,bhijd,bhjd->bhij', qc, decay, kc)
        o_intra = jnp.einsum('bhij,bhje->bhie', scores, vc)
        b_last = b[:, :, -1, :]
        S_new = jnp.exp(b_last)[..., None] * S + jnp.einsum(
            'bhjd,bhje->bhde', kc * jnp.exp(b_last[:, :, None, :] - b), vc)
        return S_new, o_inter + o_intra

    S, o = lax.scan(step, s0.astype(jnp.float32), (blocks(q), blocks(k), blocks(v), blocks(log_a)))
    o = o.transpose(1, 0, 3, 2, 4).reshape(B, n * C, H, DV)[:, :T]
    return o, S


def decoder_layer(x, c, pool_prefix, gla_s0, pos, w_ada, b_ada, g_norm1, w_in, w_pool, pool_scale,
                  w_a2, b_a, g_gla_out, w_out, g_norm2, w_ff1, w_ff3, w_ff2):
    B, T, _ = x.shape
    mod = (jax.nn.silu(c) @ w_ada + b_ada)[:, None, :]
    sh1, sc1, gt1, sh2, sc2, gt2 = jnp.split(mod, N_MOD, axis=-1)
    h = rmsnorm(x, g_norm1) * (1 + sc1) + sh1
    z = h @ w_in
    s1 = D_POOL
    s2 = s1 + D_GLA_K
    s3 = s2 + D_GLA_K
    s4 = s3 + D_GLA_V
    s5 = s4 + D_GLA_V
    u, q, k, v, g, a_lr = jnp.split(z, [s1, s2, s3, s4, s5], axis=-1)
    pool_y, pool_state = pool_mixer(u, pool_prefix, pos, w_pool, pool_scale)
    log_a = jax.nn.log_sigmoid((a_lr @ w_a2 + b_a).astype(jnp.float32)) / GATE_NORM
    qh = q.reshape(B, T, GLA_HEADS, DK_HEAD) * (DK_HEAD ** -0.5)
    kh = k.reshape(B, T, GLA_HEADS, DK_HEAD)
    vh = v.reshape(B, T, GLA_HEADS, DV_HEAD)
    ah = log_a.reshape(B, T, GLA_HEADS, DK_HEAD)
    o, gla_state = gla_recurrence(qh, kh, vh, ah, gla_s0)
    o = rmsnorm(o.astype(x.dtype), g_gla_out).reshape(B, T, D_GLA_V)
    o = jax.nn.silu(g) * o
    mix = jnp.concatenate([pool_y, o], axis=-1) @ w_out
    x = x + gt1 * mix
    h2 = rmsnorm(x, g_norm2) * (1 + sc2) + sh2
    ff = (jax.nn.silu(h2 @ w_ff1) * (h2 @ w_ff3)) @ w_ff2
    x = x + gt2 * ff
    return x, pool_state, gla_state.astype(gla_s0.dtype)


def setup_inputs(seed: int = 0) -> dict:
    key = jax.random.key(seed)
    ks = jax.random.split(key, 24)
    L = DEPTH

    def nrm(k, shape, s):
        return jax.random.normal(k, shape, jnp.float32) * s

    return {
        "x_prompt": nrm(ks[0], (BATCH, SEQ, D_MODEL), 1.0),
        "x_sample": nrm(ks[1], (DEC_BATCH, DEC_SEQ, D_MODEL), 1.0),
        "state_pool": nrm(ks[2], (L, DEC_BATCH, POOL_STATE, D_POOL), 1.0),
        "state_gla": nrm(ks[3], (L, DEC_BATCH, GLA_HEADS, DK_HEAD, DV_HEAD), 0.5),
        "c_prompt": nrm(ks[4], (BATCH, D_MODEL), 1.0),
        "c_sample": nrm(ks[5], (DEC_BATCH, D_MODEL), 1.0),
        "w_ada": nrm(ks[6], (L, D_MODEL, N_MOD * D_MODEL), 0.5 * D_MODEL ** -0.5),
        "b_ada": nrm(ks[7], (L, N_MOD * D_MODEL), 0.02),
        "g_norm1": 1.0 + nrm(ks[8], (L, D_MODEL), 0.05),
        "w_in": nrm(ks[9], (L, D_MODEL, D_IN), D_MODEL ** -0.5),
        "w_pool": nrm(ks[10], (L, N_POOL_GROUPS, POOL_GROUP, POOL_GROUP), POOL_GROUP ** -0.5),
        "pool_scale": 1.0 + nrm(ks[11], (L, D_POOL), 0.1),
        "w_a2": nrm(ks[12], (L, GATE_RANK, D_GLA_K), GATE_RANK ** -0.5),
        "b_a": nrm(ks[13], (L, D_GLA_K), 0.1),
        "g_gla_out": 1.0 + nrm(ks[14], (L, GLA_HEADS, DV_HEAD), 0.05),
        "w_out": nrm(ks[15], (L, D_MIX, D_MODEL), D_MIX ** -0.5),
        "g_norm2": 1.0 + nrm(ks[16], (L, D_MODEL), 0.05),
        "w_ff1": nrm(ks[17], (L, D_MODEL, D_FF), D_MODEL ** -0.5),
        "w_ff3": nrm(ks[18], (L, D_MODEL, D_FF), D_MODEL ** -0.5),
        "w_ff2": nrm(ks[19], (L, D_FF, D_MODEL), D_FF ** -0.5),
        "g_final": 1.0 + nrm(ks[20], (D_MODEL,), 0.05),
    }


def reference(x_prompt, x_sample, state_pool, state_gla, c_prompt, c_sample, w_ada, b_ada, g_norm1,
              w_in, w_pool, pool_scale, w_a2, b_a, g_gla_out, w_out, g_norm2, w_ff1, w_ff3, w_ff2,
              g_final):
    Bp, Tp, _ = x_prompt.shape
    Ts = x_sample.shape[1]
    pos_p = jnp.arange(Tp, dtype=jnp.int32)
    pos_s = PAST_LEN + jnp.arange(Ts, dtype=jnp.int32)
    h_p, h_s = x_prompt, x_sample
    pool_p_list, gla_p_list, pool_s_list, gla_s_list = [], [], [], []
    for l in range(DEPTH):
        lw = (w_ada[l], b_ada[l], g_norm1[l], w_in[l], w_pool[l], pool_scale[l], w_a2[l], b_a[l],
              g_gla_out[l], w_out[l], g_norm2[l], w_ff1[l], w_ff3[l], w_ff2[l])
        pool_prefix0 = jnp.zeros((Bp, POOL_STATE, D_POOL), x_prompt.dtype)
        gla_s00 = jnp.zeros((Bp, GLA_HEADS, DK_HEAD, DV_HEAD), state_gla.dtype)
        h_p, pool_p, gla_p = decoder_layer(h_p, c_prompt, pool_prefix0, gla_s00, pos_p, *lw)
        h_s, pool_s, gla_s = decoder_layer(h_s, c_sample, state_pool[l], state_gla[l], pos_s, *lw)
        pool_p_list.append(pool_p)
        gla_p_list.append(gla_p)
        pool_s_list.append(pool_s.astype(state_pool.dtype))
        gla_s_list.append(gla_s)
    y_prompt = rmsnorm(h_p, g_final)
    y_sample = rmsnorm(h_s, g_final)
    new_pool_prompt = jnp.stack(pool_p_list)
    new_gla_prompt = jnp.stack(gla_p_list)
    new_pool_sample = jnp.stack(pool_s_list)
    new_gla_sample = jnp.stack(gla_s_list)
    return (y_prompt, y_sample, new_pool_prompt, new_gla_prompt, new_pool_sample, new_gla_sample)
```

```python
import functools

import jax
import jax.numpy as jnp
from jax import lax
from jax.experimental import pallas as pl
from jax.experimental.pallas import tpu as pltpu

F32 = jnp.float32
BF16 = jnp.bfloat16

POOL_WINDOWS = (2, 4, 8, 16)
POOL_STATE = max(POOL_WINDOWS) - 1
GLA_HEADS = 4
GATE_RANK = 16
GATE_NORM = 16.0
N_MOD = 6
EPS = 1e-6
PAST_LEN = 16384

LANES = 128
VMEM_LIMIT = 56 * 1024 * 1024
GLA_CHUNK = 128
GLA_SUB = 16


def _params(n_axes):
    return pltpu.CompilerParams(dimension_semantics=("arbitrary",) * n_axes, vmem_limit_bytes=VMEM_LIMIT)


def _silu(x):
    return x * jax.nn.sigmoid(x)


def _log_sigmoid(x):
    return jnp.minimum(x, 0.0) - jnp.log(1.0 + jnp.exp(-jnp.abs(x)))


def _ada_body(c_ref, w_ref, b_ref, o_ref):
    s = _silu(c_ref[...]).astype(BF16)
    o_ref[...] = jnp.dot(s, w_ref[...].astype(BF16), preferred_element_type=F32) + b_ref[...]


def _ada(c, w, b, tn=1024):
    m, d = c.shape
    n = w.shape[1]
    return pl.pallas_call(
        _ada_body,
        grid=(n // tn,),
        in_specs=[pl.BlockSpec((m, d), lambda j: (0, 0)),
                  pl.BlockSpec((d, tn), lambda j: (0, j)),
                  pl.BlockSpec((1, tn), lambda j: (0, j))],
        out_specs=pl.BlockSpec((m, tn), lambda j: (0, j)),
        out_shape=jax.ShapeDtypeStruct((m, n), F32),
        compiler_params=_params(1),
        name="ada",
    )(c, w, b)


def _normmod_body(x_ref, g_ref, sc_ref, sh_ref, o_ref):
    x = x_ref[...]
    r = lax.rsqrt(jnp.mean(x * x, axis=-1, keepdims=True) + EPS)
    y = x * r * g_ref[...]
    o_ref[...] = (y * (1.0 + sc_ref[...]) + sh_ref[...]).astype(o_ref.dtype)


def _normmod(x, gain, mod, sc_chunk, sh_chunk, tm, tpb):
    m, d = x.shape
    r = mod.shape[1]
    return pl.pallas_call(
        _normmod_body,
        grid=(m // tm,),
        in_specs=[pl.BlockSpec((tm, d), lambda i: (i, 0)),
                  pl.BlockSpec((1, d), lambda i: (0, 0)),
                  pl.BlockSpec((None, r, d), lambda i: (i // tpb, 0, sc_chunk)),
                  pl.BlockSpec((None, r, d), lambda i: (i // tpb, 0, sh_chunk))],
        out_specs=pl.BlockSpec((tm, d), lambda i: (i, 0)),
        out_shape=jax.ShapeDtypeStruct((m, d), BF16),
        compiler_params=_params(1),
        name="normmod",
    )(x, gain, mod, mod)


def _norm_body(x_ref, g_ref, o_ref):
    x = x_ref[...]
    r = lax.rsqrt(jnp.mean(x * x, axis=-1, keepdims=True) + EPS)
    o_ref[...] = x * r * g_ref[...]


def _norm(x, gain, tm):
    m, d = x.shape
    return pl.pallas_call(
        _norm_body,
        grid=(m // tm,),
        in_specs=[pl.BlockSpec((tm, d), lambda i: (i, 0)),
                  pl.BlockSpec((1, d), lambda i: (0, 0))],
        out_specs=pl.BlockSpec((tm, d), lambda i: (i, 0)),
        out_shape=jax.ShapeDtypeStruct((m, d), F32),
        compiler_params=_params(1),
        name="finalnorm",
    )(x, gain)


def _cast_once(w_ref, wbf_ref):
    @pl.when(pl.program_id(1) == 0)
    def _():
        wbf_ref[...] = w_ref[...].astype(BF16)


def _proj_body(a_ref, w_ref, o_ref, wbf_ref):
    _cast_once(w_ref, wbf_ref)
    o_ref[...] = jnp.dot(a_ref[...], wbf_ref[...], preferred_element_type=F32).astype(o_ref.dtype)


def _proj(a, w, col0, ncols, tm, tn, out_dtype=F32):
    m, kdim = a.shape
    off = col0 // tn
    return pl.pallas_call(
        _proj_body,
        grid=(ncols // tn, m // tm),
        in_specs=[pl.BlockSpec((tm, kdim), lambda j, i: (i, 0)),
                  pl.BlockSpec((kdim, tn), lambda j, i: (0, j + off))],
        out_specs=pl.BlockSpec((tm, tn), lambda j, i: (i, j)),
        out_shape=jax.ShapeDtypeStruct((m, ncols), out_dtype),
        scratch_shapes=[pltpu.VMEM((kdim, tn), BF16)],
        compiler_params=_params(2),
        name="proj",
    )(a, w)


def _outproj_body(py_ref, o_ref, w_ref, x_ref, gt_ref, out_ref, wbf_ref, *, d_pool):
    _cast_once(w_ref, wbf_ref)
    acc = jnp.dot(py_ref[...], wbf_ref[:d_pool, :], preferred_element_type=F32)
    acc += jnp.dot(o_ref[...], wbf_ref[d_pool:, :], preferred_element_type=F32)
    out_ref[...] = x_ref[...] + gt_ref[...] * acc


def _outproj(py, o, w, x, mod, gt_chunk, tm, tn, tpb):
    m, d_pool = py.shape
    d_v = o.shape[1]
    kdim, n = w.shape
    r = mod.shape[1]
    per = n // tn
    return pl.pallas_call(
        functools.partial(_outproj_body, d_pool=d_pool),
        grid=(n // tn, m // tm),
        in_specs=[pl.BlockSpec((tm, d_pool), lambda j, i: (i, 0)),
                  pl.BlockSpec((tm, d_v), lambda j, i: (i, 0)),
                  pl.BlockSpec((kdim, tn), lambda j, i: (0, j)),
                  pl.BlockSpec((tm, tn), lambda j, i: (i, j)),
                  pl.BlockSpec((None, r, tn), lambda j, i: (i // tpb, 0, gt_chunk * per + j))],
        out_specs=pl.BlockSpec((tm, tn), lambda j, i: (i, j)),
        out_shape=jax.ShapeDtypeStruct((m, n), F32),
        scratch_shapes=[pltpu.VMEM((kdim, tn), BF16)],
        compiler_params=_params(2),
        name="outproj",
    )(py, o, w, x, mod)


def _ffn1_body(h_ref, w1_ref, w3_ref, o_ref, w1bf_ref, w3bf_ref):
    _cast_once(w1_ref, w1bf_ref)
    _cast_once(w3_ref, w3bf_ref)
    h = h_ref[...]
    a = jnp.dot(h, w1bf_ref[...], preferred_element_type=F32)
    b = jnp.dot(h, w3bf_ref[...], preferred_element_type=F32)
    o_ref[...] = (_silu(a) * b).astype(o_ref.dtype)


def _ffn1(h, w1, w3, tm, tn):
    m, kdim = h.shape
    n = w1.shape[1]
    return pl.pallas_call(
        _ffn1_body,
        grid=(n // tn, m // tm),
        in_specs=[pl.BlockSpec((tm, kdim), lambda j, i: (i, 0)),
                  pl.BlockSpec((kdim, tn), lambda j, i: (0, j)),
                  pl.BlockSpec((kdim, tn), lambda j, i: (0, j))],
        out_specs=pl.BlockSpec((tm, tn), lambda j, i: (i, j)),
        out_shape=jax.ShapeDtypeStruct((m, n), BF16),
        scratch_shapes=[pltpu.VMEM((kdim, tn), BF16), pltpu.VMEM((kdim, tn), BF16)],
        compiler_params=_params(2),
        name="ffn1",
    )(h, w1, w3)


def _ffn2_body(a_ref, w_ref, x_ref, gt_ref, out_ref, wbf_ref):
    _cast_once(w_ref, wbf_ref)
    acc = jnp.dot(a_ref[...], wbf_ref[...], preferred_element_type=F32)
    out_ref[...] = x_ref[...] + gt_ref[...] * acc


def _ffn2(a, w, x, mod, gt_chunk, tm, tn, tpb):
    m, kdim = a.shape
    n = w.shape[1]
    r = mod.shape[1]
    per = n // tn
    return pl.pallas_call(
        _ffn2_body,
        grid=(n // tn, m // tm),
        in_specs=[pl.BlockSpec((tm, kdim), lambda j, i: (i, 0)),
                  pl.BlockSpec((kdim, tn), lambda j, i: (0, j)),
                  pl.BlockSpec((tm, tn), lambda j, i: (i, j)),
                  pl.BlockSpec((None, r, tn), lambda j, i: (i // tpb, 0, gt_chunk * per + j))],
        out_specs=pl.BlockSpec((tm, tn), lambda j, i: (i, j)),
        out_shape=jax.ShapeDtypeStruct((m, n), F32),
        scratch_shapes=[pltpu.VMEM((kdim, tn), BF16)],
        compiler_params=_params(2),
        name="ffn2",
    )(a, w, x, mod)


def _pool_group_matmul(d, wp_ref, ps_ref, n_groups, gw):
    outs = []
    for gi in range(n_groups):
        dg = d[:, gi * gw:(gi + 1) * gw].astype(BF16)
        outs.append(jnp.dot(dg, wp_ref[gi].astype(BF16), preferred_element_type=F32))
    return jnp.concatenate(outs, axis=-1) * ps_ref[...]


def _pool_prompt_body(u_ref, wp_ref, ps_ref, y_ref, st_ref, buf_ref, *, tm, n_tiles):
    t = pl.program_id(1)
    halo = POOL_STATE + 1
    n_groups = len(POOL_WINDOWS)
    gw = u_ref.shape[1] // n_groups

    @pl.when(t == 0)
    def _():
        buf_ref[:halo, :] = jnp.zeros((halo, buf_ref.shape[1]), F32)

    u = u_ref[...]
    buf_ref[halo:, :] = u
    pos = t * tm + lax.broadcasted_iota(jnp.int32, (tm, 1), 0)
    parts = []
    for gi, w in enumerate(POOL_WINDOWS):
        lo, hi = gi * gw, (gi + 1) * gw
        win = u[:, lo:hi]
        for s in range(1, w):
            win = win + buf_ref[halo - s:halo - s + tm, lo:hi]
        cnt = jnp.minimum(w, pos + 1).astype(F32)
        parts.append(win / cnt - u[:, lo:hi])
    d = jnp.concatenate(parts, axis=-1)
    y_ref[...] = _pool_group_matmul(d, wp_ref, ps_ref, n_groups, gw).astype(y_ref.dtype)

    @pl.when(t == n_tiles - 1)
    def _():
        st_ref[...] = buf_ref[halo + tm - POOL_STATE:, :]

    buf_ref[:halo, :] = buf_ref[tm:, :]


def _pool_prompt(u, w_pool, pool_scale, nb, t_len, tm):
    m, dp = u.shape
    n_tiles = t_len // tm
    g, gw, _ = w_pool.shape
    return pl.pallas_call(
        functools.partial(_pool_prompt_body, tm=tm, n_tiles=n_tiles),
        grid=(nb, n_tiles),
        in_specs=[pl.BlockSpec((tm, dp), lambda b, t: (b * n_tiles + t, 0)),
                  pl.BlockSpec((g, gw, gw), lambda b, t: (0, 0, 0)),
                  pl.BlockSpec((1, dp), lambda b, t: (0, 0))],
        out_specs=[pl.BlockSpec((tm, dp), lambda b, t: (b * n_tiles + t, 0)),
                   pl.BlockSpec((None, POOL_STATE, dp), lambda b, t: (b, 0, 0))],
        out_shape=[jax.ShapeDtypeStruct((m, dp), BF16),
                   jax.ShapeDtypeStruct((nb, POOL_STATE, dp), F32)],
        scratch_shapes=[pltpu.VMEM((POOL_STATE + 1 + tm, dp), F32)],
        compiler_params=_params(2),
        name="pool_prompt",
    )(u, w_pool, pool_scale)


def _pool_sample_body(u_ref, sp_ref, wp_ref, ps_ref, y_ref, st_ref):
    n_groups = len(POOL_WINDOWS)
    gw = u_ref.shape[1] // n_groups
    u = u_ref[...]
    parts = []
    for gi, w in enumerate(POOL_WINDOWS):
        lo, hi = gi * gw, (gi + 1) * gw
        win = u[:, lo:hi]
        for s in range(1, w):
            win = win + sp_ref[:, POOL_STATE - s, lo:hi]
        cnt = float(min(w, PAST_LEN + 1))
        parts.append(win / cnt - u[:, lo:hi])
    d = jnp.concatenate(parts, axis=-1)
    y_ref[...] = _pool_group_matmul(d, wp_ref, ps_ref, n_groups, gw).astype(y_ref.dtype)
    for r in range(POOL_STATE - 1):
        st_ref[:, r, :] = sp_ref[:, r + 1, :]
    st_ref[:, POOL_STATE - 1, :] = u


def _pool_sample(u, state, w_pool, pool_scale, tb):
    nb, dp = u.shape
    g, gw, _ = w_pool.shape
    return pl.pallas_call(
        _pool_sample_body,
        grid=(nb // tb,),
        in_specs=[pl.BlockSpec((tb, dp), lambda b: (b, 0)),
                  pl.BlockSpec((tb, POOL_STATE, dp), lambda b: (b, 0, 0)),
                  pl.BlockSpec((g, gw, gw), lambda b: (0, 0, 0)),
                  pl.BlockSpec((1, dp), lambda b: (0, 0))],
        out_specs=[pl.BlockSpec((tb, dp), lambda b: (b, 0)),
                   pl.BlockSpec((tb, POOL_STATE, dp), lambda b: (b, 0, 0))],
        out_shape=[jax.ShapeDtypeStruct((nb, dp), BF16),
                   jax.ShapeDtypeStruct((nb, POOL_STATE, dp), F32)],
        compiler_params=_params(1),
        name="pool_sample",
    )(u, state, w_pool, pool_scale)


def _head_out(o, g, gain):
    r = lax.rsqrt(jnp.mean(o * o, axis=-1, keepdims=True) + EPS)
    return _silu(g) * (o * r * gain)


def _gla_prompt_body(q_ref, k_ref, v_ref, g_ref, alr_ref, wa_ref, ba_ref, gn_ref, o_ref, s_out_ref,
                     st_ref, b_scr, q_scr, k_scr, a_scr, *, n_chunks):
    c = pl.program_id(2)
    rows, dk = q_ref.shape

    @pl.when(c == 0)
    def _():
        st_ref[...] = jnp.zeros(st_ref.shape, F32)

    q = q_ref[...] * (dk ** -0.5)
    k = k_ref[...]
    vb = v_ref[...].astype(BF16)
    x = jnp.dot(alr_ref[...].astype(BF16), wa_ref[...].astype(BF16), preferred_element_type=F32) + ba_ref[...]
    la = _log_sigmoid(x) / GATE_NORM
    tri = (lax.broadcasted_iota(jnp.int32, (rows, rows), 0)
           >= lax.broadcasted_iota(jnp.int32, (rows, rows), 1)).astype(F32)
    b = jnp.dot(tri, la, precision=lax.Precision.HIGHEST, preferred_element_type=F32)
    b_scr[...] = b
    q_scr[...] = q
    k_scr[...] = k

    def block(i, carry):
        r0 = pl.multiple_of(i * GLA_SUB, GLA_SUB)
        b_i = b_scr[pl.ds(r0, GLA_SUB), :]
        q_i = q_scr[pl.ds(r0, GLA_SUB), :]
        b_first = b_scr[pl.ds(r0, 1), :]
        qt = (q_i * jnp.exp(b_i - b_first)).astype(BF16)
        kt = (k_scr[...] * jnp.exp(jnp.minimum(b_first - b_scr[...], 0.0))).astype(BF16)
        panel = lax.dot_general(qt, kt, (((1,), (1,)), ((), ())), preferred_element_type=F32)
        col = lax.broadcasted_iota(jnp.int32, (GLA_SUB, rows), 1)
        row = lax.broadcasted_iota(jnp.int32, (GLA_SUB, rows), 0) + r0
        diag = jnp.zeros((GLA_SUB, rows), F32)
        for j in range(GLA_SUB):
            b_j = b_scr[pl.ds(r0 + j, 1), :]
            k_j = k_scr[pl.ds(r0 + j, 1), :]
            p = q_i * k_j * jnp.exp(jnp.minimum(b_i - b_j, 0.0))
            s = jnp.sum(p, axis=-1, keepdims=True)
            diag = jnp.where(col == r0 + j, s, diag)
        a_scr[pl.ds(r0, GLA_SUB), :] = jnp.where(col < r0, panel, jnp.where(col <= row, diag, 0.0))
        return carry

    lax.fori_loop(0, rows // GLA_SUB, block, 0)

    st = st_ref[...]
    qe = (q * jnp.exp(b)).astype(BF16)
    o = lax.dot_general(qe, st.astype(BF16), (((1,), (1,)), ((), ())), preferred_element_type=F32)
    o += jnp.dot(a_scr[...].astype(BF16), vb, preferred_element_type=F32)
    o_ref[...] = _head_out(o, g_ref[...], gn_ref[...]).astype(o_ref.dtype)

    b_last = b[rows - 1:rows, :]
    kd = (k * jnp.exp(b_last - b)).astype(BF16)
    st_new = jnp.exp(b_last) * st + lax.dot_general(vb, kd, (((0,), (0,)), ((), ())),
                                                    preferred_element_type=F32)
    st_ref[...] = st_new

    @pl.when(c == n_chunks - 1)
    def _():
        s_out_ref[...] = st_new.T


def _gla_prompt(q, k, v, g, alr, wa, ba, gn, nb, t_len):
    dk = q.shape[1] // GLA_HEADS
    dv = v.shape[1] // GLA_HEADS
    n_chunks = t_len // GLA_CHUNK
    c = GLA_CHUNK
    row = lambda b, h, i: (b * n_chunks + i, h)
    return pl.pallas_call(
        functools.partial(_gla_prompt_body, n_chunks=n_chunks),
        grid=(nb, GLA_HEADS, n_chunks),
        in_specs=[pl.BlockSpec((c, dk), row),
                  pl.BlockSpec((c, dk), row),
                  pl.BlockSpec((c, dv), row),
                  pl.BlockSpec((c, dv), row),
                  pl.BlockSpec((c, LANES), lambda b, h, i: (b * n_chunks + i, 0)),
                  pl.BlockSpec((LANES, dk), lambda b, h, i: (0, h)),
                  pl.BlockSpec((1, dk), lambda b, h, i: (0, h)),
                  pl.BlockSpec((None, 1, dv), lambda b, h, i: (h, 0, 0))],
        out_specs=[pl.BlockSpec((c, dv), row),
                   pl.BlockSpec((None, None, dk, dv), lambda b, h, i: (b, h, 0, 0))],
        out_shape=[jax.ShapeDtypeStruct((nb * t_len, GLA_HEADS * dv), BF16),
                   jax.ShapeDtypeStruct((nb, GLA_HEADS, dk, dv), F32)],
        scratch_shapes=[pltpu.VMEM((dv, dk), F32),
                        pltpu.VMEM((c, dk), F32), pltpu.VMEM((c, dk), F32), pltpu.VMEM((c, dk), F32),
                        pltpu.VMEM((c, c), F32)],
        compiler_params=_params(3),
        name="gla_prompt",
    )(q, k, v, g, alr, wa, ba, gn)


def _column(row_vec):
    return jnp.broadcast_to(row_vec, (LANES, row_vec.shape[1])).T[:, :1]


def _gla_sample_body(q_ref, k_ref, v_ref, g_ref, alr_ref, wa_ref, ba_ref, gn_ref, s_ref, o_ref, s_out_ref):
    dk = q_ref.shape[1] // GLA_HEADS
    dv = v_ref.shape[1] // GLA_HEADS
    alr = alr_ref[...].astype(BF16)
    for h in range(GLA_HEADS):
        ks, vs = slice(h * dk, (h + 1) * dk), slice(h * dv, (h + 1) * dv)
        q = q_ref[:, ks] * (dk ** -0.5)
        kb = k_ref[:, ks].astype(BF16).astype(F32)
        vb = v_ref[:, vs].astype(BF16).astype(F32)
        x = jnp.dot(alr, wa_ref[:, ks].astype(BF16), preferred_element_type=F32) + ba_ref[:, ks]
        a = jnp.exp(_log_sigmoid(x) / GATE_NORM)
        s_old = s_ref[h]
        o = jnp.dot((q * a).astype(BF16), s_old.astype(BF16), preferred_element_type=F32)
        o += jnp.sum(q * kb, axis=-1, keepdims=True) * vb
        o_ref[:, vs] = _head_out(o, g_ref[:, vs], gn_ref[h]).astype(o_ref.dtype)
        s_out_ref[h] = _column(a) * s_old + _column(kb) * vb


def _gla_sample(q, k, v, g, alr, wa, ba, gn, state):
    nb = q.shape[0]
    dkk, dvv = q.shape[2], v.shape[2]
    dk, dv = dkk // GLA_HEADS, dvv // GLA_HEADS
    tok = lambda w: pl.BlockSpec((None, 1, w), lambda b: (b, 0, 0))
    st = pl.BlockSpec((None, GLA_HEADS, dk, dv), lambda b: (b, 0, 0, 0))
    return pl.pallas_call(
        _gla_sample_body,
        grid=(nb,),
        in_specs=[tok(dkk), tok(dkk), tok(dvv), tok(dvv), tok(LANES),
                  pl.BlockSpec((LANES, dkk), lambda b: (0, 0)),
                  pl.BlockSpec((1, dkk), lambda b: (0, 0)),
                  pl.BlockSpec((GLA_HEADS, 1, dv), lambda b: (0, 0, 0)),
                  st],
        out_specs=[tok(dvv), st],
        out_shape=[jax.ShapeDtypeStruct((nb, 1, dvv), BF16),
                   jax.ShapeDtypeStruct(state.shape, F32)],
        compiler_params=_params(1),
        name="gla_sample",
    )(q, k, v, g, alr, wa, ba, gn, state)


def _layer(x, mod, lw, dims, *, tiles, nb, t_len, pool_state=None, gla_state=None):
    (g1, w_in, w_alr, w_pool, pool_scale, wa, ba, gn, w_out, g2, w_ff1, w_ff3, w_ff2) = lw
    d_pool, d_k, d_v = dims
    tm, tm_small, tpb, tpb_small = tiles
    h1 = _normmod(x, g1, mod, 1, 0, tm_small, tpb_small)
    tn = 512
    u = _proj(h1, w_in, 0, d_pool, tm, tn)
    q = _proj(h1, w_in, d_pool, d_k, tm, tn)
    k = _proj(h1, w_in, d_pool + d_k, d_k, tm, tn)
    v = _proj(h1, w_in, d_pool + 2 * d_k, d_v, tm, tn)
    g = _proj(h1, w_in, d_pool + 2 * d_k + d_v, d_v, tm, tn)
    alr = _proj(h1, w_alr, 0, LANES, tm, LANES)
    if pool_state is None:
        py, pool_new = _pool_prompt(u, w_pool, pool_scale, nb, t_len, tm_small)
        o, gla_new = _gla_prompt(q, k, v, g, alr, wa, ba, gn, nb, t_len)
    else:
        py, pool_new = _pool_sample(u, pool_state, w_pool, pool_scale, 32)
        r3 = lambda t: t.reshape(t.shape[0], 1, t.shape[1])
        o, gla_new = _gla_sample(r3(q), r3(k), r3(v), r3(g), r3(alr), wa, ba, gn, gla_state)
        o = o.reshape(o.shape[0], o.shape[2])
    x1 = _outproj(py, o, w_out, x, mod, 2, tm, tn, tpb)
    h2 = _normmod(x1, g2, mod, 4, 3, tm_small, tpb_small)
    act = _ffn1(h2, w_ff1, w_ff3, tm, 256)
    x2 = _ffn2(act, w_ff2, x1, mod, 5, tm_small, 256, tpb_small)
    return x2, pool_new, gla_new


def kernel(x_prompt, x_sample, state_pool, state_gla, c_prompt, c_sample, w_ada, b_ada, g_norm1, w_in,
           w_pool, pool_scale, w_a2, b_a, g_gla_out, w_out, g_norm2, w_ff1, w_ff3, w_ff2, g_final):
    bp, tp, d = x_prompt.shape
    bs, ts, _ = x_sample.shape
    depth = w_ada.shape[0]
    assert ts == 1, "the decode path handles one new token per sequence"
    d_pool = w_pool.shape[1] * w_pool.shape[2]
    d_k = w_a2.shape[2]
    d_v = g_gla_out.shape[1] * g_gla_out.shape[2]
    dims = (d_pool, d_k, d_v)
    d_main = d_pool + 2 * d_k + 2 * d_v

    n_c = bp + bs
    pad = -n_c % 16
    c_all = jnp.concatenate([c_prompt, c_sample, jnp.zeros((pad, d), F32)], axis=0)

    hp = x_prompt.reshape(bp * tp, d)
    hs = x_sample.reshape(bs, d)
    pools_p, glas_p, pools_s, glas_s = [], [], [], []
    tm_p, tms_p = 1024, 256
    for l in range(depth):
        mod = _ada(c_all, w_ada[l], b_ada[l].reshape(1, -1))
        mod_p = mod[:bp].reshape(bp, 1, N_MOD * d)
        mod_s = mod[bp:n_c].reshape(1, bs, N_MOD * d)
        w_alr = jnp.pad(w_in[l][:, d_main:], ((0, 0), (0, LANES - GATE_RANK)))
        wa = jnp.pad(w_a2[l], ((0, LANES - GATE_RANK), (0, 0)))
        lw = (g_norm1[l].reshape(1, d), w_in[l], w_alr, w_pool[l], pool_scale[l].reshape(1, -1), wa,
              b_a[l].reshape(1, -1), g_gla_out[l].reshape(GLA_HEADS, 1, -1), w_out[l],
              g_norm2[l].reshape(1, d), w_ff1[l], w_ff3[l], w_ff2[l])
        hp, pool_p, gla_p = _layer(hp, mod_p, lw, dims, tiles=(tm_p, tms_p, tp // tm_p, tp // tms_p),
                                   nb=bp, t_len=tp)
        hs, pool_s, gla_s = _layer(hs, mod_s, lw, dims, tiles=(bs, bs, 1, 1), nb=bs, t_len=1,
                                   pool_state=state_pool[l], gla_state=state_gla[l])
        pools_p.append(pool_p)
        glas_p.append(gla_p)
        pools_s.append(pool_s)
        glas_s.append(gla_s)
    gf = g_final.reshape(1, d)
    y_p = _norm(hp, gf, tms_p).reshape(bp, tp, d)
    y_s = _norm(hs, gf, bs).reshape(bs, ts, d)
    return (y_p, y_s, jnp.stack(pools_p), jnp.stack(glas_p), jnp.stack(pools_s), jnp.stack(glas_s))
```

```python
import functools

import jax
import jax.numpy as jnp
from jax import lax
from jax.experimental import pallas as pl
from jax.experimental.pallas import tpu as pltpu

F32 = jnp.float32
BF16 = jnp.bfloat16

POOL_WINDOWS = (2, 4, 8, 16)
POOL_STATE = max(POOL_WINDOWS) - 1
GLA_HEADS = 4
GATE_RANK = 16
GATE_NORM = 16.0
N_MOD = 6
EPS = 1e-6
PAST_LEN = 16384

LANES = 128
SUBLANES = 8
VMEM_LIMIT = 56 * 1024 * 1024
GLA_CHUNK = 256
GLA_SUB = 16
GLA_SAFE_DECAY = 60.0

NT = (((1,), (1,)), ((), ()))
TN = (((0,), (0,)), ((), ()))


def _params(n_axes):
    return pltpu.CompilerParams(dimension_semantics=("arbitrary",) * n_axes, vmem_limit_bytes=VMEM_LIMIT)


def _silu(x):
    return x * jax.nn.sigmoid(x)


def _log_sigmoid(x):
    return jnp.minimum(x, 0.0) - jnp.log(1.0 + jnp.exp(-jnp.abs(x)))


def _dot(a, b, dims=None):
    if dims is None:
        return jnp.dot(a, b, preferred_element_type=F32)
    return lax.dot_general(a, b, dims, preferred_element_type=F32)


def _ada_body(c_ref, w_ref, b_ref, o_ref):
    s = _silu(c_ref[...]).astype(BF16)
    o_ref[...] = _dot(s, w_ref[...].astype(BF16)) + b_ref[...]


def _ada(c, w, b, tn=1024):
    m, d = c.shape
    n = w.shape[1]
    return pl.pallas_call(
        _ada_body,
        grid=(n // tn,),
        in_specs=[pl.BlockSpec((m, d), lambda j: (0, 0)),
                  pl.BlockSpec((d, tn), lambda j: (0, j)),
                  pl.BlockSpec((1, tn), lambda j: (0, j))],
        out_specs=pl.BlockSpec((m, tn), lambda j: (0, j)),
        out_shape=jax.ShapeDtypeStruct((m, n), F32),
        compiler_params=_params(1),
        name="ada",
    )(c, w, b)


def _prompt_mod_spec(n_dec, width, col_of):
    return pl.BlockSpec((SUBLANES, width), lambda *g: (n_dec // SUBLANES, col_of(*g)))


def _decode_mod_spec(n_dec, width, col_of):
    return pl.BlockSpec((n_dec, width), lambda *g: (0, col_of(*g)))


def _rms(x, gain):
    return x * lax.rsqrt(jnp.mean(x * x, axis=-1, keepdims=True) + EPS) * gain


def _normmod_body(x_ref, g_ref, sc_ref, sh_ref, o_ref, *, tpb):
    if tpb is None:
        sc, sh = sc_ref[...], sh_ref[...]
    else:
        b = pl.program_id(0) // tpb
        sc, sh = sc_ref[pl.ds(b, 1), :], sh_ref[pl.ds(b, 1), :]
    o_ref[...] = (_rms(x_ref[...], g_ref[...]) * (1.0 + sc) + sh).astype(o_ref.dtype)


def _normmod(x, gain, mod, n_dec, sc_chunk, sh_chunk, tm, tpb):
    m, d = x.shape
    spec = _decode_mod_spec if tpb is None else _prompt_mod_spec
    return pl.pallas_call(
        functools.partial(_normmod_body, tpb=tpb),
        grid=(m // tm,),
        in_specs=[pl.BlockSpec((tm, d), lambda i: (i, 0)),
                  pl.BlockSpec((1, d), lambda i: (0, 0)),
                  spec(n_dec, d, lambda i: sc_chunk),
                  spec(n_dec, d, lambda i: sh_chunk)],
        out_specs=pl.BlockSpec((tm, d), lambda i: (i, 0)),
        out_shape=jax.ShapeDtypeStruct((m, d), BF16),
        compiler_params=_params(1),
        name="normmod",
    )(x, gain, mod, mod)


def _norm_body(x_ref, g_ref, o_ref):
    o_ref[...] = _rms(x_ref[...], g_ref[...])


def _norm(x, gain, tm):
    m, d = x.shape
    return pl.pallas_call(
        _norm_body,
        grid=(m // tm,),
        in_specs=[pl.BlockSpec((tm, d), lambda i: (i, 0)),
                  pl.BlockSpec((1, d), lambda i: (0, 0))],
        out_specs=pl.BlockSpec((tm, d), lambda i: (i, 0)),
        out_shape=jax.ShapeDtypeStruct((m, d), F32),
        compiler_params=_params(1),
        name="finalnorm",
    )(x, gain)


def _cast_once(w_ref, wbf_ref):
    @pl.when(pl.program_id(1) == 0)
    def _():
        wbf_ref[...] = w_ref[...].astype(BF16)


def _row_steps(mt, prompt_fn, decode_fn):
    i = pl.program_id(1)
    pl.when(i < mt)(prompt_fn)
    pl.when(i == mt)(decode_fn)


def _prow(mt):
    return lambda j, i: jnp.minimum(i, mt - 1)


def _proj_body(ap_ref, ad_ref, wt_ref, op_ref, od_ref, wbf_ref, *, mt):
    _cast_once(wt_ref, wbf_ref)

    def prompt():
        op_ref[...] = _dot(ap_ref[...], wbf_ref[...], NT).astype(op_ref.dtype)

    def decode():
        od_ref[...] = _dot(ad_ref[...], wbf_ref[...], NT).astype(od_ref.dtype)

    _row_steps(mt, prompt, decode)


def _proj(ap, ad, wt, row0, nrows, tm, tn, out_dtype):
    mp, kdim = ap.shape
    md = ad.shape[0]
    mt = mp // tm
    off = row0 // tn
    pr = _prow(mt)
    return pl.pallas_call(
        functools.partial(_proj_body, mt=mt),
        grid=(nrows // tn, mt + 1),
        in_specs=[pl.BlockSpec((tm, kdim), lambda j, i: (pr(j, i), 0)),
                  pl.BlockSpec((md, kdim), lambda j, i: (0, 0)),
                  pl.BlockSpec((tn, kdim), lambda j, i: (j + off, 0))],
        out_specs=[pl.BlockSpec((tm, tn), lambda j, i: (pr(j, i), j)),
                   pl.BlockSpec((md, tn), lambda j, i: (0, j))],
        out_shape=[jax.ShapeDtypeStruct((mp, nrows), out_dtype),
                   jax.ShapeDtypeStruct((md, nrows), out_dtype)],
        scratch_shapes=[pltpu.VMEM((tn, kdim), BF16)],
        compiler_params=_params(2),
        name="proj",
    )(ap, ad, wt)


def _prompt_seq(mt, tpb):
    return jnp.minimum(pl.program_id(1), mt - 1) // tpb


def _outproj_body(pyp_ref, op_ref, pyd_ref, od_ref, w_ref, xp_ref, xd_ref, gp_ref, gd_ref,
                  outp_ref, outd_ref, wbf_ref, *, mt, tpb, d_pool):
    _cast_once(w_ref, wbf_ref)
    seq = _prompt_seq(mt, tpb)

    def mix(py_ref, o_ref):
        return _dot(py_ref[...], wbf_ref[:d_pool, :]) + _dot(o_ref[...], wbf_ref[d_pool:, :])

    def prompt():
        outp_ref[...] = xp_ref[...] + gp_ref[pl.ds(seq, 1), :] * mix(pyp_ref, op_ref)

    def decode():
        outd_ref[...] = xd_ref[...] + gd_ref[...] * mix(pyd_ref, od_ref)

    _row_steps(mt, prompt, decode)


def _outproj(pyp, op, pyd, od, w, xp, xd, mod, gt_chunk, tm, tn, tpb):
    mp, d_pool = pyp.shape
    md = pyd.shape[0]
    d_v = op.shape[1]
    kdim, n = w.shape
    mt = mp // tm
    per = n // tn
    pr = _prow(mt)
    gcol = lambda j, i: gt_chunk * per + j
    return pl.pallas_call(
        functools.partial(_outproj_body, mt=mt, tpb=tpb, d_pool=d_pool),
        grid=(per, mt + 1),
        in_specs=[pl.BlockSpec((tm, d_pool), lambda j, i: (pr(j, i), 0)),
                  pl.BlockSpec((tm, d_v), lambda j, i: (pr(j, i), 0)),
                  pl.BlockSpec((md, d_pool), lambda j, i: (0, 0)),
                  pl.BlockSpec((md, d_v), lambda j, i: (0, 0)),
                  pl.BlockSpec((kdim, tn), lambda j, i: (0, j)),
                  pl.BlockSpec((tm, tn), lambda j, i: (pr(j, i), j)),
                  pl.BlockSpec((md, tn), lambda j, i: (0, j)),
                  _prompt_mod_spec(md, tn, gcol),
                  _decode_mod_spec(md, tn, gcol)],
        out_specs=[pl.BlockSpec((tm, tn), lambda j, i: (pr(j, i), j)),
                   pl.BlockSpec((md, tn), lambda j, i: (0, j))],
        out_shape=[jax.ShapeDtypeStruct((mp, n), F32), jax.ShapeDtypeStruct((md, n), F32)],
        scratch_shapes=[pltpu.VMEM((kdim, tn), BF16)],
        compiler_params=_params(2),
        name="outproj",
    )(pyp, op, pyd, od, w, xp, xd, mod, mod)


def _ffn1_body(hp_ref, hd_ref, w1_ref, w3_ref, op_ref, od_ref, w1bf_ref, w3bf_ref, *, mt):
    _cast_once(w1_ref, w1bf_ref)
    _cast_once(w3_ref, w3bf_ref)

    def swiglu(h_ref, o_ref):
        h = h_ref[...]
        o_ref[...] = (_silu(_dot(h, w1bf_ref[...])) * _dot(h, w3bf_ref[...])).astype(o_ref.dtype)

    _row_steps(mt, functools.partial(swiglu, hp_ref, op_ref), functools.partial(swiglu, hd_ref, od_ref))


def _ffn1(hp, hd, w1, w3, tm, tn):
    mp, kdim = hp.shape
    md = hd.shape[0]
    n = w1.shape[1]
    mt = mp // tm
    pr = _prow(mt)
    wspec = pl.BlockSpec((kdim, tn), lambda j, i: (0, j))
    return pl.pallas_call(
        functools.partial(_ffn1_body, mt=mt),
        grid=(n // tn, mt + 1),
        in_specs=[pl.BlockSpec((tm, kdim), lambda j, i: (pr(j, i), 0)),
                  pl.BlockSpec((md, kdim), lambda j, i: (0, 0)),
                  wspec, wspec],
        out_specs=[pl.BlockSpec((tm, tn), lambda j, i: (pr(j, i), j)),
                   pl.BlockSpec((md, tn), lambda j, i: (0, j))],
        out_shape=[jax.ShapeDtypeStruct((mp, n), BF16), jax.ShapeDtypeStruct((md, n), BF16)],
        scratch_shapes=[pltpu.VMEM((kdim, tn), BF16), pltpu.VMEM((kdim, tn), BF16)],
        compiler_params=_params(2),
        name="ffn1",
    )(hp, hd, w1, w3)


def _ffn2_body(ap_ref, ad_ref, w_ref, xp_ref, xd_ref, gp_ref, gd_ref, outp_ref, outd_ref, wbf_ref, *, mt, tpb):
    _cast_once(w_ref, wbf_ref)
    seq = _prompt_seq(mt, tpb)

    def prompt():
        outp_ref[...] = xp_ref[...] + gp_ref[pl.ds(seq, 1), :] * _dot(ap_ref[...], wbf_ref[...])

    def decode():
        outd_ref[...] = xd_ref[...] + gd_ref[...] * _dot(ad_ref[...], wbf_ref[...])

    _row_steps(mt, prompt, decode)


def _ffn2(ap, ad, w, xp, xd, mod, gt_chunk, tm, tn, tpb):
    mp, kdim = ap.shape
    md = ad.shape[0]
    n = w.shape[1]
    mt = mp // tm
    per = n // tn
    pr = _prow(mt)
    gcol = lambda j, i: gt_chunk * per + j
    return pl.pallas_call(
        functools.partial(_ffn2_body, mt=mt, tpb=tpb),
        grid=(per, mt + 1),
        in_specs=[pl.BlockSpec((tm, kdim), lambda j, i: (pr(j, i), 0)),
                  pl.BlockSpec((md, kdim), lambda j, i: (0, 0)),
                  pl.BlockSpec((kdim, tn), lambda j, i: (0, j), pipeline_mode=pl.Buffered(1)),
                  pl.BlockSpec((tm, tn), lambda j, i: (pr(j, i), j)),
                  pl.BlockSpec((md, tn), lambda j, i: (0, j)),
                  _prompt_mod_spec(md, tn, gcol),
                  _decode_mod_spec(md, tn, gcol)],
        out_specs=[pl.BlockSpec((tm, tn), lambda j, i: (pr(j, i), j)),
                   pl.BlockSpec((md, tn), lambda j, i: (0, j))],
        out_shape=[jax.ShapeDtypeStruct((mp, n), F32), jax.ShapeDtypeStruct((md, n), F32)],
        scratch_shapes=[pltpu.VMEM((kdim, tn), BF16)],
        compiler_params=_params(2),
        name="ffn2",
    )(ap, ad, w, xp, xd, mod, mod)


def _pool_group_matmul(d, wp_ref, ps_ref, n_groups, gw):
    outs = []
    for gi in range(n_groups):
        dg = d[:, gi * gw:(gi + 1) * gw].astype(BF16)
        outs.append(_dot(dg, wp_ref[gi].astype(BF16)))
    return jnp.concatenate(outs, axis=-1) * ps_ref[...]


def _pool_prompt_body(u_ref, wp_ref, ps_ref, y_ref, st_ref, buf_ref, *, tm, n_tiles):
    t = pl.program_id(1)
    halo = POOL_STATE + 1
    n_groups = len(POOL_WINDOWS)
    gw = u_ref.shape[1] // n_groups

    @pl.when(t == 0)
    def _():
        buf_ref[:halo, :] = jnp.zeros((halo, buf_ref.shape[1]), F32)

    u = u_ref[...]
    buf_ref[halo:, :] = u
    pos = t * tm + lax.broadcasted_iota(jnp.int32, (tm, 1), 0)
    parts = []
    for gi, w in enumerate(POOL_WINDOWS):
        lo, hi = gi * gw, (gi + 1) * gw
        win = u[:, lo:hi]
        for s in range(1, w):
            win = win + buf_ref[halo - s:halo - s + tm, lo:hi]
        cnt = jnp.minimum(w, pos + 1).astype(F32)
        parts.append(win / cnt - u[:, lo:hi])
    d = jnp.concatenate(parts, axis=-1)
    y_ref[...] = _pool_group_matmul(d, wp_ref, ps_ref, n_groups, gw).astype(y_ref.dtype)

    @pl.when(t == n_tiles - 1)
    def _():
        st_ref[...] = buf_ref[halo + tm - POOL_STATE:, :]

    buf_ref[:halo, :] = buf_ref[tm:, :]


def _pool_prompt(u, w_pool, pool_scale, nb, t_len, tm):
    m, dp = u.shape
    n_tiles = t_len // tm
    g, gw, _ = w_pool.shape
    return pl.pallas_call(
        functools.partial(_pool_prompt_body, tm=tm, n_tiles=n_tiles),
        grid=(nb, n_tiles),
        in_specs=[pl.BlockSpec((tm, dp), lambda b, t: (b * n_tiles + t, 0)),
                  pl.BlockSpec((g, gw, gw), lambda b, t: (0, 0, 0)),
                  pl.BlockSpec((1, dp), lambda b, t: (0, 0))],
        out_specs=[pl.BlockSpec((tm, dp), lambda b, t: (b * n_tiles + t, 0)),
                   pl.BlockSpec((None, POOL_STATE, dp), lambda b, t: (b, 0, 0))],
        out_shape=[jax.ShapeDtypeStruct((m, dp), BF16),
                   jax.ShapeDtypeStruct((nb, POOL_STATE, dp), F32)],
        scratch_shapes=[pltpu.VMEM((POOL_STATE + 1 + tm, dp), F32)],
        compiler_params=_params(2),
        name="pool_prompt",
    )(u, w_pool, pool_scale)


def _pool_decode_body(u_ref, sp_ref, wp_ref, ps_ref, y_ref, st_ref):
    n_groups = len(POOL_WINDOWS)
    gw = u_ref.shape[1] // n_groups
    u = u_ref[...]
    parts = []
    for gi, w in enumerate(POOL_WINDOWS):
        lo, hi = gi * gw, (gi + 1) * gw
        win = u[:, lo:hi]
        for s in range(1, w):
            win = win + sp_ref[POOL_STATE - s, :, lo:hi]
        cnt = float(min(w, PAST_LEN + 1))
        parts.append(win / cnt - u[:, lo:hi])
    d = jnp.concatenate(parts, axis=-1)
    y_ref[...] = _pool_group_matmul(d, wp_ref, ps_ref, n_groups, gw).astype(y_ref.dtype)
    for r in range(POOL_STATE - 1):
        st_ref[r] = sp_ref[r + 1]
    st_ref[POOL_STATE - 1] = u


def _pool_decode(u, state_t, w_pool, pool_scale, tb):
    nb, dp = u.shape
    g, gw, _ = w_pool.shape
    return pl.pallas_call(
        _pool_decode_body,
        grid=(nb // tb,),
        in_specs=[pl.BlockSpec((tb, dp), lambda b: (b, 0)),
                  pl.BlockSpec((POOL_STATE, tb, dp), lambda b: (0, b, 0)),
                  pl.BlockSpec((g, gw, gw), lambda b: (0, 0, 0)),
                  pl.BlockSpec((1, dp), lambda b: (0, 0))],
        out_specs=[pl.BlockSpec((tb, dp), lambda b: (b, 0)),
                   pl.BlockSpec((POOL_STATE, tb, dp), lambda b: (0, b, 0))],
        out_shape=[jax.ShapeDtypeStruct((nb, dp), BF16),
                   jax.ShapeDtypeStruct((POOL_STATE, nb, dp), F32)],
        compiler_params=_params(1),
        name="pool_decode",
    )(u, state_t, w_pool, pool_scale)


def _head_out(o, g, gain):
    return _silu(g) * _rms(o, gain)


def _column(row_vec):
    return jnp.broadcast_to(row_vec, (LANES, row_vec.shape[1])).T[:, :1]


def _cumsum_rows(x):
    n = x.shape[0]
    tri = (lax.broadcasted_iota(jnp.int32, (n, n), 0) >= lax.broadcasted_iota(jnp.int32, (n, n), 1))
    tri = jnp.where(tri, 1.0, 0.0).astype(BF16)
    hi = x.astype(BF16)
    r1 = x - hi.astype(F32)
    mid = r1.astype(BF16)
    lo = (r1 - mid.astype(F32)).astype(BF16)
    return _dot(tri, hi) + _dot(tri, mid) + _dot(tri, lo)


def _exact_scores(b_scr, q_scr, k_scr, a_scr):
    rows = b_scr.shape[0]

    def block(i, carry):
        r0 = pl.multiple_of(i * GLA_SUB, GLA_SUB)
        b_i = b_scr[pl.ds(r0, GLA_SUB), :]
        q_i = q_scr[pl.ds(r0, GLA_SUB), :]
        b_first = b_scr[pl.ds(r0, 1), :]
        qt = (q_i * jnp.exp(b_i - b_first)).astype(BF16)
        kt = (k_scr[...] * jnp.exp(jnp.minimum(b_first - b_scr[...], 0.0))).astype(BF16)
        panel = _dot(qt, kt, NT)
        col = lax.broadcasted_iota(jnp.int32, (GLA_SUB, rows), 1)
        row = lax.broadcasted_iota(jnp.int32, (GLA_SUB, rows), 0) + r0
        diag = jnp.zeros((GLA_SUB, rows), F32)
        for j in range(GLA_SUB):
            b_j = b_scr[pl.ds(r0 + j, 1), :]
            k_j = k_scr[pl.ds(r0 + j, 1), :]
            p = q_i * k_j * jnp.exp(jnp.minimum(b_i - b_j, 0.0))
            diag = jnp.where(col == r0 + j, jnp.sum(p, axis=-1, keepdims=True), diag)
        a_scr[pl.ds(r0, GLA_SUB), :] = jnp.where(col < r0, panel, jnp.where(col <= row, diag, 0.0))
        return carry

    lax.fori_loop(0, rows // GLA_SUB, block, 0)


def _gla_prompt_body(q_ref, k_ref, v_ref, g_ref, alr_ref, wa_ref, ba_ref, gn_ref, o_ref, s_ref,
                     b_scr, q_scr, k_scr, a_scr):
    rows, dk = q_ref.shape

    @pl.when(pl.program_id(2) == 0)
    def _():
        s_ref[...] = jnp.zeros(s_ref.shape, F32)

    q = q_ref[...] * (dk ** -0.5)
    k = k_ref[...]
    vb = v_ref[...]
    x = _dot(alr_ref[...].astype(BF16), wa_ref[...].astype(BF16)) + ba_ref[...]
    b = _cumsum_rows(_log_sigmoid(x) / GATE_NORM)
    b_last = b[rows - 1:rows, :]
    qe = (q * jnp.exp(b)).astype(BF16)

    mild = jnp.max(-b_last) < GLA_SAFE_DECAY

    @pl.when(mild)
    def _():
        kinv = (k * jnp.exp(-b)).astype(BF16)
        causal = (lax.broadcasted_iota(jnp.int32, (rows, rows), 1)
                  <= lax.broadcasted_iota(jnp.int32, (rows, rows), 0))
        a_scr[...] = jnp.where(causal, _dot(qe, kinv, NT), 0.0)

    @pl.when(jnp.logical_not(mild))
    def _():
        b_scr[...] = b
        q_scr[...] = q
        k_scr[...] = k
        _exact_scores(b_scr, q_scr, k_scr, a_scr)

    s_old = s_ref[...]
    o = _dot(qe, s_old.astype(BF16)) + _dot(a_scr[...].astype(BF16), vb)
    o_ref[...] = _head_out(o, g_ref[...].astype(F32), gn_ref[...]).astype(o_ref.dtype)
    kd = (k * jnp.exp(b_last - b)).astype(BF16)
    s_ref[...] = _column(jnp.exp(b_last)) * s_old + _dot(kd, vb, TN)


def _gla_prompt(q, k, v, g, alr, wa, ba, gn, nb, t_len):
    dk = q.shape[1] // GLA_HEADS
    dv = v.shape[1] // GLA_HEADS
    c = GLA_CHUNK
    n_chunks = t_len // c
    row = lambda b, h, i: (b * n_chunks + i, h)
    return pl.pallas_call(
        _gla_prompt_body,
        grid=(nb, GLA_HEADS, n_chunks),
        in_specs=[pl.BlockSpec((c, dk), row),
                  pl.BlockSpec((c, dk), row),
                  pl.BlockSpec((c, dv), row),
                  pl.BlockSpec((c, dv), row),
                  pl.BlockSpec((c, LANES), lambda b, h, i: (b * n_chunks + i, 0)),
                  pl.BlockSpec((LANES, dk), lambda b, h, i: (0, h)),
                  pl.BlockSpec((1, dk), lambda b, h, i: (0, h)),
                  pl.BlockSpec((None, 1, dv), lambda b, h, i: (h, 0, 0))],
        out_specs=[pl.BlockSpec((c, dv), row),
                   pl.BlockSpec((None, None, dk, dv), lambda b, h, i: (b, h, 0, 0))],
        out_shape=[jax.ShapeDtypeStruct((nb * t_len, GLA_HEADS * dv), BF16),
                   jax.ShapeDtypeStruct((nb, GLA_HEADS, dk, dv), F32)],
        scratch_shapes=[pltpu.VMEM((c, dk), F32), pltpu.VMEM((c, dk), F32), pltpu.VMEM((c, dk), F32),
                        pltpu.VMEM((c, c), F32)],
        compiler_params=_params(3),
        name="gla_prompt",
    )(q, k, v, g, alr, wa, ba, gn)


def _gla_decode_body(q_ref, k_ref, v_ref, g_ref, alr_ref, wa_ref, ba_ref, gn_ref, s_ref, o_ref, s_out_ref):
    dk = q_ref.shape[1] // GLA_HEADS
    dv = v_ref.shape[1] // GLA_HEADS
    alr = alr_ref[...].astype(BF16)
    for h in range(GLA_HEADS):
        ks, vs = slice(h * dk, (h + 1) * dk), slice(h * dv, (h + 1) * dv)
        q = q_ref[:, ks] * (dk ** -0.5)
        kb = k_ref[:, ks].astype(BF16).astype(F32)
        vb = v_ref[:, vs].astype(F32)
        x = _dot(alr, wa_ref[:, ks].astype(BF16)) + ba_ref[:, ks]
        a = jnp.exp(_log_sigmoid(x) / GATE_NORM)
        s_old = s_ref[h]
        o = _dot((q * a).astype(BF16), s_old.astype(BF16))
        o += jnp.sum(q * kb, axis=-1, keepdims=True) * vb
        o_ref[:, vs] = _head_out(o, g_ref[:, vs].astype(F32), gn_ref[h]).astype(o_ref.dtype)
        s_out_ref[h] = _column(a) * s_old + _column(kb) * vb


def _gla_decode(q, k, v, g, alr, wa, ba, gn, state):
    nb = q.shape[0]
    dkk, dvv = q.shape[2], v.shape[2]
    dk, dv = dkk // GLA_HEADS, dvv // GLA_HEADS
    tok = lambda w: pl.BlockSpec((None, 1, w), lambda b: (b, 0, 0))
    st = pl.BlockSpec((None, GLA_HEADS, dk, dv), lambda b: (b, 0, 0, 0))
    return pl.pallas_call(
        _gla_decode_body,
        grid=(nb,),
        in_specs=[tok(dkk), tok(dkk), tok(dvv), tok(dvv), tok(LANES),
                  pl.BlockSpec((LANES, dkk), lambda b: (0, 0)),
                  pl.BlockSpec((1, dkk), lambda b: (0, 0)),
                  pl.BlockSpec((GLA_HEADS, 1, dv), lambda b: (0, 0, 0)),
                  st],
        out_specs=[tok(dvv), st],
        out_shape=[jax.ShapeDtypeStruct((nb, 1, dvv), BF16),
                   jax.ShapeDtypeStruct(state.shape, F32)],
        compiler_params=_params(1),
        name="gla_decode",
    )(q, k, v, g, alr, wa, ba, gn, state)


def _tiles(t_len):
    tm = min(1024, t_len)
    tms = min(256, t_len)
    return tm, tms


def _layer(xp, xd, mod, lw, dims, nb, t_len, pool_state_t, gla_state):
    (g1, w_in_t, w_alr_t, w_pool, pool_scale, wa, ba, gn, w_out, g2, w_ff1, w_ff3, w_ff2) = lw
    d_pool, d_k, d_v = dims
    n_dec = xd.shape[0]
    tm, tms = _tiles(t_len)
    tpb, tpbs = t_len // tm, t_len // tms
    tn = 512

    h1p = _normmod(xp, g1, mod, n_dec, 1, 0, tms, tpbs)
    h1d = _normmod(xd, g1, mod, n_dec, 1, 0, n_dec, None)
    up, ud = _proj(h1p, h1d, w_in_t, 0, d_pool, tm, tn, F32)
    qp, qd = _proj(h1p, h1d, w_in_t, d_pool, d_k, tm, tn, F32)
    kp, kd = _proj(h1p, h1d, w_in_t, d_pool + d_k, d_k, tm, tn, F32)
    vp, vd = _proj(h1p, h1d, w_in_t, d_pool + 2 * d_k, d_v, tm, tn, BF16)
    gp, gd = _proj(h1p, h1d, w_in_t, d_pool + 2 * d_k + d_v, d_v, tm, tn, BF16)
    ap, ad = _proj(h1p, h1d, w_alr_t, 0, LANES, tm, LANES, F32)

    pyp, pool_p = _pool_prompt(up, w_pool, pool_scale, nb, t_len, tms)
    op, gla_p = _gla_prompt(qp, kp, vp, gp, ap, wa, ba, gn, nb, t_len)
    pyd, pool_d_t = _pool_decode(ud, pool_state_t, w_pool, pool_scale, 32)
    r3 = lambda t: t.reshape(t.shape[0], 1, t.shape[1])
    od, gla_d = _gla_decode(r3(qd), r3(kd), r3(vd), r3(gd), r3(ad), wa, ba, gn, gla_state)
    od = od.reshape(n_dec, d_v)

    x1p, x1d = _outproj(pyp, op, pyd, od, w_out, xp, xd, mod, 2, tm, tn, tpb)
    h2p = _normmod(x1p, g2, mod, n_dec, 4, 3, tms, tpbs)
    h2d = _normmod(x1d, g2, mod, n_dec, 4, 3, n_dec, None)
    actp, actd = _ffn1(h2p, h2d, w_ff1, w_ff3, tm, 256)
    x2p, x2d = _ffn2(actp, actd, w_ff2, x1p, x1d, mod, 5, tms, tn, tpbs)
    return x2p, x2d, pool_p, gla_p, pool_d_t, gla_d


def kernel(x_prompt, x_sample, state_pool, state_gla, c_prompt, c_sample, w_ada, b_ada, g_norm1, w_in,
           w_pool, pool_scale, w_a2, b_a, g_gla_out, w_out, g_norm2, w_ff1, w_ff3, w_ff2, g_final):
    bp, tp, d = x_prompt.shape
    bs, ts, _ = x_sample.shape
    depth = w_ada.shape[0]
    assert ts == 1, "the decode path handles one new token per sequence"
    assert bs % SUBLANES == 0 and bp <= SUBLANES
    d_pool = w_pool.shape[1] * w_pool.shape[2]
    d_k = w_a2.shape[2]
    d_v = g_gla_out.shape[1] * g_gla_out.shape[2]
    dims = (d_pool, d_k, d_v)
    d_main = d_pool + 2 * d_k + 2 * d_v

    c_all = jnp.concatenate([c_sample, c_prompt, jnp.zeros((2 * SUBLANES - bp, d), F32)], axis=0)

    hp = x_prompt.reshape(bp * tp, d)
    hd = x_sample.reshape(bs, d)
    pools_p, glas_p, pools_d, glas_d = [], [], [], []
    for l in range(depth):
        mod = _ada(c_all, w_ada[l], b_ada[l].reshape(1, -1))
        w_in_t = w_in[l].T
        w_alr_t = jnp.pad(w_in_t[d_main:], ((0, LANES - GATE_RANK), (0, 0)))
        wa = jnp.pad(w_a2[l], ((0, LANES - GATE_RANK), (0, 0)))
        lw = (g_norm1[l].reshape(1, d), w_in_t, w_alr_t, w_pool[l], pool_scale[l].reshape(1, -1), wa,
              b_a[l].reshape(1, -1), g_gla_out[l].reshape(GLA_HEADS, 1, -1), w_out[l],
              g_norm2[l].reshape(1, d), w_ff1[l], w_ff3[l], w_ff2[l])
        pool_state_t = jnp.transpose(state_pool[l], (1, 0, 2))
        hp, hd, pool_p, gla_p, pool_d_t, gla_d = _layer(hp, hd, mod, lw, dims, bp, tp, pool_state_t,
                                                        state_gla[l])
        pools_p.append(pool_p)
        glas_p.append(gla_p)
        pools_d.append(jnp.transpose(pool_d_t, (1, 0, 2)))
        glas_d.append(gla_d)
    gf = g_final.reshape(1, d)
    y_p = _norm(hp, gf, _tiles(tp)[1]).reshape(bp, tp, d)
    y_d = _norm(hd, gf, bs).reshape(bs, ts, d)
    return (y_p, y_d, jnp.stack(pools_p), jnp.stack(glas_p), jnp.stack(pools_d), jnp.stack(glas_d))
```

```python
import functools

import jax
import jax.numpy as jnp
from jax import lax
from jax.experimental import pallas as pl
from jax.experimental.pallas import tpu as pltpu

F32 = jnp.float32
BF16 = jnp.bfloat16

POOL_WINDOWS = (2, 4, 8, 16)
POOL_STATE = max(POOL_WINDOWS) - 1
GLA_HEADS = 4
GATE_RANK = 16
GATE_NORM = 16.0
N_MOD = 6
EPS = 1e-6
PAST_LEN = 16384

LANES = 128
SUBLANES = 8
VMEM_LIMIT = 56 * 1024 * 1024
GLA_CHUNK = 256
GLA_SUB = 16
GLA_SAFE_DECAY = 60.0

NT = (((1,), (1,)), ((), ()))
TN = (((0,), (0,)), ((), ()))


def _params(n_axes):
    return pltpu.CompilerParams(dimension_semantics=("arbitrary",) * n_axes, vmem_limit_bytes=VMEM_LIMIT)


def _silu(x):
    return x * jax.nn.sigmoid(x)


def _log_sigmoid(x):
    return jnp.minimum(x, 0.0) - jnp.log(1.0 + jnp.exp(-jnp.abs(x)))


def _dot(a, b, dims=None):
    if dims is None:
        return jnp.dot(a, b, preferred_element_type=F32)
    return lax.dot_general(a, b, dims, preferred_element_type=F32)


def _ada_body(c_ref, w_ref, b_ref, o_ref):
    s = _silu(c_ref[...]).astype(BF16)
    o_ref[...] = _dot(s, w_ref[...].astype(BF16)) + b_ref[...]


def _ada(c, w, b, tn=1024):
    m, d = c.shape
    n = w.shape[1]
    return pl.pallas_call(
        _ada_body,
        grid=(n // tn,),
        in_specs=[pl.BlockSpec((m, d), lambda j: (0, 0)),
                  pl.BlockSpec((d, tn), lambda j: (0, j)),
                  pl.BlockSpec((1, tn), lambda j: (0, j))],
        out_specs=pl.BlockSpec((m, tn), lambda j: (0, j)),
        out_shape=jax.ShapeDtypeStruct((m, n), F32),
        compiler_params=_params(1),
        name="ada",
    )(c, w, b)


def _prompt_mod_spec(n_dec, width, col_of):
    return pl.BlockSpec((SUBLANES, width), lambda *g: (n_dec // SUBLANES, col_of(*g)))


def _decode_mod_spec(n_dec, width, col_of):
    return pl.BlockSpec((n_dec, width), lambda *g: (0, col_of(*g)))


def _rms(x, gain):
    return x * lax.rsqrt(jnp.mean(x * x, axis=-1, keepdims=True) + EPS) * gain


def _normmod_body(x_ref, g_ref, sc_ref, sh_ref, o_ref, *, tpb):
    if tpb is None:
        sc, sh = sc_ref[...], sh_ref[...]
    else:
        b = pl.program_id(0) // tpb
        sc, sh = sc_ref[pl.ds(b, 1), :], sh_ref[pl.ds(b, 1), :]
    o_ref[...] = (_rms(x_ref[...], g_ref[...]) * (1.0 + sc) + sh).astype(o_ref.dtype)


def _normmod(x, gain, mod, n_dec, sc_chunk, sh_chunk, tm, tpb):
    m, d = x.shape
    spec = _decode_mod_spec if tpb is None else _prompt_mod_spec
    return pl.pallas_call(
        functools.partial(_normmod_body, tpb=tpb),
        grid=(m // tm,),
        in_specs=[pl.BlockSpec((tm, d), lambda i: (i, 0)),
                  pl.BlockSpec((1, d), lambda i: (0, 0)),
                  spec(n_dec, d, lambda i: sc_chunk),
                  spec(n_dec, d, lambda i: sh_chunk)],
        out_specs=pl.BlockSpec((tm, d), lambda i: (i, 0)),
        out_shape=jax.ShapeDtypeStruct((m, d), BF16),
        compiler_params=_params(1),
        name="normmod",
    )(x, gain, mod, mod)


def _norm_body(x_ref, g_ref, o_ref):
    o_ref[...] = _rms(x_ref[...], g_ref[...])


def _norm(x, gain, tm):
    m, d = x.shape
    return pl.pallas_call(
        _norm_body,
        grid=(m // tm,),
        in_specs=[pl.BlockSpec((tm, d), lambda i: (i, 0)),
                  pl.BlockSpec((1, d), lambda i: (0, 0))],
        out_specs=pl.BlockSpec((tm, d), lambda i: (i, 0)),
        out_shape=jax.ShapeDtypeStruct((m, d), F32),
        compiler_params=_params(1),
        name="finalnorm",
    )(x, gain)


def _cast_once(w_ref, wbf_ref):
    @pl.when(pl.program_id(1) == 0)
    def _():
        wbf_ref[...] = w_ref[...].astype(BF16)


def _row_steps(prompt_fn, decode_fn):
    i = pl.program_id(1)
    pl.when(i == 0)(decode_fn)
    pl.when(i > 0)(prompt_fn)


def _prow(mt):
    def tile(j, i):
        t = jnp.maximum(i - 1, 0)
        return jnp.where(j % 2 == 0, t, mt - 1 - t)
    return tile


def _prompt_seq(mt, tpb):
    return _prow(mt)(pl.program_id(0), pl.program_id(1)) // tpb


def _proj_body(ap_ref, ad_ref, wt_ref, op_ref, od_ref, wbf_ref):
    _cast_once(wt_ref, wbf_ref)

    def prompt():
        op_ref[...] = _dot(ap_ref[...], wbf_ref[...], NT).astype(op_ref.dtype)

    def decode():
        od_ref[...] = _dot(ad_ref[...], wbf_ref[...], NT).astype(od_ref.dtype)

    _row_steps(prompt, decode)


def _proj(ap, ad, wt, n_tiles, wblock, tm, tn, out_dtype):
    mp, kdim = ap.shape
    md = ad.shape[0]
    mt = mp // tm
    pr = _prow(mt)
    return pl.pallas_call(
        _proj_body,
        grid=(n_tiles, mt + 1),
        in_specs=[pl.BlockSpec((tm, kdim), lambda j, i: (pr(j, i), 0)),
                  pl.BlockSpec((md, kdim), lambda j, i: (0, 0)),
                  pl.BlockSpec((tn, kdim), lambda j, i: (wblock(j), 0))],
        out_specs=[pl.BlockSpec((tm, tn), lambda j, i: (pr(j, i), j)),
                   pl.BlockSpec((md, tn), lambda j, i: (0, j))],
        out_shape=[jax.ShapeDtypeStruct((mp, n_tiles * tn), out_dtype),
                   jax.ShapeDtypeStruct((md, n_tiles * tn), out_dtype)],
        scratch_shapes=[pltpu.VMEM((tn, kdim), BF16)],
        compiler_params=_params(2),
        name="proj",
    )(ap, ad, wt)


def _outproj_body(pyp_ref, op_ref, pyd_ref, od_ref, w_ref, xp_ref, xd_ref, gp_ref, gd_ref,
                  outp_ref, outd_ref, wbf_ref, *, mt, tpb, d_pool):
    _cast_once(w_ref, wbf_ref)
    seq = _prompt_seq(mt, tpb)

    def mix(py_ref, o_ref):
        return _dot(py_ref[...], wbf_ref[:d_pool, :]) + _dot(o_ref[...], wbf_ref[d_pool:, :])

    def prompt():
        outp_ref[...] = xp_ref[...] + gp_ref[pl.ds(seq, 1), :] * mix(pyp_ref, op_ref)

    def decode():
        outd_ref[...] = xd_ref[...] + gd_ref[...] * mix(pyd_ref, od_ref)

    _row_steps(prompt, decode)


def _outproj(pyp, op, pyd, od, w, xp, xd, mod, gt_chunk, tm, tn, tpb):
    mp, d_pool = pyp.shape
    md = pyd.shape[0]
    d_v = op.shape[1]
    kdim, n = w.shape
    mt = mp // tm
    per = n // tn
    pr = _prow(mt)
    gcol = lambda j, i: gt_chunk * per + j
    return pl.pallas_call(
        functools.partial(_outproj_body, mt=mt, tpb=tpb, d_pool=d_pool),
        grid=(per, mt + 1),
        in_specs=[pl.BlockSpec((tm, d_pool), lambda j, i: (pr(j, i), 0)),
                  pl.BlockSpec((tm, d_v), lambda j, i: (pr(j, i), 0)),
                  pl.BlockSpec((md, d_pool), lambda j, i: (0, 0)),
                  pl.BlockSpec((md, d_v), lambda j, i: (0, 0)),
                  pl.BlockSpec((kdim, tn), lambda j, i: (0, j)),
                  pl.BlockSpec((tm, tn), lambda j, i: (pr(j, i), j)),
                  pl.BlockSpec((md, tn), lambda j, i: (0, j)),
                  _prompt_mod_spec(md, tn, gcol),
                  _decode_mod_spec(md, tn, gcol)],
        out_specs=[pl.BlockSpec((tm, tn), lambda j, i: (pr(j, i), j)),
                   pl.BlockSpec((md, tn), lambda j, i: (0, j))],
        out_shape=[jax.ShapeDtypeStruct((mp, n), F32), jax.ShapeDtypeStruct((md, n), F32)],
        scratch_shapes=[pltpu.VMEM((kdim, tn), BF16)],
        compiler_params=_params(2),
        name="outproj",
    )(pyp, op, pyd, od, w, xp, xd, mod, mod)


def _ffn1_body(hp_ref, hd_ref, w1_ref, w3_ref, op_ref, od_ref, w1bf_ref, w3bf_ref):
    _cast_once(w1_ref, w1bf_ref)
    _cast_once(w3_ref, w3bf_ref)

    def swiglu(h_ref, o_ref):
        h = h_ref[...]
        o_ref[...] = (_silu(_dot(h, w1bf_ref[...])) * _dot(h, w3bf_ref[...])).astype(o_ref.dtype)

    _row_steps(functools.partial(swiglu, hp_ref, op_ref), functools.partial(swiglu, hd_ref, od_ref))


def _ffn1(hp, hd, w1, w3, tm, tn):
    mp, kdim = hp.shape
    md = hd.shape[0]
    n = w1.shape[1]
    mt = mp // tm
    pr = _prow(mt)
    wspec = pl.BlockSpec((kdim, tn), lambda j, i: (0, j))
    return pl.pallas_call(
        _ffn1_body,
        grid=(n // tn, mt + 1),
        in_specs=[pl.BlockSpec((tm, kdim), lambda j, i: (pr(j, i), 0)),
                  pl.BlockSpec((md, kdim), lambda j, i: (0, 0)),
                  wspec, wspec],
        out_specs=[pl.BlockSpec((tm, tn), lambda j, i: (pr(j, i), j)),
                   pl.BlockSpec((md, tn), lambda j, i: (0, j))],
        out_shape=[jax.ShapeDtypeStruct((mp, n), BF16), jax.ShapeDtypeStruct((md, n), BF16)],
        scratch_shapes=[pltpu.VMEM((kdim, tn), BF16), pltpu.VMEM((kdim, tn), BF16)],
        compiler_params=_params(2),
        name="ffn1",
    )(hp, hd, w1, w3)


def _ffn2_body(ap_ref, ad_ref, w_ref, xp_ref, xd_ref, gp_ref, gd_ref, outp_ref, outd_ref, wbf_ref, *, mt, tpb):
    _cast_once(w_ref, wbf_ref)
    seq = _prompt_seq(mt, tpb)

    def prompt():
        outp_ref[...] = xp_ref[...] + gp_ref[pl.ds(seq, 1), :] * _dot(ap_ref[...], wbf_ref[...])

    def decode():
        outd_ref[...] = xd_ref[...] + gd_ref[...] * _dot(ad_ref[...], wbf_ref[...])

    _row_steps(prompt, decode)


def _ffn2(ap, ad, w, xp, xd, mod, gt_chunk, tm, tn, tpb):
    mp, kdim = ap.shape
    md = ad.shape[0]
    n = w.shape[1]
    mt = mp // tm
    per = n // tn
    pr = _prow(mt)
    gcol = lambda j, i: gt_chunk * per + j
    return pl.pallas_call(
        functools.partial(_ffn2_body, mt=mt, tpb=tpb),
        grid=(per, mt + 1),
        in_specs=[pl.BlockSpec((tm, kdim), lambda j, i: (pr(j, i), 0)),
                  pl.BlockSpec((md, kdim), lambda j, i: (0, 0)),
                  pl.BlockSpec((kdim, tn), lambda j, i: (0, j), pipeline_mode=pl.Buffered(1)),
                  pl.BlockSpec((tm, tn), lambda j, i: (pr(j, i), j)),
                  pl.BlockSpec((md, tn), lambda j, i: (0, j)),
                  _prompt_mod_spec(md, tn, gcol),
                  _decode_mod_spec(md, tn, gcol)],
        out_specs=[pl.BlockSpec((tm, tn), lambda j, i: (pr(j, i), j)),
                   pl.BlockSpec((md, tn), lambda j, i: (0, j))],
        out_shape=[jax.ShapeDtypeStruct((mp, n), F32), jax.ShapeDtypeStruct((md, n), F32)],
        scratch_shapes=[pltpu.VMEM((kdim, tn), BF16)],
        compiler_params=_params(2),
        name="ffn2",
    )(ap, ad, w, xp, xd, mod, mod)


def _pool_group_matmul(d, wp_ref, ps_ref, n_groups, gw):
    outs = []
    for gi in range(n_groups):
        dg = d[:, gi * gw:(gi + 1) * gw].astype(BF16)
        outs.append(_dot(dg, wp_ref[gi].astype(BF16)))
    return jnp.concatenate(outs, axis=-1) * ps_ref[...]


def _pool_prompt_body(u_ref, wp_ref, ps_ref, y_ref, st_ref, buf_ref, *, tm, n_tiles):
    t = pl.program_id(1)
    halo = POOL_STATE + 1
    n_groups = len(POOL_WINDOWS)
    gw = u_ref.shape[1] // n_groups

    @pl.when(t == 0)
    def _():
        buf_ref[:halo, :] = jnp.zeros((halo, buf_ref.shape[1]), F32)

    u = u_ref[...]
    buf_ref[halo:, :] = u
    pos = t * tm + lax.broadcasted_iota(jnp.int32, (tm, 1), 0)
    parts = []
    for gi, w in enumerate(POOL_WINDOWS):
        lo, hi = gi * gw, (gi + 1) * gw
        win = u[:, lo:hi]
        for s in range(1, w):
            win = win + buf_ref[halo - s:halo - s + tm, lo:hi]
        cnt = jnp.minimum(w, pos + 1).astype(F32)
        parts.append(win / cnt - u[:, lo:hi])
    d = jnp.concatenate(parts, axis=-1)
    y_ref[...] = _pool_group_matmul(d, wp_ref, ps_ref, n_groups, gw).astype(y_ref.dtype)

    @pl.when(t == n_tiles - 1)
    def _():
        st_ref[...] = buf_ref[halo + tm - POOL_STATE:, :]

    buf_ref[:halo, :] = buf_ref[tm:, :]


def _pool_prompt(u, ucol, w_pool, pool_scale, nb, t_len, tm):
    m = u.shape[0]
    n_tiles = t_len // tm
    g, gw, _ = w_pool.shape
    dp = g * gw
    return pl.pallas_call(
        functools.partial(_pool_prompt_body, tm=tm, n_tiles=n_tiles),
        grid=(nb, n_tiles),
        in_specs=[pl.BlockSpec((tm, dp), lambda b, t: (b * n_tiles + t, ucol)),
                  pl.BlockSpec((g, gw, gw), lambda b, t: (0, 0, 0)),
                  pl.BlockSpec((1, dp), lambda b, t: (0, 0))],
        out_specs=[pl.BlockSpec((tm, dp), lambda b, t: (b * n_tiles + t, 0)),
                   pl.BlockSpec((None, POOL_STATE, dp), lambda b, t: (b, 0, 0))],
        out_shape=[jax.ShapeDtypeStruct((m, dp), BF16),
                   jax.ShapeDtypeStruct((nb, POOL_STATE, dp), F32)],
        scratch_shapes=[pltpu.VMEM((POOL_STATE + 1 + tm, dp), F32)],
        compiler_params=_params(2),
        name="pool_prompt",
    )(u, w_pool, pool_scale)


def _pool_decode_body(u_ref, sp_ref, wp_ref, ps_ref, y_ref, st_ref):
    n_groups = len(POOL_WINDOWS)
    gw = u_ref.shape[1] // n_groups
    u = u_ref[...]
    parts = []
    for gi, w in enumerate(POOL_WINDOWS):
        lo, hi = gi * gw, (gi + 1) * gw
        win = u[:, lo:hi]
        for s in range(1, w):
            win = win + sp_ref[POOL_STATE - s, :, lo:hi]
        cnt = float(min(w, PAST_LEN + 1))
        parts.append(win / cnt - u[:, lo:hi])
    d = jnp.concatenate(parts, axis=-1)
    y_ref[...] = _pool_group_matmul(d, wp_ref, ps_ref, n_groups, gw).astype(y_ref.dtype)
    for r in range(POOL_STATE - 1):
        st_ref[r] = sp_ref[r + 1]
    st_ref[POOL_STATE - 1] = u


def _pool_decode(u, ucol, state_t, w_pool, pool_scale, tb):
    nb = u.shape[0]
    g, gw, _ = w_pool.shape
    dp = g * gw
    return pl.pallas_call(
        _pool_decode_body,
        grid=(nb // tb,),
        in_specs=[pl.BlockSpec((tb, dp), lambda b: (b, ucol)),
                  pl.BlockSpec((POOL_STATE, tb, dp), lambda b: (0, b, 0)),
                  pl.BlockSpec((g, gw, gw), lambda b: (0, 0, 0)),
                  pl.BlockSpec((1, dp), lambda b: (0, 0))],
        out_specs=[pl.BlockSpec((tb, dp), lambda b: (b, 0)),
                   pl.BlockSpec((POOL_STATE, tb, dp), lambda b: (0, b, 0))],
        out_shape=[jax.ShapeDtypeStruct((nb, dp), BF16),
                   jax.ShapeDtypeStruct((POOL_STATE, nb, dp), F32)],
        compiler_params=_params(1),
        name="pool_decode",
    )(u, state_t, w_pool, pool_scale)


def _head_out(o, g, gain):
    return _silu(g) * _rms(o, gain)


def _column(row_vec):
    return jnp.broadcast_to(row_vec, (LANES, row_vec.shape[1])).T[:, :1]


def _cumsum_rows(x):
    n = x.shape[0]
    tri = (lax.broadcasted_iota(jnp.int32, (n, n), 0) >= lax.broadcasted_iota(jnp.int32, (n, n), 1))
    tri = jnp.where(tri, 1.0, 0.0).astype(BF16)
    hi = x.astype(BF16)
    r1 = x - hi.astype(F32)
    mid = r1.astype(BF16)
    lo = (r1 - mid.astype(F32)).astype(BF16)
    return _dot(tri, hi) + _dot(tri, mid) + _dot(tri, lo)


def _exact_scores(b_scr, q_scr, k_scr, a_scr):
    rows = b_scr.shape[0]

    def block(i, carry):
        r0 = pl.multiple_of(i * GLA_SUB, GLA_SUB)
        b_i = b_scr[pl.ds(r0, GLA_SUB), :]
        q_i = q_scr[pl.ds(r0, GLA_SUB), :]
        b_first = b_scr[pl.ds(r0, 1), :]
        qt = (q_i * jnp.exp(b_i - b_first)).astype(BF16)
        kt = (k_scr[...] * jnp.exp(jnp.minimum(b_first - b_scr[...], 0.0))).astype(BF16)
        panel = _dot(qt, kt, NT)
        col = lax.broadcasted_iota(jnp.int32, (GLA_SUB, rows), 1)
        row = lax.broadcasted_iota(jnp.int32, (GLA_SUB, rows), 0) + r0
        diag = jnp.zeros((GLA_SUB, rows), F32)
        for j in range(GLA_SUB):
            b_j = b_scr[pl.ds(r0 + j, 1), :]
            k_j = k_scr[pl.ds(r0 + j, 1), :]
            p = q_i * k_j * jnp.exp(jnp.minimum(b_i - b_j, 0.0))
            diag = jnp.where(col == r0 + j, jnp.sum(p, axis=-1, keepdims=True), diag)
        a_scr[pl.ds(r0, GLA_SUB), :] = jnp.where(col < r0, panel, jnp.where(col <= row, diag, 0.0))
        return carry

    lax.fori_loop(0, rows // GLA_SUB, block, 0)


def _gla_prompt_body(q_ref, k_ref, v_ref, g_ref, alr_ref, wa_ref, ba_ref, gn_ref, o_ref, s_ref,
                     b_scr, q_scr, k_scr, a_scr):
    rows, dk = q_ref.shape

    @pl.when(pl.program_id(2) == 0)
    def _():
        s_ref[...] = jnp.zeros(s_ref.shape, F32)

    q = q_ref[...] * (dk ** -0.5)
    k = k_ref[...]
    vb = v_ref[...]
    x = _dot(alr_ref[...].astype(BF16), wa_ref[...].astype(BF16)) + ba_ref[...]
    b = _cumsum_rows(_log_sigmoid(x) / GATE_NORM)
    b_last = b[rows - 1:rows, :]
    qe = (q * jnp.exp(b)).astype(BF16)

    mild = jnp.max(-b_last) < GLA_SAFE_DECAY

    @pl.when(mild)
    def _():
        kinv = (k * jnp.exp(-b)).astype(BF16)
        causal = (lax.broadcasted_iota(jnp.int32, (rows, rows), 1)
                  <= lax.broadcasted_iota(jnp.int32, (rows, rows), 0))
        a_scr[...] = jnp.where(causal, _dot(qe, kinv, NT), 0.0)

    @pl.when(jnp.logical_not(mild))
    def _():
        b_scr[...] = b
        q_scr[...] = q
        k_scr[...] = k
        _exact_scores(b_scr, q_scr, k_scr, a_scr)

    s_old = s_ref[...]
    o = _dot(qe, s_old.astype(BF16)) + _dot(a_scr[...].astype(BF16), vb)
    o_ref[...] = _head_out(o, g_ref[...].astype(F32), gn_ref[...]).astype(o_ref.dtype)
    kd = (k * jnp.exp(b_last - b)).astype(BF16)
    s_ref[...] = _column(jnp.exp(b_last)) * s_old + _dot(kd, vb, TN)


def _gla_prompt(qku, vg, alr, wa, ba, gn, nb, t_len):
    dk = wa.shape[1] // GLA_HEADS
    dv = gn.shape[2]
    c = GLA_CHUNK
    n_chunks = t_len // c
    row = lambda b, h, i: (b * n_chunks + i, h)
    row2 = lambda b, h, i: (b * n_chunks + i, GLA_HEADS + h)
    return pl.pallas_call(
        _gla_prompt_body,
        grid=(nb, GLA_HEADS, n_chunks),
        in_specs=[pl.BlockSpec((c, dk), row),
                  pl.BlockSpec((c, dk), row2),
                  pl.BlockSpec((c, dv), row),
                  pl.BlockSpec((c, dv), row2),
                  pl.BlockSpec((c, LANES), lambda b, h, i: (b * n_chunks + i, 0)),
                  pl.BlockSpec((LANES, dk), lambda b, h, i: (0, h)),
                  pl.BlockSpec((1, dk), lambda b, h, i: (0, h)),
                  pl.BlockSpec((None, 1, dv), lambda b, h, i: (h, 0, 0))],
        out_specs=[pl.BlockSpec((c, dv), row),
                   pl.BlockSpec((None, None, dk, dv), lambda b, h, i: (b, h, 0, 0))],
        out_shape=[jax.ShapeDtypeStruct((nb * t_len, GLA_HEADS * dv), BF16),
                   jax.ShapeDtypeStruct((nb, GLA_HEADS, dk, dv), F32)],
        scratch_shapes=[pltpu.VMEM((c, dk), F32), pltpu.VMEM((c, dk), F32), pltpu.VMEM((c, dk), F32),
                        pltpu.VMEM((c, c), F32)],
        compiler_params=_params(3),
        name="gla_prompt",
    )(qku, qku, vg, vg, alr, wa, ba, gn)


def _gla_decode_body(q_ref, k_ref, v_ref, g_ref, alr_ref, wa_ref, ba_ref, gn_ref, s_ref, o_ref, s_out_ref):
    dk = q_ref.shape[1] // GLA_HEADS
    dv = v_ref.shape[1] // GLA_HEADS
    alr = alr_ref[...].astype(BF16)
    for h in range(GLA_HEADS):
        ks, vs = slice(h * dk, (h + 1) * dk), slice(h * dv, (h + 1) * dv)
        q = q_ref[:, ks] * (dk ** -0.5)
        kb = k_ref[:, ks].astype(BF16).astype(F32)
        vb = v_ref[:, vs].astype(F32)
        x = _dot(alr, wa_ref[:, ks].astype(BF16)) + ba_ref[:, ks]
        a = jnp.exp(_log_sigmoid(x) / GATE_NORM)
        s_old = s_ref[h]
        o = _dot((q * a).astype(BF16), s_old.astype(BF16))
        o += jnp.sum(q * kb, axis=-1, keepdims=True) * vb
        o_ref[:, vs] = _head_out(o, g_ref[:, vs].astype(F32), gn_ref[h]).astype(o_ref.dtype)
        s_out_ref[h] = _column(a) * s_old + _column(kb) * vb


def _gla_decode(qku, vg, alr, wa, ba, gn, state):
    nb = qku.shape[0]
    dkk = wa.shape[1]
    dk, dv = dkk // GLA_HEADS, gn.shape[2]
    dvv = GLA_HEADS * dv
    tok = lambda w, blk=0: pl.BlockSpec((None, 1, w), lambda b: (b, 0, blk))
    st = pl.BlockSpec((None, GLA_HEADS, dk, dv), lambda b: (b, 0, 0, 0))
    return pl.pallas_call(
        _gla_decode_body,
        grid=(nb,),
        in_specs=[tok(dkk), tok(dkk, 1), tok(dvv), tok(dvv, 1), tok(LANES),
                  pl.BlockSpec((LANES, dkk), lambda b: (0, 0)),
                  pl.BlockSpec((1, dkk), lambda b: (0, 0)),
                  pl.BlockSpec((GLA_HEADS, 1, dv), lambda b: (0, 0, 0)),
                  st],
        out_specs=[tok(dvv), st],
        out_shape=[jax.ShapeDtypeStruct((nb, 1, dvv), BF16),
                   jax.ShapeDtypeStruct(state.shape, F32)],
        compiler_params=_params(1),
        name="gla_decode",
    )(qku, qku, vg, vg, alr, wa, ba, gn, state)


def _tiles(t_len):
    tm = min(1024, t_len)
    tms = min(256, t_len)
    return tm, tms


def _layer(xp, xd, mod, lw, dims, nb, t_len, pool_state_t, gla_state):
    (g1, w_in_t, w_alr_t, w_pool, pool_scale, wa, ba, gn, w_out, g2, w_ff1, w_ff3, w_ff2) = lw
    d_pool, d_k, d_v = dims
    n_dec = xd.shape[0]
    tm, tms = _tiles(t_len)
    tpb, tpbs = t_len // tm, t_len // tms
    tn = 512

    h1p = _normmod(xp, g1, mod, n_dec, 1, 0, tms, tpbs)
    h1d = _normmod(xd, g1, mod, n_dec, 1, 0, n_dec, None)
    u_tiles, qk_tiles, vg_tiles = d_pool // tn, 2 * d_k // tn, 2 * d_v // tn
    qku_block = lambda j: jnp.where(j < qk_tiles, j + u_tiles, j - qk_tiles)
    qkup, qkud = _proj(h1p, h1d, w_in_t, qk_tiles + u_tiles, qku_block, tm, tn, F32)
    vgp, vgd = _proj(h1p, h1d, w_in_t, vg_tiles, lambda j: j + u_tiles + qk_tiles, tm, tn, BF16)
    ap, ad = _proj(h1p, h1d, w_alr_t, 1, lambda j: 0, tm, LANES, F32)
    ucol = 2 * d_k // d_pool

    pyp, pool_p = _pool_prompt(qkup, ucol, w_pool, pool_scale, nb, t_len, tms)
    op, gla_p = _gla_prompt(qkup, vgp, ap, wa, ba, gn, nb, t_len)
    pyd, pool_d_t = _pool_decode(qkud, ucol, pool_state_t, w_pool, pool_scale, 32)
    r3 = lambda t: t.reshape(t.shape[0], 1, t.shape[1])
    od, gla_d = _gla_decode(r3(qkud), r3(vgd), r3(ad), wa, ba, gn, gla_state)
    od = od.reshape(n_dec, d_v)

    x1p, x1d = _outproj(pyp, op, pyd, od, w_out, xp, xd, mod, 2, tm, tn, tpb)
    h2p = _normmod(x1p, g2, mod, n_dec, 4, 3, tms, tpbs)
    h2d = _normmod(x1d, g2, mod, n_dec, 4, 3, n_dec, None)
    actp, actd = _ffn1(h2p, h2d, w_ff1, w_ff3, tm, 256)
    x2p, x2d = _ffn2(actp, actd, w_ff2, x1p, x1d, mod, 5, tms, tn, tpbs)
    return x2p, x2d, pool_p, gla_p, pool_d_t, gla_d


def kernel(x_prompt, x_sample, state_pool, state_gla, c_prompt, c_sample, w_ada, b_ada, g_norm1, w_in,
           w_pool, pool_scale, w_a2, b_a, g_gla_out, w_out, g_norm2, w_ff1, w_ff3, w_ff2, g_final):
    bp, tp, d = x_prompt.shape
    bs, ts, _ = x_sample.shape
    depth = w_ada.shape[0]
    assert ts == 1, "the decode path handles one new token per sequence"
    assert bs % SUBLANES == 0 and bp <= SUBLANES
    d_pool = w_pool.shape[1] * w_pool.shape[2]
    d_k = w_a2.shape[2]
    d_v = g_gla_out.shape[1] * g_gla_out.shape[2]
    dims = (d_pool, d_k, d_v)
    d_main = d_pool + 2 * d_k + 2 * d_v

    c_all = jnp.concatenate([c_sample, c_prompt, jnp.zeros((2 * SUBLANES - bp, d), F32)], axis=0)

    hp = x_prompt.reshape(bp * tp, d)
    hd = x_sample.reshape(bs, d)
    pools_p, glas_p, pools_d, glas_d = [], [], [], []
    for l in range(depth):
        mod = _ada(c_all, w_ada[l], b_ada[l].reshape(1, -1))
        w_in_t = w_in[l].T
        w_alr_t = jnp.pad(w_in_t[d_main:], ((0, LANES - GATE_RANK), (0, 0)))
        wa = jnp.pad(w_a2[l], ((0, LANES - GATE_RANK), (0, 0)))
        lw = (g_norm1[l].reshape(1, d), w_in_t, w_alr_t, w_pool[l], pool_scale[l].reshape(1, -1), wa,
              b_a[l].reshape(1, -1), g_gla_out[l].reshape(GLA_HEADS, 1, -1), w_out[l],
              g_norm2[l].reshape(1, d), w_ff1[l], w_ff3[l], w_ff2[l])
        pool_state_t = jnp.transpose(state_pool[l], (1, 0, 2))
        hp, hd, pool_p, gla_p, pool_d_t, gla_d = _layer(hp, hd, mod, lw, dims, bp, tp, pool_state_t,
                                                        state_gla[l])
        pools_p.append(pool_p)
        glas_p.append(gla_p)
        pools_d.append(jnp.transpose(pool_d_t, (1, 0, 2)))
        glas_d.append(gla_d)
    gf = g_final.reshape(1, d)
    y_p = _norm(hp, gf, _tiles(tp)[1]).reshape(bp, tp, d)
    y_d = _norm(hd, gf, bs).reshape(bs, ts, d)
    return (y_p, y_d, jnp.stack(pools_p), jnp.stack(glas_p), jnp.stack(pools_d), jnp.stack(glas_d))
```

```python
import functools

import jax
import jax.numpy as jnp
from jax import lax
from jax.experimental import pallas as pl
from jax.experimental.pallas import tpu as pltpu

F32 = jnp.float32
BF16 = jnp.bfloat16

POOL_WINDOWS = (2, 4, 8, 16)
POOL_STATE = max(POOL_WINDOWS) - 1
GLA_HEADS = 4
GATE_RANK = 16
GATE_NORM = 16.0
N_MOD = 6
EPS = 1e-6
PAST_LEN = 16384

LANES = 128
SUBLANES = 8
VMEM_LIMIT = 56 * 1024 * 1024
GLA_CHUNK = 256
GLA_SUB = 16
GLA_SAFE_DECAY = 60.0

NT = (((1,), (1,)), ((), ()))
TN = (((0,), (0,)), ((), ()))


def _params(n_axes):
    return pltpu.CompilerParams(dimension_semantics=("arbitrary",) * n_axes, vmem_limit_bytes=VMEM_LIMIT)


def _silu(x):
    return x * jax.nn.sigmoid(x)


def _log_sigmoid(x):
    return jnp.minimum(x, 0.0) - jnp.log(1.0 + jnp.exp(-jnp.abs(x)))


def _dot(a, b, dims=None):
    if dims is None:
        return jnp.dot(a, b, preferred_element_type=F32)
    return lax.dot_general(a, b, dims, preferred_element_type=F32)


def _ada_body(c_ref, w_ref, b_ref, o_ref):
    s = _silu(c_ref[...]).astype(BF16)
    o_ref[...] = _dot(s, w_ref[...].astype(BF16)) + b_ref[...]


def _ada(c, w, b, tn=1024):
    m, d = c.shape
    n = w.shape[1]
    return pl.pallas_call(
        _ada_body,
        grid=(n // tn,),
        in_specs=[pl.BlockSpec((m, d), lambda j: (0, 0)),
                  pl.BlockSpec((d, tn), lambda j: (0, j)),
                  pl.BlockSpec((1, tn), lambda j: (0, j))],
        out_specs=pl.BlockSpec((m, tn), lambda j: (0, j)),
        out_shape=jax.ShapeDtypeStruct((m, n), F32),
        compiler_params=_params(1),
        name="ada",
    )(c, w, b)


def _prompt_mod_spec(n_dec, width, col_of):
    return pl.BlockSpec((SUBLANES, width), lambda *g: (n_dec // SUBLANES, col_of(*g)))


def _decode_mod_spec(n_dec, width, col_of):
    return pl.BlockSpec((n_dec, width), lambda *g: (0, col_of(*g)))


def _rms(x, gain):
    return x * lax.rsqrt(jnp.mean(x * x, axis=-1, keepdims=True) + EPS) * gain


def _normmod_body(x_ref, g_ref, sc_ref, sh_ref, o_ref, *, tpb):
    if tpb is None:
        sc, sh = sc_ref[...], sh_ref[...]
    else:
        b = pl.program_id(0) // tpb
        sc, sh = sc_ref[pl.ds(b, 1), :], sh_ref[pl.ds(b, 1), :]
    o_ref[...] = (_rms(x_ref[...], g_ref[...]) * (1.0 + sc) + sh).astype(o_ref.dtype)


def _normmod(x, gain, mod, n_dec, sc_chunk, sh_chunk, tm, tpb):
    m, d = x.shape
    spec = _decode_mod_spec if tpb is None else _prompt_mod_spec
    return pl.pallas_call(
        functools.partial(_normmod_body, tpb=tpb),
        grid=(m // tm,),
        in_specs=[pl.BlockSpec((tm, d), lambda i: (i, 0)),
                  pl.BlockSpec((1, d), lambda i: (0, 0)),
                  spec(n_dec, d, lambda i: sc_chunk),
                  spec(n_dec, d, lambda i: sh_chunk)],
        out_specs=pl.BlockSpec((tm, d), lambda i: (i, 0)),
        out_shape=jax.ShapeDtypeStruct((m, d), BF16),
        compiler_params=_params(1),
        name="normmod",
    )(x, gain, mod, mod)


def _norm_body(x_ref, g_ref, o_ref):
    o_ref[...] = _rms(x_ref[...], g_ref[...])


def _norm(x, gain, tm):
    m, d = x.shape
    return pl.pallas_call(
        _norm_body,
        grid=(m // tm,),
        in_specs=[pl.BlockSpec((tm, d), lambda i: (i, 0)),
                  pl.BlockSpec((1, d), lambda i: (0, 0))],
        out_specs=pl.BlockSpec((tm, d), lambda i: (i, 0)),
        out_shape=jax.ShapeDtypeStruct((m, d), F32),
        compiler_params=_params(1),
        name="finalnorm",
    )(x, gain)


def _cast_once(w_ref, wbf_ref):
    @pl.when(pl.program_id(1) == 0)
    def _():
        wbf_ref[...] = w_ref[...].astype(BF16)


def _row_steps(prompt_fn, decode_fn):
    i = pl.program_id(1)
    pl.when(i == 0)(decode_fn)
    pl.when(i > 0)(prompt_fn)


def _prow(mt):
    def tile(j, i):
        t = jnp.maximum(i - 1, 0)
        return jnp.where(j % 2 == 0, t, mt - 1 - t)
    return tile


def _prompt_seq(mt, tpb):
    return _prow(mt)(pl.program_id(0), pl.program_id(1)) // tpb


def _proj_body(ap_ref, ad_ref, wt_ref, op_ref, od_ref, wbf_ref):
    _cast_once(wt_ref, wbf_ref)

    def prompt():
        op_ref[...] = _dot(ap_ref[...], wbf_ref[...], NT).astype(op_ref.dtype)

    def decode():
        od_ref[...] = _dot(ad_ref[...], wbf_ref[...], NT).astype(od_ref.dtype)

    _row_steps(prompt, decode)


def _proj(ap, ad, wt, n_tiles, wblock, tm, tn, out_dtype):
    mp, kdim = ap.shape
    md = ad.shape[0]
    mt = mp // tm
    pr = _prow(mt)
    return pl.pallas_call(
        _proj_body,
        grid=(n_tiles, mt + 1),
        in_specs=[pl.BlockSpec((tm, kdim), lambda j, i: (pr(j, i), 0)),
                  pl.BlockSpec((md, kdim), lambda j, i: (0, 0)),
                  pl.BlockSpec((tn, kdim), lambda j, i: (wblock(j), 0))],
        out_specs=[pl.BlockSpec((tm, tn), lambda j, i: (pr(j, i), j)),
                   pl.BlockSpec((md, tn), lambda j, i: (0, j))],
        out_shape=[jax.ShapeDtypeStruct((mp, n_tiles * tn), out_dtype),
                   jax.ShapeDtypeStruct((md, n_tiles * tn), out_dtype)],
        scratch_shapes=[pltpu.VMEM((tn, kdim), BF16)],
        compiler_params=_params(2),
        name="proj",
    )(ap, ad, wt)


def _outproj_body(pyp_ref, op_ref, pyd_ref, od_ref, w_ref, xp_ref, xd_ref, gp_ref, gd_ref,
                  outp_ref, outd_ref, wbf_ref, *, mt, tpb, d_pool):
    _cast_once(w_ref, wbf_ref)
    seq = _prompt_seq(mt, tpb)

    def mix(py_ref, o_ref):
        return _dot(py_ref[...], wbf_ref[:d_pool, :]) + _dot(o_ref[...], wbf_ref[d_pool:, :])

    def prompt():
        outp_ref[...] = xp_ref[...] + gp_ref[pl.ds(seq, 1), :] * mix(pyp_ref, op_ref)

    def decode():
        outd_ref[...] = xd_ref[...] + gd_ref[...] * mix(pyd_ref, od_ref)

    _row_steps(prompt, decode)


def _outproj(pyp, op, pyd, od, w, xp, xd, mod, gt_chunk, tm, tn, tpb):
    mp, d_pool = pyp.shape
    md = pyd.shape[0]
    d_v = op.shape[1]
    kdim, n = w.shape
    mt = mp // tm
    per = n // tn
    pr = _prow(mt)
    gcol = lambda j, i: gt_chunk * per + j
    return pl.pallas_call(
        functools.partial(_outproj_body, mt=mt, tpb=tpb, d_pool=d_pool),
        grid=(per, mt + 1),
        in_specs=[pl.BlockSpec((tm, d_pool), lambda j, i: (pr(j, i), 0)),
                  pl.BlockSpec((tm, d_v), lambda j, i: (pr(j, i), 0)),
                  pl.BlockSpec((md, d_pool), lambda j, i: (0, 0)),
                  pl.BlockSpec((md, d_v), lambda j, i: (0, 0)),
                  pl.BlockSpec((kdim, tn), lambda j, i: (0, j)),
                  pl.BlockSpec((tm, tn), lambda j, i: (pr(j, i), j)),
                  pl.BlockSpec((md, tn), lambda j, i: (0, j)),
                  _prompt_mod_spec(md, tn, gcol),
                  _decode_mod_spec(md, tn, gcol)],
        out_specs=[pl.BlockSpec((tm, tn), lambda j, i: (pr(j, i), j)),
                   pl.BlockSpec((md, tn), lambda j, i: (0, j))],
        out_shape=[jax.ShapeDtypeStruct((mp, n), F32), jax.ShapeDtypeStruct((md, n), F32)],
        scratch_shapes=[pltpu.VMEM((kdim, tn), BF16)],
        compiler_params=_params(2),
        name="outproj",
    )(pyp, op, pyd, od, w, xp, xd, mod, mod)


def _ffn1_steps(n, tn, mt):
    return (n // tn) * (mt + 1)


def _ffn1_body(hp_ref, hd_ref, w1_ref, w3_ref, a_ref, kb_ref, v_ref, s_ref, s_partial_ref,
               op_ref, od_ref, s_out_ref, w1bf_ref, w3bf_ref):
    del s_partial_ref
    _cast_once(w1_ref, w1bf_ref)
    _cast_once(w3_ref, w3bf_ref)

    def swiglu(h_ref, o_ref):
        h = h_ref[...]
        o_ref[...] = (_silu(_dot(h, w1bf_ref[...])) * _dot(h, w3bf_ref[...])).astype(o_ref.dtype)
        s_out_ref[...] = _decode_state_update(a_ref[...], kb_ref[...], v_ref[...].astype(F32), s_ref[...])

    _row_steps(functools.partial(swiglu, hp_ref, op_ref), functools.partial(swiglu, hd_ref, od_ref))


def _ffn1(hp, hd, w1, w3, tm, tn, a, kb, vg, state, state_partial, n_pairs):
    mp, kdim = hp.shape
    md = hd.shape[0]
    n = w1.shape[1]
    mt = mp // tm
    pr = _prow(mt)
    _, n_heads, dk, dv = state.shape
    assert n_pairs <= _ffn1_steps(n, tn, mt)
    wspec = pl.BlockSpec((kdim, tn), lambda j, i: (0, j))

    def pair(j, i):
        p = jnp.minimum(j * (mt + 1) + i, n_pairs - 1)
        return p // n_heads, p % n_heads

    vec = lambda w: pl.BlockSpec((None, 1, w), lambda j, i: (pair(j, i)[0], 0, pair(j, i)[1]))
    st = pl.BlockSpec((None, None, dk, dv), lambda j, i: (*pair(j, i), 0, 0))
    return pl.pallas_call(
        _ffn1_body,
        grid=(n // tn, mt + 1),
        in_specs=[pl.BlockSpec((tm, kdim), lambda j, i: (pr(j, i), 0)),
                  pl.BlockSpec((md, kdim), lambda j, i: (0, 0)),
                  wspec, wspec, vec(dk), vec(dk), vec(dv), st,
                  pl.BlockSpec(memory_space=pl.ANY)],
        out_specs=[pl.BlockSpec((tm, tn), lambda j, i: (pr(j, i), j)),
                   pl.BlockSpec((md, tn), lambda j, i: (0, j)),
                   st],
        out_shape=[jax.ShapeDtypeStruct((mp, n), BF16), jax.ShapeDtypeStruct((md, n), BF16),
                   jax.ShapeDtypeStruct(state.shape, F32)],
        input_output_aliases={8: 2},
        scratch_shapes=[pltpu.VMEM((kdim, tn), BF16), pltpu.VMEM((kdim, tn), BF16)],
        compiler_params=_params(2),
        name="ffn1",
    )(hp, hd, w1, w3, a, kb, vg, state, state_partial)


def _ffn2_body(ap_ref, ad_ref, w_ref, xp_ref, xd_ref, gp_ref, gd_ref, outp_ref, outd_ref, wbf_ref, *, mt, tpb):
    _cast_once(w_ref, wbf_ref)
    seq = _prompt_seq(mt, tpb)

    def prompt():
        outp_ref[...] = xp_ref[...] + gp_ref[pl.ds(seq, 1), :] * _dot(ap_ref[...], wbf_ref[...])

    def decode():
        outd_ref[...] = xd_ref[...] + gd_ref[...] * _dot(ad_ref[...], wbf_ref[...])

    _row_steps(prompt, decode)


def _ffn2(ap, ad, w, xp, xd, mod, gt_chunk, tm, tn, tpb):
    mp, kdim = ap.shape
    md = ad.shape[0]
    n = w.shape[1]
    mt = mp // tm
    per = n // tn
    pr = _prow(mt)
    gcol = lambda j, i: gt_chunk * per + j
    return pl.pallas_call(
        functools.partial(_ffn2_body, mt=mt, tpb=tpb),
        grid=(per, mt + 1),
        in_specs=[pl.BlockSpec((tm, kdim), lambda j, i: (pr(j, i), 0)),
                  pl.BlockSpec((md, kdim), lambda j, i: (0, 0)),
                  pl.BlockSpec((kdim, tn), lambda j, i: (0, j), pipeline_mode=pl.Buffered(1)),
                  pl.BlockSpec((tm, tn), lambda j, i: (pr(j, i), j)),
                  pl.BlockSpec((md, tn), lambda j, i: (0, j)),
                  _prompt_mod_spec(md, tn, gcol),
                  _decode_mod_spec(md, tn, gcol)],
        out_specs=[pl.BlockSpec((tm, tn), lambda j, i: (pr(j, i), j)),
                   pl.BlockSpec((md, tn), lambda j, i: (0, j))],
        out_shape=[jax.ShapeDtypeStruct((mp, n), F32), jax.ShapeDtypeStruct((md, n), F32)],
        scratch_shapes=[pltpu.VMEM((kdim, tn), BF16)],
        compiler_params=_params(2),
        name="ffn2",
    )(ap, ad, w, xp, xd, mod, mod)


def _pool_group_matmul(d, wp_ref, ps_ref, n_groups, gw):
    outs = []
    for gi in range(n_groups):
        dg = d[:, gi * gw:(gi + 1) * gw].astype(BF16)
        outs.append(_dot(dg, wp_ref[gi].astype(BF16)))
    return jnp.concatenate(outs, axis=-1) * ps_ref[...]


def _pool_prompt_body(u_ref, wp_ref, ps_ref, y_ref, st_ref, buf_ref, *, tm, n_tiles):
    t = pl.program_id(1)
    halo = POOL_STATE + 1
    n_groups = len(POOL_WINDOWS)
    gw = u_ref.shape[1] // n_groups

    @pl.when(t == 0)
    def _():
        buf_ref[:halo, :] = jnp.zeros((halo, buf_ref.shape[1]), F32)

    u = u_ref[...]
    buf_ref[halo:, :] = u
    pos = t * tm + lax.broadcasted_iota(jnp.int32, (tm, 1), 0)
    parts = []
    for gi, w in enumerate(POOL_WINDOWS):
        lo, hi = gi * gw, (gi + 1) * gw
        win = u[:, lo:hi]
        for s in range(1, w):
            win = win + buf_ref[halo - s:halo - s + tm, lo:hi]
        cnt = jnp.minimum(w, pos + 1).astype(F32)
        parts.append(win / cnt - u[:, lo:hi])
    d = jnp.concatenate(parts, axis=-1)
    y_ref[...] = _pool_group_matmul(d, wp_ref, ps_ref, n_groups, gw).astype(y_ref.dtype)

    @pl.when(t == n_tiles - 1)
    def _():
        st_ref[...] = buf_ref[halo + tm - POOL_STATE:, :]

    buf_ref[:halo, :] = buf_ref[tm:, :]


def _pool_prompt(u, ucol, w_pool, pool_scale, nb, t_len, tm):
    m = u.shape[0]
    n_tiles = t_len // tm
    g, gw, _ = w_pool.shape
    dp = g * gw
    return pl.pallas_call(
        functools.partial(_pool_prompt_body, tm=tm, n_tiles=n_tiles),
        grid=(nb, n_tiles),
        in_specs=[pl.BlockSpec((tm, dp), lambda b, t: (b * n_tiles + t, ucol)),
                  pl.BlockSpec((g, gw, gw), lambda b, t: (0, 0, 0)),
                  pl.BlockSpec((1, dp), lambda b, t: (0, 0))],
        out_specs=[pl.BlockSpec((tm, dp), lambda b, t: (b * n_tiles + t, 0)),
                   pl.BlockSpec((None, POOL_STATE, dp), lambda b, t: (b, 0, 0))],
        out_shape=[jax.ShapeDtypeStruct((m, dp), BF16),
                   jax.ShapeDtypeStruct((nb, POOL_STATE, dp), F32)],
        scratch_shapes=[pltpu.VMEM((POOL_STATE + 1 + tm, dp), F32)],
        compiler_params=_params(2),
        name="pool_prompt",
    )(u, w_pool, pool_scale)


def _pool_decode_body(u_ref, sp_ref, wp_ref, ps_ref, y_ref, st_ref):
    n_groups = len(POOL_WINDOWS)
    gw = u_ref.shape[1] // n_groups
    u = u_ref[...]
    parts = []
    for gi, w in enumerate(POOL_WINDOWS):
        lo, hi = gi * gw, (gi + 1) * gw
        win = u[:, lo:hi]
        for s in range(1, w):
            win = win + sp_ref[POOL_STATE - s, :, lo:hi]
        cnt = float(min(w, PAST_LEN + 1))
        parts.append(win / cnt - u[:, lo:hi])
    d = jnp.concatenate(parts, axis=-1)
    y_ref[...] = _pool_group_matmul(d, wp_ref, ps_ref, n_groups, gw).astype(y_ref.dtype)
    for r in range(POOL_STATE - 1):
        st_ref[r] = sp_ref[r + 1]
    st_ref[POOL_STATE - 1] = u


def _pool_decode(u, ucol, state_t, w_pool, pool_scale, tb):
    nb = u.shape[0]
    g, gw, _ = w_pool.shape
    dp = g * gw
    return pl.pallas_call(
        _pool_decode_body,
        grid=(nb // tb,),
        in_specs=[pl.BlockSpec((tb, dp), lambda b: (b, ucol)),
                  pl.BlockSpec((POOL_STATE, tb, dp), lambda b: (0, b, 0)),
                  pl.BlockSpec((g, gw, gw), lambda b: (0, 0, 0)),
                  pl.BlockSpec((1, dp), lambda b: (0, 0))],
        out_specs=[pl.BlockSpec((tb, dp), lambda b: (b, 0)),
                   pl.BlockSpec((POOL_STATE, tb, dp), lambda b: (0, b, 0))],
        out_shape=[jax.ShapeDtypeStruct((nb, dp), BF16),
                   jax.ShapeDtypeStruct((POOL_STATE, nb, dp), F32)],
        compiler_params=_params(1),
        name="pool_decode",
    )(u, state_t, w_pool, pool_scale)


def _head_out(o, g, gain):
    return _silu(g) * _rms(o, gain)


def _column(row_vec):
    return jnp.broadcast_to(row_vec, (LANES, row_vec.shape[1])).T[:, :1]


def _cumsum_rows(x):
    n = x.shape[0]
    tri = (lax.broadcasted_iota(jnp.int32, (n, n), 0) >= lax.broadcasted_iota(jnp.int32, (n, n), 1))
    tri = jnp.where(tri, 1.0, 0.0).astype(BF16)
    hi = x.astype(BF16)
    r1 = x - hi.astype(F32)
    mid = r1.astype(BF16)
    lo = (r1 - mid.astype(F32)).astype(BF16)
    return _dot(tri, hi) + _dot(tri, mid) + _dot(tri, lo)


def _exact_scores(b_scr, q_scr, k_scr, a_scr):
    rows = b_scr.shape[0]

    def block(i, carry):
        r0 = pl.multiple_of(i * GLA_SUB, GLA_SUB)
        b_i = b_scr[pl.ds(r0, GLA_SUB), :]
        q_i = q_scr[pl.ds(r0, GLA_SUB), :]
        b_first = b_scr[pl.ds(r0, 1), :]
        qt = (q_i * jnp.exp(b_i - b_first)).astype(BF16)
        kt = (k_scr[...] * jnp.exp(jnp.minimum(b_first - b_scr[...], 0.0))).astype(BF16)
        panel = _dot(qt, kt, NT)
        col = lax.broadcasted_iota(jnp.int32, (GLA_SUB, rows), 1)
        row = lax.broadcasted_iota(jnp.int32, (GLA_SUB, rows), 0) + r0
        diag = jnp.zeros((GLA_SUB, rows), F32)
        for j in range(GLA_SUB):
            b_j = b_scr[pl.ds(r0 + j, 1), :]
            k_j = k_scr[pl.ds(r0 + j, 1), :]
            p = q_i * k_j * jnp.exp(jnp.minimum(b_i - b_j, 0.0))
            diag = jnp.where(col == r0 + j, jnp.sum(p, axis=-1, keepdims=True), diag)
        a_scr[pl.ds(r0, GLA_SUB), :] = jnp.where(col < r0, panel, jnp.where(col <= row, diag, 0.0))
        return carry

    lax.fori_loop(0, rows // GLA_SUB, block, 0)


def _gla_prompt_body(q_ref, k_ref, v_ref, g_ref, alr_ref, wa_ref, ba_ref, gn_ref, o_ref, s_ref,
                     b_all, qe_all, b_scr, q_scr, k_scr, a_scr):
    rows = q_ref.shape[0]
    dk = q_ref.shape[1] // GLA_HEADS
    dv = v_ref.shape[1] // GLA_HEADS
    scale = dk ** -0.5
    heads = [(h, slice(h * dk, (h + 1) * dk), slice(h * dv, (h + 1) * dv)) for h in range(GLA_HEADS)]

    @pl.when(pl.program_id(1) == 0)
    def _():
        s_ref[...] = jnp.zeros(s_ref.shape, F32)

    x = _dot(alr_ref[...].astype(BF16), wa_ref[...].astype(BF16)) + ba_ref[...]
    b_all[...] = _cumsum_rows(_log_sigmoid(x) / GATE_NORM)
    qe_all[...] = (q_ref[...] * scale * jnp.exp(b_all[...])).astype(BF16)

    mild = jnp.max(-b_all[rows - 1:rows, :]) < GLA_SAFE_DECAY

    @pl.when(mild)
    def _():
        causal = (lax.broadcasted_iota(jnp.int32, (rows, rows), 1)
                  <= lax.broadcasted_iota(jnp.int32, (rows, rows), 0))
        for h, ks, _ in heads:
            kinv = (k_ref[:, ks] * jnp.exp(-b_all[:, ks])).astype(BF16)
            a_scr[h] = jnp.where(causal, _dot(qe_all[:, ks], kinv, NT), 0.0)

    @pl.when(jnp.logical_not(mild))
    def _():
        for h, ks, _ in heads:
            b_scr[...] = b_all[:, ks]
            q_scr[...] = q_ref[:, ks] * scale
            k_scr[...] = k_ref[:, ks]
            _exact_scores(b_scr, q_scr, k_scr, a_scr.at[h])

    for h, ks, vs in heads:
        b = b_all[:, ks]
        b_last = b_all[rows - 1:rows, ks]
        vb = v_ref[:, vs]
        s_old = s_ref[h]
        o = _dot(qe_all[:, ks], s_old.astype(BF16)) + _dot(a_scr[h].astype(BF16), vb)
        o_ref[:, vs] = _head_out(o, g_ref[:, vs].astype(F32), gn_ref[h]).astype(o_ref.dtype)
        kd = (k_ref[:, ks] * jnp.exp(b_last - b)).astype(BF16)
        s_ref[h] = _column(jnp.exp(b_last)) * s_old + _dot(kd, vb, TN)


def _gla_prompt(qku, vg, alr, wa, ba, gn, nb, t_len):
    dkk = wa.shape[1]
    dk, dv = dkk // GLA_HEADS, gn.shape[2]
    dvv = GLA_HEADS * dv
    c = GLA_CHUNK
    n_chunks = t_len // c
    tok = lambda w, blk=0: pl.BlockSpec((c, w), lambda b, i: (b * n_chunks + i, blk))
    return pl.pallas_call(
        _gla_prompt_body,
        grid=(nb, n_chunks),
        in_specs=[tok(dkk), tok(dkk, 1), tok(dvv), tok(dvv, 1), tok(LANES),
                  pl.BlockSpec((LANES, dkk), lambda b, i: (0, 0)),
                  pl.BlockSpec((1, dkk), lambda b, i: (0, 0)),
                  pl.BlockSpec((GLA_HEADS, 1, dv), lambda b, i: (0, 0, 0))],
        out_specs=[tok(dvv),
                   pl.BlockSpec((None, GLA_HEADS, dk, dv), lambda b, i: (b, 0, 0, 0))],
        out_shape=[jax.ShapeDtypeStruct((nb * t_len, dvv), BF16),
                   jax.ShapeDtypeStruct((nb, GLA_HEADS, dk, dv), F32)],
        scratch_shapes=[pltpu.VMEM((c, dkk), F32), pltpu.VMEM((c, dkk), BF16),
                        pltpu.VMEM((c, dk), F32), pltpu.VMEM((c, dk), F32), pltpu.VMEM((c, dk), F32),
                        pltpu.VMEM((GLA_HEADS, c, c), F32)],
        compiler_params=_params(2),
        name="gla_prompt",
    )(qku, qku, vg, vg, alr, wa, ba, gn)


def _decode_state_update(a_row, kb_row, v_row, s_old):
    return _column(a_row) * s_old + _column(kb_row) * v_row


def _gla_decode_body(q_ref, k_ref, v_ref, g_ref, alr_ref, wa_ref, ba_ref, gn_ref, s_ref,
                     o_ref, a_ref, kb_ref, s_out_ref, *, first_updated):
    dk = q_ref.shape[1] // GLA_HEADS
    dv = v_ref.shape[1] // GLA_HEADS
    alr = alr_ref[...].astype(BF16)
    for h in range(GLA_HEADS):
        ks, vs = slice(h * dk, (h + 1) * dk), slice(h * dv, (h + 1) * dv)
        q = q_ref[:, ks] * (dk ** -0.5)
        kb = k_ref[:, ks].astype(BF16).astype(F32)
        vb = v_ref[:, vs].astype(F32)
        x = _dot(alr, wa_ref[:, ks].astype(BF16)) + ba_ref[:, ks]
        a = jnp.exp(_log_sigmoid(x) / GATE_NORM)
        a_ref[:, ks] = a
        kb_ref[:, ks] = kb
        o = _dot((q * a).astype(BF16), s_ref[h].astype(BF16))
        o += jnp.sum(q * kb, axis=-1, keepdims=True) * vb
        o_ref[:, vs] = _head_out(o, g_ref[:, vs].astype(F32), gn_ref[h]).astype(o_ref.dtype)

    @pl.when(pl.program_id(0) >= first_updated)
    def _():
        for h in range(GLA_HEADS):
            ks, vs = slice(h * dk, (h + 1) * dk), slice(h * dv, (h + 1) * dv)
            s_out_ref[h] = _decode_state_update(a_ref[:, ks], kb_ref[:, ks], v_ref[:, vs].astype(F32), s_ref[h])


def _gla_decode(qku, vg, alr, wa, ba, gn, state, first_updated):
    nb = qku.shape[0]
    dkk = wa.shape[1]
    dk, dv = dkk // GLA_HEADS, gn.shape[2]
    dvv = GLA_HEADS * dv
    tok = lambda w, blk=0: pl.BlockSpec((None, 1, w), lambda b: (b, 0, blk))
    st = pl.BlockSpec((None, GLA_HEADS, dk, dv), lambda b: (b, 0, 0, 0))
    st_out = pl.BlockSpec((None, GLA_HEADS, dk, dv), lambda b: (jnp.maximum(b, first_updated), 0, 0, 0))
    return pl.pallas_call(
        functools.partial(_gla_decode_body, first_updated=first_updated),
        grid=(nb,),
        in_specs=[tok(dkk), tok(dkk, 1), tok(dvv), tok(dvv, 1), tok(LANES),
                  pl.BlockSpec((LANES, dkk), lambda b: (0, 0)),
                  pl.BlockSpec((1, dkk), lambda b: (0, 0)),
                  pl.BlockSpec((GLA_HEADS, 1, dv), lambda b: (0, 0, 0)),
                  st],
        out_specs=[tok(dvv), tok(dkk), tok(dkk), st_out],
        out_shape=[jax.ShapeDtypeStruct((nb, 1, dvv), BF16),
                   jax.ShapeDtypeStruct((nb, 1, dkk), F32),
                   jax.ShapeDtypeStruct((nb, 1, dkk), F32),
                   jax.ShapeDtypeStruct(state.shape, F32)],
        compiler_params=_params(1),
        name="gla_decode",
    )(qku, qku, vg, vg, alr, wa, ba, gn, state)


def _tiles(t_len):
    tm = min(1024, t_len)
    tms = min(256, t_len)
    return tm, tms


def _layer(xp, xd, mod, lw, dims, nb, t_len, pool_state_t, gla_state):
    (g1, w_in_t, w_alr_t, w_pool, pool_scale, wa, ba, gn, w_out, g2, w_ff1, w_ff3, w_ff2) = lw
    d_pool, d_k, d_v = dims
    n_dec = xd.shape[0]
    tm, tms = _tiles(t_len)
    tpb, tpbs = t_len // tm, t_len // tms
    tn = 512

    h1p = _normmod(xp, g1, mod, n_dec, 1, 0, tms, tpbs)
    h1d = _normmod(xd, g1, mod, n_dec, 1, 0, n_dec, None)
    u_tiles, qk_tiles, vg_tiles = d_pool // tn, 2 * d_k // tn, 2 * d_v // tn
    qku_block = lambda j: jnp.where(j < qk_tiles, j + u_tiles, j - qk_tiles)
    qkup, qkud = _proj(h1p, h1d, w_in_t, qk_tiles + u_tiles, qku_block, tm, tn, F32)
    vgp, vgd = _proj(h1p, h1d, w_in_t, vg_tiles, lambda j: j + u_tiles + qk_tiles, tm, tn, BF16)
    ap, ad = _proj(h1p, h1d, w_alr_t, 1, lambda j: 0, tm, LANES, F32)
    ucol = 2 * d_k // d_pool

    pyp, pool_p = _pool_prompt(qkup, ucol, w_pool, pool_scale, nb, t_len, tms)
    op, gla_p = _gla_prompt(qkup, vgp, ap, wa, ba, gn, nb, t_len)
    pyd, pool_d_t = _pool_decode(qkud, ucol, pool_state_t, w_pool, pool_scale, 32)
    r3 = lambda t: t.reshape(t.shape[0], 1, t.shape[1])
    tn_ff = 256
    n_pairs = min((n_dec - 1) * GLA_HEADS, _ffn1_steps(w_ff1.shape[1], tn_ff, xp.shape[0] // tm))
    n_pairs -= n_pairs % GLA_HEADS
    vgd3 = r3(vgd)
    od, a_d, kb_d, gla_partial = _gla_decode(r3(qkud), vgd3, r3(ad), wa, ba, gn, gla_state,
                                             n_pairs // GLA_HEADS)
    od = od.reshape(n_dec, d_v)

    x1p, x1d = _outproj(pyp, op, pyd, od, w_out, xp, xd, mod, 2, tm, tn, tpb)
    h2p = _normmod(x1p, g2, mod, n_dec, 4, 3, tms, tpbs)
    h2d = _normmod(x1d, g2, mod, n_dec, 4, 3, n_dec, None)
    actp, actd, gla_d = _ffn1(h2p, h2d, w_ff1, w_ff3, tm, tn_ff, a_d, kb_d, vgd3, gla_state, gla_partial,
                              n_pairs)
    x2p, x2d = _ffn2(actp, actd, w_ff2, x1p, x1d, mod, 5, tms, tn, tpbs)
    return x2p, x2d, pool_p, gla_p, pool_d_t, gla_d


def kernel(x_prompt, x_sample, state_pool, state_gla, c_prompt, c_sample, w_ada, b_ada, g_norm1, w_in,
           w_pool, pool_scale, w_a2, b_a, g_gla_out, w_out, g_norm2, w_ff1, w_ff3, w_ff2, g_final):
    bp, tp, d = x_prompt.shape
    bs, ts, _ = x_sample.shape
    depth = w_ada.shape[0]
    assert ts == 1, "the decode path handles one new token per sequence"
    assert bs % SUBLANES == 0 and bp <= SUBLANES
    d_pool = w_pool.shape[1] * w_pool.shape[2]
    d_k = w_a2.shape[2]
    d_v = g_gla_out.shape[1] * g_gla_out.shape[2]
    dims = (d_pool, d_k, d_v)
    d_main = d_pool + 2 * d_k + 2 * d_v

    c_all = jnp.concatenate([c_sample, c_prompt, jnp.zeros((2 * SUBLANES - bp, d), F32)], axis=0)

    hp = x_prompt.reshape(bp * tp, d)
    hd = x_sample.reshape(bs, d)
    pools_p, glas_p, pools_d, glas_d = [], [], [], []
    for l in range(depth):
        mod = _ada(c_all, w_ada[l], b_ada[l].reshape(1, -1))
        w_in_t = w_in[l].T
        w_alr_t = jnp.pad(w_in_t[d_main:], ((0, LANES - GATE_RANK), (0, 0)))
        wa = jnp.pad(w_a2[l], ((0, LANES - GATE_RANK), (0, 0)))
        lw = (g_norm1[l].reshape(1, d), w_in_t, w_alr_t, w_pool[l], pool_scale[l].reshape(1, -1), wa,
              b_a[l].reshape(1, -1), g_gla_out[l].reshape(GLA_HEADS, 1, -1), w_out[l],
              g_norm2[l].reshape(1, d), w_ff1[l], w_ff3[l], w_ff2[l])
        pool_state_t = jnp.transpose(state_pool[l], (1, 0, 2))
        hp, hd, pool_p, gla_p, pool_d_t, gla_d = _layer(hp, hd, mod, lw, dims, bp, tp, pool_state_t,
                                                        state_gla[l])
        pools_p.append(pool_p)
        glas_p.append(gla_p)
        pools_d.append(jnp.transpose(pool_d_t, (1, 0, 2)))
        glas_d.append(gla_d)
    gf = g_final.reshape(1, d)
    y_p = _norm(hp, gf, _tiles(tp)[1]).reshape(bp, tp, d)
    y_d = _norm(hd, gf, bs).reshape(bs, ts, d)
    return (y_p, y_d, jnp.stack(pools_p), jnp.stack(glas_p), jnp.stack(pools_d), jnp.stack(glas_d))
```

```python
import functools

import jax
import jax.numpy as jnp
from jax import lax
from jax.experimental import pallas as pl
from jax.experimental.pallas import tpu as pltpu

F32 = jnp.float32
BF16 = jnp.bfloat16

POOL_WINDOWS = (2, 4, 8, 16)
POOL_STATE = max(POOL_WINDOWS) - 1
GLA_HEADS = 4
GATE_RANK = 16
GATE_NORM = 16.0
N_MOD = 6
EPS = 1e-6
PAST_LEN = 16384

LANES = 128
SUBLANES = 8
VMEM_LIMIT = 56 * 1024 * 1024
GLA_CHUNK = 256
GLA_SUB = 16
GLA_SAFE_DECAY = 60.0

NT = (((1,), (1,)), ((), ()))
TN = (((0,), (0,)), ((), ()))


def _params(n_axes):
    return pltpu.CompilerParams(dimension_semantics=("arbitrary",) * n_axes, vmem_limit_bytes=VMEM_LIMIT)


def _silu(x):
    return x * jax.nn.sigmoid(x)


def _log_sigmoid(x):
    return jnp.minimum(x, 0.0) - jnp.log(1.0 + jnp.exp(-jnp.abs(x)))


def _dot(a, b, dims=None):
    if dims is None:
        return jnp.dot(a, b, preferred_element_type=F32)
    return lax.dot_general(a, b, dims, preferred_element_type=F32)


def _ada_body(c_ref, w_ref, b_ref, o_ref):
    s = _silu(c_ref[...]).astype(BF16)
    o_ref[...] = _dot(s, w_ref[...].astype(BF16)) + b_ref[...]


def _ada(c, w, b, tn=1024):
    m, d = c.shape
    n = w.shape[1]
    return pl.pallas_call(
        _ada_body,
        grid=(n // tn,),
        in_specs=[pl.BlockSpec((m, d), lambda j: (0, 0)),
                  pl.BlockSpec((d, tn), lambda j: (0, j)),
                  pl.BlockSpec((1, tn), lambda j: (0, j))],
        out_specs=pl.BlockSpec((m, tn), lambda j: (0, j)),
        out_shape=jax.ShapeDtypeStruct((m, n), F32),
        compiler_params=_params(1),
        name="ada",
    )(c, w, b)


def _prompt_mod_spec(n_dec, width, col_of):
    return pl.BlockSpec((SUBLANES, width), lambda *g: (n_dec // SUBLANES, col_of(*g)))


def _decode_mod_spec(n_dec, width, col_of):
    return pl.BlockSpec((n_dec, width), lambda *g: (0, col_of(*g)))


def _rms(x, gain):
    return x * lax.rsqrt(jnp.mean(x * x, axis=-1, keepdims=True) + EPS) * gain


def _normmod_body(x_ref, g_ref, sc_ref, sh_ref, *rest, tpb):
    if tpb is None:
        sc, sh = sc_ref[...], sh_ref[...]
    else:
        b = pl.program_id(0) // tpb
        sc, sh = sc_ref[pl.ds(b, 1), :], sh_ref[pl.ds(b, 1), :]
    h = (_rms(x_ref[...], g_ref[...]) * (1.0 + sc) + sh).astype(BF16)
    if len(rest) == 1:
        (o_ref,) = rest
    else:
        wt_ref, o_ref, side_ref = rest
        side_ref[...] = _dot(h, wt_ref[...].astype(BF16), NT)
    o_ref[...] = h


def _normmod(x, gain, mod, n_dec, sc_chunk, sh_chunk, tm, tpb, side_wt=None):
    m, d = x.shape
    spec = _decode_mod_spec if tpb is None else _prompt_mod_spec
    in_specs = [pl.BlockSpec((tm, d), lambda i: (i, 0)),
                pl.BlockSpec((1, d), lambda i: (0, 0)),
                spec(n_dec, d, lambda i: sc_chunk),
                spec(n_dec, d, lambda i: sh_chunk)]
    out_specs = pl.BlockSpec((tm, d), lambda i: (i, 0))
    out_shape = jax.ShapeDtypeStruct((m, d), BF16)
    args = (x, gain, mod, mod)
    if side_wt is not None:
        ns = side_wt.shape[0]
        in_specs.append(pl.BlockSpec((ns, d), lambda i: (0, 0)))
        out_specs = [out_specs, pl.BlockSpec((tm, ns), lambda i: (i, 0))]
        out_shape = [out_shape, jax.ShapeDtypeStruct((m, ns), F32)]
        args += (side_wt,)
    return pl.pallas_call(
        functools.partial(_normmod_body, tpb=tpb),
        grid=(m // tm,),
        in_specs=in_specs,
        out_specs=out_specs,
        out_shape=out_shape,
        compiler_params=_params(1),
        name="normmod",
    )(*args)


def _norm_body(x_ref, g_ref, o_ref):
    o_ref[...] = _rms(x_ref[...], g_ref[...])


def _norm(x, gain, tm):
    m, d = x.shape
    return pl.pallas_call(
        _norm_body,
        grid=(m // tm,),
        in_specs=[pl.BlockSpec((tm, d), lambda i: (i, 0)),
                  pl.BlockSpec((1, d), lambda i: (0, 0))],
        out_specs=pl.BlockSpec((tm, d), lambda i: (i, 0)),
        out_shape=jax.ShapeDtypeStruct((m, d), F32),
        compiler_params=_params(1),
        name="finalnorm",
    )(x, gain)


def _cast_once(w_ref, wbf_ref):
    @pl.when(pl.program_id(1) == 0)
    def _():
        wbf_ref[...] = w_ref[...].astype(BF16)


def _row_steps(prompt_fn, decode_fn):
    i = pl.program_id(1)
    pl.when(i == 0)(decode_fn)
    pl.when(i > 0)(prompt_fn)


def _prow(mt):
    def tile(j, i):
        t = jnp.maximum(i - 1, 0)
        return jnp.where(j % 2 == 0, t, mt - 1 - t)
    return tile


def _prompt_seq(mt, tpb):
    return _prow(mt)(pl.program_id(0), pl.program_id(1)) // tpb


def _proj_body(ap_ref, ad_ref, wt_ref, op_ref, od_ref, wbf_ref):
    _cast_once(wt_ref, wbf_ref)

    def prompt():
        op_ref[...] = _dot(ap_ref[...], wbf_ref[...], NT).astype(op_ref.dtype)

    def decode():
        od_ref[...] = _dot(ad_ref[...], wbf_ref[...], NT).astype(od_ref.dtype)

    _row_steps(prompt, decode)


def _proj(ap, ad, wt, n_tiles, wblock, tm, tn, out_dtype):
    mp, kdim = ap.shape
    md = ad.shape[0]
    mt = mp // tm
    pr = _prow(mt)
    return pl.pallas_call(
        _proj_body,
        grid=(n_tiles, mt + 1),
        in_specs=[pl.BlockSpec((tm, kdim), lambda j, i: (pr(j, i), 0)),
                  pl.BlockSpec((md, kdim), lambda j, i: (0, 0)),
                  pl.BlockSpec((tn, kdim), lambda j, i: (wblock(j), 0))],
        out_specs=[pl.BlockSpec((tm, tn), lambda j, i: (pr(j, i), j)),
                   pl.BlockSpec((md, tn), lambda j, i: (0, j))],
        out_shape=[jax.ShapeDtypeStruct((mp, n_tiles * tn), out_dtype),
                   jax.ShapeDtypeStruct((md, n_tiles * tn), out_dtype)],
        scratch_shapes=[pltpu.VMEM((tn, kdim), BF16)],
        compiler_params=_params(2),
        name="proj",
    )(ap, ad, wt)


def _outproj_body(pyp_ref, op_ref, pyd_ref, od_ref, w_ref, xp_ref, xd_ref, gp_ref, gd_ref,
                  outp_ref, outd_ref, wbf_ref, *, mt, tpb, d_pool):
    _cast_once(w_ref, wbf_ref)
    seq = _prompt_seq(mt, tpb)

    def mix(py_ref, o_ref):
        return _dot(py_ref[...], wbf_ref[:d_pool, :]) + _dot(o_ref[...], wbf_ref[d_pool:, :])

    def prompt():
        outp_ref[...] = xp_ref[...] + gp_ref[pl.ds(seq, 1), :] * mix(pyp_ref, op_ref)

    def decode():
        outd_ref[...] = xd_ref[...] + gd_ref[...] * mix(pyd_ref, od_ref)

    _row_steps(prompt, decode)


def _outproj(pyp, op, pyd, od, w, xp, xd, mod, gt_chunk, tm, tn, tpb):
    mp, d_pool = pyp.shape
    md = pyd.shape[0]
    d_v = op.shape[1]
    kdim, n = w.shape
    mt = mp // tm
    per = n // tn
    pr = _prow(mt)
    gcol = lambda j, i: gt_chunk * per + j
    return pl.pallas_call(
        functools.partial(_outproj_body, mt=mt, tpb=tpb, d_pool=d_pool),
        grid=(per, mt + 1),
        in_specs=[pl.BlockSpec((tm, d_pool), lambda j, i: (pr(j, i), 0)),
                  pl.BlockSpec((tm, d_v), lambda j, i: (pr(j, i), 0)),
                  pl.BlockSpec((md, d_pool), lambda j, i: (0, 0)),
                  pl.BlockSpec((md, d_v), lambda j, i: (0, 0)),
                  pl.BlockSpec((kdim, tn), lambda j, i: (0, j)),
                  pl.BlockSpec((tm, tn), lambda j, i: (pr(j, i), j)),
                  pl.BlockSpec((md, tn), lambda j, i: (0, j)),
                  _prompt_mod_spec(md, tn, gcol),
                  _decode_mod_spec(md, tn, gcol)],
        out_specs=[pl.BlockSpec((tm, tn), lambda j, i: (pr(j, i), j)),
                   pl.BlockSpec((md, tn), lambda j, i: (0, j))],
        out_shape=[jax.ShapeDtypeStruct((mp, n), F32), jax.ShapeDtypeStruct((md, n), F32)],
        scratch_shapes=[pltpu.VMEM((kdim, tn), BF16)],
        compiler_params=_params(2),
        name="outproj",
    )(pyp, op, pyd, od, w, xp, xd, mod, mod)


def _ffn1_steps(n, tn, mt):
    return (n // tn) * (mt + 1)


def _ffn1_body(hp_ref, hd_ref, w1_ref, w3_ref, a_ref, kb_ref, v_ref, s_ref, s_partial_ref,
               op_ref, od_ref, s_out_ref, w1bf_ref, w3bf_ref):
    del s_partial_ref
    _cast_once(w1_ref, w1bf_ref)
    _cast_once(w3_ref, w3bf_ref)

    def swiglu(h_ref, o_ref):
        h = h_ref[...]
        o_ref[...] = (_silu(_dot(h, w1bf_ref[...])) * _dot(h, w3bf_ref[...])).astype(o_ref.dtype)
        s_out_ref[...] = _decode_state_update(a_ref[...], kb_ref[...], v_ref[...].astype(F32), s_ref[...])

    _row_steps(functools.partial(swiglu, hp_ref, op_ref), functools.partial(swiglu, hd_ref, od_ref))


def _ffn1(hp, hd, w1, w3, tm, tn, a, kb, vg, state, state_partial, n_pairs):
    mp, kdim = hp.shape
    md = hd.shape[0]
    n = w1.shape[1]
    mt = mp // tm
    pr = _prow(mt)
    _, n_heads, dk, dv = state.shape
    assert n_pairs <= _ffn1_steps(n, tn, mt)
    wspec = pl.BlockSpec((kdim, tn), lambda j, i: (0, j))

    def pair(j, i):
        p = jnp.minimum(j * (mt + 1) + i, n_pairs - 1)
        return p // n_heads, p % n_heads

    vec = lambda w: pl.BlockSpec((None, 1, w), lambda j, i: (pair(j, i)[0], 0, pair(j, i)[1]))
    st = pl.BlockSpec((None, None, dk, dv), lambda j, i: (*pair(j, i), 0, 0))
    return pl.pallas_call(
        _ffn1_body,
        grid=(n // tn, mt + 1),
        in_specs=[pl.BlockSpec((tm, kdim), lambda j, i: (pr(j, i), 0)),
                  pl.BlockSpec((md, kdim), lambda j, i: (0, 0)),
                  wspec, wspec, vec(dk), vec(dk), vec(dv), st,
                  pl.BlockSpec(memory_space=pl.ANY)],
        out_specs=[pl.BlockSpec((tm, tn), lambda j, i: (pr(j, i), j)),
                   pl.BlockSpec((md, tn), lambda j, i: (0, j)),
                   st],
        out_shape=[jax.ShapeDtypeStruct((mp, n), BF16), jax.ShapeDtypeStruct((md, n), BF16),
                   jax.ShapeDtypeStruct(state.shape, F32)],
        input_output_aliases={8: 2},
        scratch_shapes=[pltpu.VMEM((kdim, tn), BF16), pltpu.VMEM((kdim, tn), BF16)],
        compiler_params=_params(2),
        name="ffn1",
    )(hp, hd, w1, w3, a, kb, vg, state, state_partial)


def _ffn2_body(ap_ref, ad_ref, w_ref, xp_ref, xd_ref, gp_ref, gd_ref, outp_ref, outd_ref, wbf_ref, *, mt, tpb):
    _cast_once(w_ref, wbf_ref)
    seq = _prompt_seq(mt, tpb)

    def prompt():
        outp_ref[...] = xp_ref[...] + gp_ref[pl.ds(seq, 1), :] * _dot(ap_ref[...], wbf_ref[...])

    def decode():
        outd_ref[...] = xd_ref[...] + gd_ref[...] * _dot(ad_ref[...], wbf_ref[...])

    _row_steps(prompt, decode)


def _ffn2(ap, ad, w, xp, xd, mod, gt_chunk, tm, tn, tpb):
    mp, kdim = ap.shape
    md = ad.shape[0]
    n = w.shape[1]
    mt = mp // tm
    per = n // tn
    pr = _prow(mt)
    gcol = lambda j, i: gt_chunk * per + j
    return pl.pallas_call(
        functools.partial(_ffn2_body, mt=mt, tpb=tpb),
        grid=(per, mt + 1),
        in_specs=[pl.BlockSpec((tm, kdim), lambda j, i: (pr(j, i), 0)),
                  pl.BlockSpec((md, kdim), lambda j, i: (0, 0)),
                  pl.BlockSpec((kdim, tn), lambda j, i: (0, j), pipeline_mode=pl.Buffered(1)),
                  pl.BlockSpec((tm, tn), lambda j, i: (pr(j, i), j)),
                  pl.BlockSpec((md, tn), lambda j, i: (0, j)),
                  _prompt_mod_spec(md, tn, gcol),
                  _decode_mod_spec(md, tn, gcol)],
        out_specs=[pl.BlockSpec((tm, tn), lambda j, i: (pr(j, i), j)),
                   pl.BlockSpec((md, tn), lambda j, i: (0, j))],
        out_shape=[jax.ShapeDtypeStruct((mp, n), F32), jax.ShapeDtypeStruct((md, n), F32)],
        scratch_shapes=[pltpu.VMEM((kdim, tn), BF16)],
        compiler_params=_params(2),
        name="ffn2",
    )(ap, ad, w, xp, xd, mod, mod)


def _pool_group_matmul(d, wp_ref, ps_ref, n_groups, gw):
    outs = []
    for gi in range(n_groups):
        dg = d[:, gi * gw:(gi + 1) * gw].astype(BF16)
        outs.append(_dot(dg, wp_ref[gi].astype(BF16)))
    return jnp.concatenate(outs, axis=-1) * ps_ref[...]


def _pool_prompt_body(u_ref, wp_ref, ps_ref, y_ref, st_ref, buf_ref, *, tm, n_tiles):
    t = pl.program_id(1)
    halo = POOL_STATE + 1
    n_groups = len(POOL_WINDOWS)
    gw = u_ref.shape[1] // n_groups

    @pl.when(t == 0)
    def _():
        buf_ref[:halo, :] = jnp.zeros((halo, buf_ref.shape[1]), F32)

    u = u_ref[...]
    buf_ref[halo:, :] = u
    pos = t * tm + lax.broadcasted_iota(jnp.int32, (tm, 1), 0)
    parts = []
    for gi, w in enumerate(POOL_WINDOWS):
        lo, hi = gi * gw, (gi + 1) * gw
        win = u[:, lo:hi]
        for s in range(1, w):
            win = win + buf_ref[halo - s:halo - s + tm, lo:hi]
        cnt = jnp.minimum(w, pos + 1).astype(F32)
        parts.append(win / cnt - u[:, lo:hi])
    d = jnp.concatenate(parts, axis=-1)
    y_ref[...] = _pool_group_matmul(d, wp_ref, ps_ref, n_groups, gw).astype(y_ref.dtype)

    @pl.when(t == n_tiles - 1)
    def _():
        st_ref[...] = buf_ref[halo + tm - POOL_STATE:, :]

    buf_ref[:halo, :] = buf_ref[tm:, :]


def _pool_prompt(u, ucol, w_pool, pool_scale, nb, t_len, tm):
    m = u.shape[0]
    n_tiles = t_len // tm
    g, gw, _ = w_pool.shape
    dp = g * gw
    return pl.pallas_call(
        functools.partial(_pool_prompt_body, tm=tm, n_tiles=n_tiles),
        grid=(nb, n_tiles),
        in_specs=[pl.BlockSpec((tm, dp), lambda b, t: (b * n_tiles + t, ucol)),
                  pl.BlockSpec((g, gw, gw), lambda b, t: (0, 0, 0)),
                  pl.BlockSpec((1, dp), lambda b, t: (0, 0))],
        out_specs=[pl.BlockSpec((tm, dp), lambda b, t: (b * n_tiles + t, 0)),
                   pl.BlockSpec((None, POOL_STATE, dp), lambda b, t: (b, 0, 0))],
        out_shape=[jax.ShapeDtypeStruct((m, dp), BF16),
                   jax.ShapeDtypeStruct((nb, POOL_STATE, dp), F32)],
        scratch_shapes=[pltpu.VMEM((POOL_STATE + 1 + tm, dp), F32)],
        compiler_params=_params(2),
        name="pool_prompt",
    )(u, w_pool, pool_scale)


def _pool_decode_body(u_ref, sp_ref, wp_ref, ps_ref, y_ref, st_ref):
    n_groups = len(POOL_WINDOWS)
    gw = u_ref.shape[1] // n_groups
    u = u_ref[...]
    parts = []
    for gi, w in enumerate(POOL_WINDOWS):
        lo, hi = gi * gw, (gi + 1) * gw
        win = u[:, lo:hi]
        for s in range(1, w):
            win = win + sp_ref[POOL_STATE - s, :, lo:hi]
        cnt = float(min(w, PAST_LEN + 1))
        parts.append(win / cnt - u[:, lo:hi])
    d = jnp.concatenate(parts, axis=-1)
    y_ref[...] = _pool_group_matmul(d, wp_ref, ps_ref, n_groups, gw).astype(y_ref.dtype)
    for r in range(POOL_STATE - 1):
        st_ref[r] = sp_ref[r + 1]
    st_ref[POOL_STATE - 1] = u


def _pool_decode(u, ucol, state_t, w_pool, pool_scale, tb):
    nb = u.shape[0]
    g, gw, _ = w_pool.shape
    dp = g * gw
    return pl.pallas_call(
        _pool_decode_body,
        grid=(nb // tb,),
        in_specs=[pl.BlockSpec((tb, dp), lambda b: (b, ucol)),
                  pl.BlockSpec((POOL_STATE, tb, dp), lambda b: (0, b, 0)),
                  pl.BlockSpec((g, gw, gw), lambda b: (0, 0, 0)),
                  pl.BlockSpec((1, dp), lambda b: (0, 0))],
        out_specs=[pl.BlockSpec((tb, dp), lambda b: (b, 0)),
                   pl.BlockSpec((POOL_STATE, tb, dp), lambda b: (0, b, 0))],
        out_shape=[jax.ShapeDtypeStruct((nb, dp), BF16),
                   jax.ShapeDtypeStruct((POOL_STATE, nb, dp), F32)],
        compiler_params=_params(1),
        name="pool_decode",
    )(u, state_t, w_pool, pool_scale)


def _head_out(o, g, gain):
    return _silu(g) * _rms(o, gain)


def _column(row_vec):
    return jnp.broadcast_to(row_vec, (LANES, row_vec.shape[1])).T[:, :1]


def _cumsum_rows(x):
    n = x.shape[0]
    tri = (lax.broadcasted_iota(jnp.int32, (n, n), 0) >= lax.broadcasted_iota(jnp.int32, (n, n), 1))
    tri = jnp.where(tri, 1.0, 0.0).astype(BF16)
    hi = x.astype(BF16)
    r1 = x - hi.astype(F32)
    mid = r1.astype(BF16)
    lo = (r1 - mid.astype(F32)).astype(BF16)
    return _dot(tri, hi) + _dot(tri, mid) + _dot(tri, lo)


def _exact_scores(b_scr, q_scr, k_scr, a_scr):
    rows = b_scr.shape[0]

    def block(i, carry):
        r0 = pl.multiple_of(i * GLA_SUB, GLA_SUB)
        b_i = b_scr[pl.ds(r0, GLA_SUB), :]
        q_i = q_scr[pl.ds(r0, GLA_SUB), :]
        b_first = b_scr[pl.ds(r0, 1), :]
        qt = (q_i * jnp.exp(b_i - b_first)).astype(BF16)
        kt = (k_scr[...] * jnp.exp(jnp.minimum(b_first - b_scr[...], 0.0))).astype(BF16)
        panel = _dot(qt, kt, NT)
        col = lax.broadcasted_iota(jnp.int32, (GLA_SUB, rows), 1)
        row = lax.broadcasted_iota(jnp.int32, (GLA_SUB, rows), 0) + r0
        diag = jnp.zeros((GLA_SUB, rows), F32)
        for j in range(GLA_SUB):
            b_j = b_scr[pl.ds(r0 + j, 1), :]
            k_j = k_scr[pl.ds(r0 + j, 1), :]
            p = q_i * k_j * jnp.exp(jnp.minimum(b_i - b_j, 0.0))
            diag = jnp.where(col == r0 + j, jnp.sum(p, axis=-1, keepdims=True), diag)
        a_scr[pl.ds(r0, GLA_SUB), :] = jnp.where(col < r0, panel, jnp.where(col <= row, diag, 0.0))
        return carry

    lax.fori_loop(0, rows // GLA_SUB, block, 0)


def _gla_prompt_body(q_ref, k_ref, v_ref, g_ref, alr_ref, wa_ref, ba_ref, gn_ref, o_ref, s_ref,
                     b_all, qe_all, b_scr, q_scr, k_scr, a_scr):
    rows = q_ref.shape[0]
    dk = q_ref.shape[1] // GLA_HEADS
    dv = v_ref.shape[1] // GLA_HEADS
    scale = dk ** -0.5
    heads = [(h, slice(h * dk, (h + 1) * dk), slice(h * dv, (h + 1) * dv)) for h in range(GLA_HEADS)]

    @pl.when(pl.program_id(1) == 0)
    def _():
        s_ref[...] = jnp.zeros(s_ref.shape, F32)

    x = _dot(alr_ref[...].astype(BF16), wa_ref[...].astype(BF16)) + ba_ref[...]
    b_all[...] = _cumsum_rows(_log_sigmoid(x) / GATE_NORM)
    qe_all[...] = (q_ref[...] * scale * jnp.exp(b_all[...])).astype(BF16)

    mild = jnp.max(-b_all[rows - 1:rows, :]) < GLA_SAFE_DECAY

    @pl.when(mild)
    def _():
        causal = (lax.broadcasted_iota(jnp.int32, (rows, rows), 1)
                  <= lax.broadcasted_iota(jnp.int32, (rows, rows), 0))
        for h, ks, _ in heads:
            kinv = (k_ref[:, ks] * jnp.exp(-b_all[:, ks])).astype(BF16)
            a_scr[h] = jnp.where(causal, _dot(qe_all[:, ks], kinv, NT), 0.0)

    @pl.when(jnp.logical_not(mild))
    def _():
        for h, ks, _ in heads:
            b_scr[...] = b_all[:, ks]
            q_scr[...] = q_ref[:, ks] * scale
            k_scr[...] = k_ref[:, ks]
            _exact_scores(b_scr, q_scr, k_scr, a_scr.at[h])

    for h, ks, vs in heads:
        b = b_all[:, ks]
        b_last = b_all[rows - 1:rows, ks]
        vb = v_ref[:, vs]
        s_old = s_ref[h]
        o = _dot(qe_all[:, ks], s_old.astype(BF16)) + _dot(a_scr[h].astype(BF16), vb)
        o_ref[:, vs] = _head_out(o, g_ref[:, vs].astype(F32), gn_ref[h]).astype(o_ref.dtype)
        kd = (k_ref[:, ks] * jnp.exp(b_last - b)).astype(BF16)
        s_ref[h] = _column(jnp.exp(b_last)) * s_old + _dot(kd, vb, TN)


def _gla_prompt(qku, vg, alr, wa, ba, gn, nb, t_len):
    dkk = wa.shape[1]
    dk, dv = dkk // GLA_HEADS, gn.shape[2]
    dvv = GLA_HEADS * dv
    c = GLA_CHUNK
    n_chunks = t_len // c
    tok = lambda w, blk=0: pl.BlockSpec((c, w), lambda b, i: (b * n_chunks + i, blk))
    return pl.pallas_call(
        _gla_prompt_body,
        grid=(nb, n_chunks),
        in_specs=[tok(dkk), tok(dkk, 1), tok(dvv), tok(dvv, 1), tok(LANES),
                  pl.BlockSpec((LANES, dkk), lambda b, i: (0, 0)),
                  pl.BlockSpec((1, dkk), lambda b, i: (0, 0)),
                  pl.BlockSpec((GLA_HEADS, 1, dv), lambda b, i: (0, 0, 0))],
        out_specs=[tok(dvv),
                   pl.BlockSpec((None, GLA_HEADS, dk, dv), lambda b, i: (b, 0, 0, 0))],
        out_shape=[jax.ShapeDtypeStruct((nb * t_len, dvv), BF16),
                   jax.ShapeDtypeStruct((nb, GLA_HEADS, dk, dv), F32)],
        scratch_shapes=[pltpu.VMEM((c, dkk), F32), pltpu.VMEM((c, dkk), BF16),
                        pltpu.VMEM((c, dk), F32), pltpu.VMEM((c, dk), F32), pltpu.VMEM((c, dk), F32),
                        pltpu.VMEM((GLA_HEADS, c, c), F32)],
        compiler_params=_params(2),
        name="gla_prompt",
    )(qku, qku, vg, vg, alr, wa, ba, gn)


def _decode_state_update(a_row, kb_row, v_row, s_old):
    return _column(a_row) * s_old + _column(kb_row) * v_row


def _gla_decode_body(q_ref, k_ref, v_ref, g_ref, alr_ref, wa_ref, ba_ref, gn_ref, s_ref,
                     o_ref, a_ref, kb_ref, s_out_ref, *, first_updated):
    dk = q_ref.shape[1] // GLA_HEADS
    dv = v_ref.shape[1] // GLA_HEADS
    alr = alr_ref[...].astype(BF16)
    for h in range(GLA_HEADS):
        ks, vs = slice(h * dk, (h + 1) * dk), slice(h * dv, (h + 1) * dv)
        q = q_ref[:, ks] * (dk ** -0.5)
        kb = k_ref[:, ks].astype(BF16).astype(F32)
        vb = v_ref[:, vs].astype(F32)
        x = _dot(alr, wa_ref[:, ks].astype(BF16)) + ba_ref[:, ks]
        a = jnp.exp(_log_sigmoid(x) / GATE_NORM)
        a_ref[:, ks] = a
        kb_ref[:, ks] = kb
        o = jnp.sum(_column(q * a) * s_ref[h], axis=0, keepdims=True)
        o += jnp.sum(q * kb, axis=-1, keepdims=True) * vb
        o_ref[:, vs] = _head_out(o, g_ref[:, vs].astype(F32), gn_ref[h]).astype(o_ref.dtype)

    @pl.when(pl.program_id(0) >= first_updated)
    def _():
        for h in range(GLA_HEADS):
            ks, vs = slice(h * dk, (h + 1) * dk), slice(h * dv, (h + 1) * dv)
            s_out_ref[h] = _decode_state_update(a_ref[:, ks], kb_ref[:, ks], v_ref[:, vs].astype(F32), s_ref[h])


def _gla_decode(qku, vg, alr, wa, ba, gn, state, first_updated):
    nb = qku.shape[0]
    dkk = wa.shape[1]
    dk, dv = dkk // GLA_HEADS, gn.shape[2]
    dvv = GLA_HEADS * dv
    tok = lambda w, blk=0: pl.BlockSpec((None, 1, w), lambda b: (b, 0, blk))
    st = pl.BlockSpec((None, GLA_HEADS, dk, dv), lambda b: (b, 0, 0, 0))
    st_out = pl.BlockSpec((None, GLA_HEADS, dk, dv), lambda b: (jnp.maximum(b, first_updated), 0, 0, 0))
    return pl.pallas_call(
        functools.partial(_gla_decode_body, first_updated=first_updated),
        grid=(nb,),
        in_specs=[tok(dkk), tok(dkk, 1), tok(dvv), tok(dvv, 1), tok(LANES),
                  pl.BlockSpec((LANES, dkk), lambda b: (0, 0)),
                  pl.BlockSpec((1, dkk), lambda b: (0, 0)),
                  pl.BlockSpec((GLA_HEADS, 1, dv), lambda b: (0, 0, 0)),
                  st],
        out_specs=[tok(dvv), tok(dkk), tok(dkk), st_out],
        out_shape=[jax.ShapeDtypeStruct((nb, 1, dvv), BF16),
                   jax.ShapeDtypeStruct((nb, 1, dkk), F32),
                   jax.ShapeDtypeStruct((nb, 1, dkk), F32),
                   jax.ShapeDtypeStruct(state.shape, F32)],
        compiler_params=_params(1),
        name="gla_decode",
    )(qku, qku, vg, vg, alr, wa, ba, gn, state)


def _tiles(t_len):
    tm = min(1024, t_len)
    tms = min(256, t_len)
    return tm, tms


def _layer(xp, xd, mod, lw, dims, nb, t_len, pool_state_t, gla_state):
    (g1, w_in_t, w_alr_t, w_pool, pool_scale, wa, ba, gn, w_out, g2, w_ff1, w_ff3, w_ff2) = lw
    d_pool, d_k, d_v = dims
    n_dec = xd.shape[0]
    tm, tms = _tiles(t_len)
    tpb, tpbs = t_len // tm, t_len // tms
    tn = 512

    h1p, ap = _normmod(xp, g1, mod, n_dec, 1, 0, tms, tpbs, w_alr_t)
    h1d, ad = _normmod(xd, g1, mod, n_dec, 1, 0, n_dec, None, w_alr_t)
    u_tiles, qk_tiles, vg_tiles = d_pool // tn, 2 * d_k // tn, 2 * d_v // tn
    qku_block = lambda j: jnp.where(j < qk_tiles, j + u_tiles, j - qk_tiles)
    qkup, qkud = _proj(h1p, h1d, w_in_t, qk_tiles + u_tiles, qku_block, tm, tn, F32)
    vgp, vgd = _proj(h1p, h1d, w_in_t, vg_tiles, lambda j: j + u_tiles + qk_tiles, tm, tn, BF16)
    ucol = 2 * d_k // d_pool

    pyp, pool_p = _pool_prompt(qkup, ucol, w_pool, pool_scale, nb, t_len, tms)
    op, gla_p = _gla_prompt(qkup, vgp, ap, wa, ba, gn, nb, t_len)
    pyd, pool_d_t = _pool_decode(qkud, ucol, pool_state_t, w_pool, pool_scale, 32)
    r3 = lambda t: t.reshape(t.shape[0], 1, t.shape[1])
    tn_ff = 256
    n_pairs = min((n_dec - 1) * GLA_HEADS, _ffn1_steps(w_ff1.shape[1], tn_ff, xp.shape[0] // tm))
    n_pairs -= n_pairs % GLA_HEADS
    vgd3 = r3(vgd)
    od, a_d, kb_d, gla_partial = _gla_decode(r3(qkud), vgd3, r3(ad), wa, ba, gn, gla_state,
                                             n_pairs // GLA_HEADS)
    od = od.reshape(n_dec, d_v)

    x1p, x1d = _outproj(pyp, op, pyd, od, w_out, xp, xd, mod, 2, tm, tn, tpb)
    h2p = _normmod(x1p, g2, mod, n_dec, 4, 3, tms, tpbs)
    h2d = _normmod(x1d, g2, mod, n_dec, 4, 3, n_dec, None)
    actp, actd, gla_d = _ffn1(h2p, h2d, w_ff1, w_ff3, tm, tn_ff, a_d, kb_d, vgd3, gla_state, gla_partial,
                              n_pairs)
    x2p, x2d = _ffn2(actp, actd, w_ff2, x1p, x1d, mod, 5, tms, tn, tpbs)
    return x2p, x2d, pool_p, gla_p, pool_d_t, gla_d


def kernel(x_prompt, x_sample, state_pool, state_gla, c_prompt, c_sample, w_ada, b_ada, g_norm1, w_in,
           w_pool, pool_scale, w_a2, b_a, g_gla_out, w_out, g_norm2, w_ff1, w_ff3, w_ff2, g_final):
    bp, tp, d = x_prompt.shape
    bs, ts, _ = x_sample.shape
    depth = w_ada.shape[0]
    assert ts == 1, "the decode path handles one new token per sequence"
    assert bs % SUBLANES == 0 and bp <= SUBLANES
    d_pool = w_pool.shape[1] * w_pool.shape[2]
    d_k = w_a2.shape[2]
    d_v = g_gla_out.shape[1] * g_gla_out.shape[2]
    dims = (d_pool, d_k, d_v)
    d_main = d_pool + 2 * d_k + 2 * d_v

    c_all = jnp.concatenate([c_sample, c_prompt, jnp.zeros((2 * SUBLANES - bp, d), F32)], axis=0)

    hp = x_prompt.reshape(bp * tp, d)
    hd = x_sample.reshape(bs, d)
    pools_p, glas_p, pools_d, glas_d = [], [], [], []
    for l in range(depth):
        mod = _ada(c_all, w_ada[l], b_ada[l].reshape(1, -1))
        w_in_t = w_in[l].T
        w_alr_t = jnp.pad(w_in_t[d_main:], ((0, LANES - GATE_RANK), (0, 0)))
        wa = jnp.pad(w_a2[l], ((0, LANES - GATE_RANK), (0, 0)))
        lw = (g_norm1[l].reshape(1, d), w_in_t, w_alr_t, w_pool[l], pool_scale[l].reshape(1, -1), wa,
              b_a[l].reshape(1, -1), g_gla_out[l].reshape(GLA_HEADS, 1, -1), w_out[l],
              g_norm2[l].reshape(1, d), w_ff1[l], w_ff3[l], w_ff2[l])
        pool_state_t = jnp.transpose(state_pool[l], (1, 0, 2))
        hp, hd, pool_p, gla_p, pool_d_t, gla_d = _layer(hp, hd, mod, lw, dims, bp, tp, pool_state_t,
                                                        state_gla[l])
        pools_p.append(pool_p)
        glas_p.append(gla_p)
        pools_d.append(jnp.transpose(pool_d_t, (1, 0, 2)))
        glas_d.append(gla_d)
    gf = g_final.reshape(1, d)
    y_p = _norm(hp, gf, _tiles(tp)[1]).reshape(bp, tp, d)
    y_d = _norm(hd, gf, bs).reshape(bs, ts, d)
    return (y_p, y_d, jnp.stack(pools_p), jnp.stack(glas_p), jnp.stack(pools_d), jnp.stack(glas_d))
```

```python
import functools

import jax
import jax.numpy as jnp
from jax import lax
from jax.experimental import pallas as pl
from jax.experimental.pallas import tpu as pltpu

F32 = jnp.float32
BF16 = jnp.bfloat16

POOL_WINDOWS = (2, 4, 8, 16)
POOL_STATE = max(POOL_WINDOWS) - 1
GLA_HEADS = 4
GATE_RANK = 16
GATE_NORM = 16.0
N_MOD = 6
EPS = 1e-6
PAST_LEN = 16384

LANES = 128
SUBLANES = 8
VMEM_LIMIT = 56 * 1024 * 1024
GLA_CHUNK = 256
GLA_SUB = 16
GLA_SAFE_DECAY = 60.0

NT = (((1,), (1,)), ((), ()))
TN = (((0,), (0,)), ((), ()))


def _params(n_axes):
    return pltpu.CompilerParams(dimension_semantics=("arbitrary",) * n_axes, vmem_limit_bytes=VMEM_LIMIT)


def _silu(x):
    return x * jax.nn.sigmoid(x)


def _log_sigmoid(x):
    return jnp.minimum(x, 0.0) - jnp.log(1.0 + jnp.exp(-jnp.abs(x)))


def _dot(a, b, dims=None):
    if dims is None:
        return jnp.dot(a, b, preferred_element_type=F32)
    return lax.dot_general(a, b, dims, preferred_element_type=F32)


def _ada_body(c_ref, w_ref, b_ref, o_ref):
    s = _silu(c_ref[...]).astype(BF16)
    o_ref[...] = _dot(s, w_ref[...].astype(BF16)) + b_ref[...]


def _ada(c, w, b, tn=1024):
    m, d = c.shape
    n = w.shape[1]
    return pl.pallas_call(
        _ada_body,
        grid=(n // tn,),
        in_specs=[pl.BlockSpec((m, d), lambda j: (0, 0)),
                  pl.BlockSpec((d, tn), lambda j: (0, j)),
                  pl.BlockSpec((1, tn), lambda j: (0, j))],
        out_specs=pl.BlockSpec((m, tn), lambda j: (0, j)),
        out_shape=jax.ShapeDtypeStruct((m, n), F32),
        compiler_params=_params(1),
        name="ada",
    )(c, w, b)


def _prompt_mod_spec(n_dec, width, col_of):
    return pl.BlockSpec((SUBLANES, width), lambda *g: (n_dec // SUBLANES, col_of(*g)))


def _decode_mod_spec(n_dec, width, col_of):
    return pl.BlockSpec((n_dec, width), lambda *g: (0, col_of(*g)))


def _rms(x, gain):
    return x * lax.rsqrt(jnp.mean(x * x, axis=-1, keepdims=True) + EPS) * gain


def _normmod_body(x_ref, g_ref, sc_ref, sh_ref, *rest, tpb):
    if tpb is None:
        sc, sh = sc_ref[...], sh_ref[...]
    else:
        b = pl.program_id(0) // tpb
        sc, sh = sc_ref[pl.ds(b, 1), :], sh_ref[pl.ds(b, 1), :]
    h = (_rms(x_ref[...], g_ref[...]) * (1.0 + sc) + sh).astype(BF16)
    if len(rest) == 1:
        (o_ref,) = rest
    else:
        wt_ref, o_ref, side_ref = rest
        side_ref[...] = _dot(h, wt_ref[...].astype(BF16), NT)
    o_ref[...] = h


def _normmod(x, gain, mod, n_dec, sc_chunk, sh_chunk, tm, tpb, side_wt=None):
    m, d = x.shape
    spec = _decode_mod_spec if tpb is None else _prompt_mod_spec
    in_specs = [pl.BlockSpec((tm, d), lambda i: (i, 0)),
                pl.BlockSpec((1, d), lambda i: (0, 0)),
                spec(n_dec, d, lambda i: sc_chunk),
                spec(n_dec, d, lambda i: sh_chunk)]
    out_specs = pl.BlockSpec((tm, d), lambda i: (i, 0))
    out_shape = jax.ShapeDtypeStruct((m, d), BF16)
    args = (x, gain, mod, mod)
    if side_wt is not None:
        ns = side_wt.shape[0]
        in_specs.append(pl.BlockSpec((ns, d), lambda i: (0, 0)))
        out_specs = [out_specs, pl.BlockSpec((tm, ns), lambda i: (i, 0))]
        out_shape = [out_shape, jax.ShapeDtypeStruct((m, ns), F32)]
        args += (side_wt,)
    return pl.pallas_call(
        functools.partial(_normmod_body, tpb=tpb),
        grid=(m // tm,),
        in_specs=in_specs,
        out_specs=out_specs,
        out_shape=out_shape,
        compiler_params=_params(1),
        name="normmod",
    )(*args)


def _norm_body(x_ref, g_ref, o_ref):
    o_ref[...] = _rms(x_ref[...], g_ref[...])


def _norm(x, gain, tm):
    m, d = x.shape
    return pl.pallas_call(
        _norm_body,
        grid=(m // tm,),
        in_specs=[pl.BlockSpec((tm, d), lambda i: (i, 0)),
                  pl.BlockSpec((1, d), lambda i: (0, 0))],
        out_specs=pl.BlockSpec((tm, d), lambda i: (i, 0)),
        out_shape=jax.ShapeDtypeStruct((m, d), F32),
        compiler_params=_params(1),
        name="finalnorm",
    )(x, gain)


def _cast_once(w_ref, wbf_ref):
    @pl.when(pl.program_id(1) == 0)
    def _():
        wbf_ref[...] = w_ref[...].astype(BF16)


def _row_steps(prompt_fn, decode_fn):
    i = pl.program_id(1)
    pl.when(i == 0)(decode_fn)
    pl.when(i > 0)(prompt_fn)


def _prow(mt):
    def tile(j, i):
        t = jnp.maximum(i - 1, 0)
        return jnp.where(j % 2 == 0, t, mt - 1 - t)
    return tile


def _prompt_seq(mt, tpb):
    return _prow(mt)(pl.program_id(0), pl.program_id(1)) // tpb


def _proj_body(ap_ref, ad_ref, wt_ref, op_ref, od_ref, wbf_ref):
    _cast_once(wt_ref, wbf_ref)

    def prompt():
        op_ref[...] = _dot(ap_ref[...], wbf_ref[...], NT).astype(op_ref.dtype)

    def decode():
        od_ref[...] = _dot(ad_ref[...], wbf_ref[...], NT).astype(od_ref.dtype)

    _row_steps(prompt, decode)


def _proj(ap, ad, wt, n_tiles, wblock, tm, tn, out_dtype):
    mp, kdim = ap.shape
    md = ad.shape[0]
    mt = mp // tm
    pr = _prow(mt)
    return pl.pallas_call(
        _proj_body,
        grid=(n_tiles, mt + 1),
        in_specs=[pl.BlockSpec((tm, kdim), lambda j, i: (pr(j, i), 0)),
                  pl.BlockSpec((md, kdim), lambda j, i: (0, 0)),
                  pl.BlockSpec((tn, kdim), lambda j, i: (wblock(j), 0))],
        out_specs=[pl.BlockSpec((tm, tn), lambda j, i: (pr(j, i), j)),
                   pl.BlockSpec((md, tn), lambda j, i: (0, j))],
        out_shape=[jax.ShapeDtypeStruct((mp, n_tiles * tn), out_dtype),
                   jax.ShapeDtypeStruct((md, n_tiles * tn), out_dtype)],
        scratch_shapes=[pltpu.VMEM((tn, kdim), BF16)],
        compiler_params=_params(2),
        name="proj",
    )(ap, ad, wt)


def _outproj_body(pyp_ref, op_ref, pyd_ref, od_ref, w_ref, xp_ref, xd_ref, gp_ref, gd_ref,
                  outp_ref, outd_ref, wbf_ref, *, mt, tpb, d_pool):
    _cast_once(w_ref, wbf_ref)
    seq = _prompt_seq(mt, tpb)

    def mix(py_ref, o_ref):
        return _dot(py_ref[...], wbf_ref[:d_pool, :]) + _dot(o_ref[...], wbf_ref[d_pool:, :])

    def prompt():
        outp_ref[...] = xp_ref[...] + gp_ref[pl.ds(seq, 1), :] * mix(pyp_ref, op_ref)

    def decode():
        outd_ref[...] = xd_ref[...] + gd_ref[...] * mix(pyd_ref, od_ref)

    _row_steps(prompt, decode)


def _outproj(pyp, op, pyd, od, w, xp, xd, mod, gt_chunk, tm, tn, tpb):
    mp, d_pool = pyp.shape
    md = pyd.shape[0]
    d_v = op.shape[1]
    kdim, n = w.shape
    mt = mp // tm
    per = n // tn
    pr = _prow(mt)
    gcol = lambda j, i: gt_chunk * per + j
    return pl.pallas_call(
        functools.partial(_outproj_body, mt=mt, tpb=tpb, d_pool=d_pool),
        grid=(per, mt + 1),
        in_specs=[pl.BlockSpec((tm, d_pool), lambda j, i: (pr(j, i), 0)),
                  pl.BlockSpec((tm, d_v), lambda j, i: (pr(j, i), 0)),
                  pl.BlockSpec((md, d_pool), lambda j, i: (0, 0)),
                  pl.BlockSpec((md, d_v), lambda j, i: (0, 0)),
                  pl.BlockSpec((kdim, tn), lambda j, i: (0, j)),
                  pl.BlockSpec((tm, tn), lambda j, i: (pr(j, i), j)),
                  pl.BlockSpec((md, tn), lambda j, i: (0, j)),
                  _prompt_mod_spec(md, tn, gcol),
                  _decode_mod_spec(md, tn, gcol)],
        out_specs=[pl.BlockSpec((tm, tn), lambda j, i: (pr(j, i), j)),
                   pl.BlockSpec((md, tn), lambda j, i: (0, j))],
        out_shape=[jax.ShapeDtypeStruct((mp, n), F32), jax.ShapeDtypeStruct((md, n), F32)],
        scratch_shapes=[pltpu.VMEM((kdim, tn), BF16)],
        compiler_params=_params(2),
        name="outproj",
    )(pyp, op, pyd, od, w, xp, xd, mod, mod)


def _state_block_rows(n_rows, dk, n_steps):
    rb = -(-n_rows // n_steps)
    rb += -rb % LANES
    while n_rows % rb:
        rb += LANES
    assert dk % LANES == 0 and rb <= dk + LANES, "a row block may touch at most two (sequence, head) pairs"
    return rb


def _state_job(a_ref, kb_ref, v0_ref, v1_ref, s_ref, s_out_ref, rows_of_first_pair):
    in_first = lax.broadcasted_iota(jnp.int32, (s_ref.shape[0], 1), 0) < rows_of_first_pair
    v = jnp.where(in_first, v0_ref[...].astype(F32), v1_ref[...].astype(F32))
    s_out_ref[...] = _column(a_ref[...]) * s_ref[...] + _column(kb_ref[...]) * v


def _ffn1_body(hp_ref, hd_ref, w1_ref, w3_ref, a_ref, kb_ref, v0_ref, v1_ref, s_ref,
               op_ref, od_ref, s_out_ref, w1bf_ref, w3bf_ref, *, dk, n_blocks):
    _cast_once(w1_ref, w1bf_ref)
    _cast_once(w3_ref, w3bf_ref)
    blk = jnp.minimum(pl.program_id(0) * pl.num_programs(1) + pl.program_id(1), n_blocks - 1)
    rows_of_first_pair = dk - (blk * s_ref.shape[0]) % dk

    def swiglu(h_ref, o_ref):
        h = h_ref[...]
        o_ref[...] = (_silu(_dot(h, w1bf_ref[...])) * _dot(h, w3bf_ref[...])).astype(o_ref.dtype)
        _state_job(a_ref, kb_ref, v0_ref, v1_ref, s_ref, s_out_ref, rows_of_first_pair)

    _row_steps(functools.partial(swiglu, hp_ref, op_ref), functools.partial(swiglu, hd_ref, od_ref))


def _ffn1(hp, hd, w1, w3, tm, tn, a, kb, vg, state):
    mp, kdim = hp.shape
    md = hd.shape[0]
    n = w1.shape[1]
    mt = mp // tm
    pr = _prow(mt)
    n_seq, n_heads, dk, dv = state.shape
    n_rows = n_seq * n_heads * dk
    rb = _state_block_rows(n_rows, dk, (n // tn) * (mt + 1))
    n_blocks = n_rows // rb
    block = lambda j, i: jnp.minimum(j * (mt + 1) + i, n_blocks - 1)

    def value_row(which):
        def index(j, i):
            p = jnp.minimum(block(j, i) * rb // dk + which, n_seq * n_heads - 1)
            return p // n_heads, 0, p % n_heads
        return pl.BlockSpec((None, 1, dv), index)

    row_vec = pl.BlockSpec((None, 1, rb), lambda j, i: (block(j, i), 0, 0))
    st = pl.BlockSpec((rb, dv), lambda j, i: (block(j, i), 0))
    wspec = pl.BlockSpec((kdim, tn), lambda j, i: (0, j))
    act_p, act_d, state_new = pl.pallas_call(
        functools.partial(_ffn1_body, dk=dk, n_blocks=n_blocks),
        grid=(n // tn, mt + 1),
        in_specs=[pl.BlockSpec((tm, kdim), lambda j, i: (pr(j, i), 0)),
                  pl.BlockSpec((md, kdim), lambda j, i: (0, 0)),
                  wspec, wspec, row_vec, row_vec, value_row(0), value_row(1), st],
        out_specs=[pl.BlockSpec((tm, tn), lambda j, i: (pr(j, i), j)),
                   pl.BlockSpec((md, tn), lambda j, i: (0, j)),
                   st],
        out_shape=[jax.ShapeDtypeStruct((mp, n), BF16), jax.ShapeDtypeStruct((md, n), BF16),
                   jax.ShapeDtypeStruct((n_rows, dv), F32)],
        scratch_shapes=[pltpu.VMEM((kdim, tn), BF16), pltpu.VMEM((kdim, tn), BF16)],
        compiler_params=_params(2),
        name="ffn1",
    )(hp, hd, w1, w3, a.reshape(n_blocks, 1, rb), kb.reshape(n_blocks, 1, rb), vg, vg,
      state.reshape(n_rows, dv))
    return act_p, act_d, state_new.reshape(state.shape)


FFN2_WEIGHT_CHUNKS = 8


def _ffn2_body(ap_ref, ad_ref, w_hbm, xp_ref, xd_ref, gp_ref, gd_ref, outp_ref, outd_ref,
               wbf_ref, stage_ref, sem, *, mt, tpb):
    j, i = pl.program_id(0), pl.program_id(1)
    kc, tn = stage_ref.shape[1], stage_ref.shape[2]
    n_chunks = wbf_ref.shape[1] // kc
    slot = j % 2

    def chunk(col_tile, c):
        return pltpu.make_async_copy(w_hbm.at[pl.ds(c * kc, kc), pl.ds(col_tile * tn, tn)],
                                     stage_ref.at[c % 2], sem.at[c % 2])

    def land(dst_slot, c):
        wbf_ref[dst_slot, pl.ds(c * kc, kc), :] = stage_ref[c % 2].astype(BF16)

    @pl.when((j == 0) & (i == 0))
    def _():
        for c in range(n_chunks):
            chunk(0, c).start()
            chunk(0, c).wait()
            land(0, c)

    @pl.when(j + 1 < pl.num_programs(0))
    def _():
        @pl.when(i == 0)
        def _():
            chunk(j + 1, 0).start()

        for c in range(n_chunks):
            @pl.when(i == c + 1)
            def _(c=c):
                chunk(j + 1, c).wait()
                if c + 1 < n_chunks:
                    chunk(j + 1, c + 1).start()
                land(1 - slot, c)

    seq = _prompt_seq(mt, tpb)

    def prompt():
        outp_ref[...] = xp_ref[...] + gp_ref[pl.ds(seq, 1), :] * _dot(ap_ref[...], wbf_ref[slot])

    def decode():
        outd_ref[...] = xd_ref[...] + gd_ref[...] * _dot(ad_ref[...], wbf_ref[slot])

    _row_steps(prompt, decode)


def _ffn2(ap, ad, w, xp, xd, mod, gt_chunk, tm, tn, tpb):
    mp, kdim = ap.shape
    md = ad.shape[0]
    n = w.shape[1]
    mt = mp // tm
    per = n // tn
    pr = _prow(mt)
    kc = kdim // FFN2_WEIGHT_CHUNKS
    assert kc * FFN2_WEIGHT_CHUNKS == kdim and kc % 16 == 0 and mt >= FFN2_WEIGHT_CHUNKS
    gcol = lambda j, i: gt_chunk * per + j
    return pl.pallas_call(
        functools.partial(_ffn2_body, mt=mt, tpb=tpb),
        grid=(per, mt + 1),
        in_specs=[pl.BlockSpec((tm, kdim), lambda j, i: (pr(j, i), 0)),
                  pl.BlockSpec((md, kdim), lambda j, i: (0, 0), pipeline_mode=pl.Buffered(1)),
                  pl.BlockSpec(memory_space=pl.ANY),
                  pl.BlockSpec((tm, tn), lambda j, i: (pr(j, i), j)),
                  pl.BlockSpec((md, tn), lambda j, i: (0, j)),
                  _prompt_mod_spec(md, tn, gcol),
                  _decode_mod_spec(md, tn, gcol)],
        out_specs=[pl.BlockSpec((tm, tn), lambda j, i: (pr(j, i), j)),
                   pl.BlockSpec((md, tn), lambda j, i: (0, j))],
        out_shape=[jax.ShapeDtypeStruct((mp, n), F32), jax.ShapeDtypeStruct((md, n), F32)],
        scratch_shapes=[pltpu.VMEM((2, kdim, tn), BF16), pltpu.VMEM((2, kc, tn), F32),
                        pltpu.SemaphoreType.DMA((2,))],
        compiler_params=_params(2),
        name="ffn2",
    )(ap, ad, w, xp, xd, mod, mod)


def _pool_group_matmul(d, wp_ref, ps_ref, n_groups, gw):
    outs = []
    for gi in range(n_groups):
        dg = d[:, gi * gw:(gi + 1) * gw].astype(BF16)
        outs.append(_dot(dg, wp_ref[gi].astype(BF16)))
    return jnp.concatenate(outs, axis=-1) * ps_ref[...]


def _pool_prompt_body(u_ref, wp_ref, ps_ref, y_ref, st_ref, buf_ref, *, tm, n_tiles):
    t = pl.program_id(1)
    halo = POOL_STATE + 1
    n_groups = len(POOL_WINDOWS)
    gw = u_ref.shape[1] // n_groups

    @pl.when(t == 0)
    def _():
        buf_ref[:halo, :] = jnp.zeros((halo, buf_ref.shape[1]), F32)

    u = u_ref[...]
    buf_ref[halo:, :] = u
    pos = t * tm + lax.broadcasted_iota(jnp.int32, (tm, 1), 0)
    parts = []
    for gi, w in enumerate(POOL_WINDOWS):
        lo, hi = gi * gw, (gi + 1) * gw
        win = u[:, lo:hi]
        for s in range(1, w):
            win = win + buf_ref[halo - s:halo - s + tm, lo:hi]
        cnt = jnp.minimum(w, pos + 1).astype(F32)
        parts.append(win / cnt - u[:, lo:hi])
    d = jnp.concatenate(parts, axis=-1)
    y_ref[...] = _pool_group_matmul(d, wp_ref, ps_ref, n_groups, gw).astype(y_ref.dtype)

    @pl.when(t == n_tiles - 1)
    def _():
        st_ref[...] = buf_ref[halo + tm - POOL_STATE:, :]

    buf_ref[:halo, :] = buf_ref[tm:, :]


def _pool_prompt(u, ucol, w_pool, pool_scale, nb, t_len, tm):
    m = u.shape[0]
    n_tiles = t_len // tm
    g, gw, _ = w_pool.shape
    dp = g * gw
    return pl.pallas_call(
        functools.partial(_pool_prompt_body, tm=tm, n_tiles=n_tiles),
        grid=(nb, n_tiles),
        in_specs=[pl.BlockSpec((tm, dp), lambda b, t: (b * n_tiles + t, ucol)),
                  pl.BlockSpec((g, gw, gw), lambda b, t: (0, 0, 0)),
                  pl.BlockSpec((1, dp), lambda b, t: (0, 0))],
        out_specs=[pl.BlockSpec((tm, dp), lambda b, t: (b * n_tiles + t, 0)),
                   pl.BlockSpec((None, POOL_STATE, dp), lambda b, t: (b, 0, 0))],
        out_shape=[jax.ShapeDtypeStruct((m, dp), BF16),
                   jax.ShapeDtypeStruct((nb, POOL_STATE, dp), F32)],
        scratch_shapes=[pltpu.VMEM((POOL_STATE + 1 + tm, dp), F32)],
        compiler_params=_params(2),
        name="pool_prompt",
    )(u, w_pool, pool_scale)


def _pool_decode_body(u_ref, sp_ref, wp_ref, ps_ref, y_ref, st_ref):
    n_groups = len(POOL_WINDOWS)
    gw = u_ref.shape[1] // n_groups
    u = u_ref[...]
    parts = []
    for gi, w in enumerate(POOL_WINDOWS):
        lo, hi = gi * gw, (gi + 1) * gw
        win = u[:, lo:hi]
        for s in range(1, w):
            win = win + sp_ref[POOL_STATE - s, :, lo:hi]
        cnt = float(min(w, PAST_LEN + 1))
        parts.append(win / cnt - u[:, lo:hi])
    d = jnp.concatenate(parts, axis=-1)
    y_ref[...] = _pool_group_matmul(d, wp_ref, ps_ref, n_groups, gw).astype(y_ref.dtype)
    for r in range(POOL_STATE - 1):
        st_ref[r] = sp_ref[r + 1]
    st_ref[POOL_STATE - 1] = u


def _pool_decode(u, ucol, state_t, w_pool, pool_scale, tb):
    nb = u.shape[0]
    g, gw, _ = w_pool.shape
    dp = g * gw
    return pl.pallas_call(
        _pool_decode_body,
        grid=(nb // tb,),
        in_specs=[pl.BlockSpec((tb, dp), lambda b: (b, ucol)),
                  pl.BlockSpec((POOL_STATE, tb, dp), lambda b: (0, b, 0)),
                  pl.BlockSpec((g, gw, gw), lambda b: (0, 0, 0)),
                  pl.BlockSpec((1, dp), lambda b: (0, 0))],
        out_specs=[pl.BlockSpec((tb, dp), lambda b: (b, 0)),
                   pl.BlockSpec((POOL_STATE, tb, dp), lambda b: (0, b, 0))],
        out_shape=[jax.ShapeDtypeStruct((nb, dp), BF16),
                   jax.ShapeDtypeStruct((POOL_STATE, nb, dp), F32)],
        compiler_params=_params(1),
        name="pool_decode",
    )(u, state_t, w_pool, pool_scale)


def _head_out(o, g, gain):
    return _silu(g) * _rms(o, gain)


def _column(row_vec):
    return jnp.broadcast_to(row_vec, (LANES, row_vec.shape[1])).T[:, :1]


def _cumsum_rows(x):
    n = x.shape[0]
    tri = (lax.broadcasted_iota(jnp.int32, (n, n), 0) >= lax.broadcasted_iota(jnp.int32, (n, n), 1))
    tri = jnp.where(tri, 1.0, 0.0).astype(BF16)
    hi = x.astype(BF16)
    r1 = x - hi.astype(F32)
    mid = r1.astype(BF16)
    lo = (r1 - mid.astype(F32)).astype(BF16)
    return _dot(tri, hi) + _dot(tri, mid) + _dot(tri, lo)


def _exact_scores(b_scr, q_scr, k_scr, a_scr):
    rows = b_scr.shape[0]

    def block(i, carry):
        r0 = pl.multiple_of(i * GLA_SUB, GLA_SUB)
        b_i = b_scr[pl.ds(r0, GLA_SUB), :]
        q_i = q_scr[pl.ds(r0, GLA_SUB), :]
        b_first = b_scr[pl.ds(r0, 1), :]
        qt = (q_i * jnp.exp(b_i - b_first)).astype(BF16)
        kt = (k_scr[...] * jnp.exp(jnp.minimum(b_first - b_scr[...], 0.0))).astype(BF16)
        panel = _dot(qt, kt, NT)
        col = lax.broadcasted_iota(jnp.int32, (GLA_SUB, rows), 1)
        row = lax.broadcasted_iota(jnp.int32, (GLA_SUB, rows), 0) + r0
        diag = jnp.zeros((GLA_SUB, rows), F32)
        for j in range(GLA_SUB):
            b_j = b_scr[pl.ds(r0 + j, 1), :]
            k_j = k_scr[pl.ds(r0 + j, 1), :]
            p = q_i * k_j * jnp.exp(jnp.minimum(b_i - b_j, 0.0))
            diag = jnp.where(col == r0 + j, jnp.sum(p, axis=-1, keepdims=True), diag)
        a_scr[pl.ds(r0, GLA_SUB), :] = jnp.where(col < r0, panel, jnp.where(col <= row, diag, 0.0))
        return carry

    lax.fori_loop(0, rows // GLA_SUB, block, 0)


def _gla_prompt_body(q_ref, k_ref, v_ref, g_ref, alr_ref, wa_ref, ba_ref, gn_ref, o_ref, s_ref,
                     b_all, qe_all, b_scr, q_scr, k_scr, a_scr):
    rows = q_ref.shape[0]
    dk = q_ref.shape[1] // GLA_HEADS
    dv = v_ref.shape[1] // GLA_HEADS
    scale = dk ** -0.5
    heads = [(h, slice(h * dk, (h + 1) * dk), slice(h * dv, (h + 1) * dv)) for h in range(GLA_HEADS)]

    @pl.when(pl.program_id(1) == 0)
    def _():
        s_ref[...] = jnp.zeros(s_ref.shape, F32)

    x = _dot(alr_ref[...].astype(BF16), wa_ref[...].astype(BF16)) + ba_ref[...]
    b_all[...] = _cumsum_rows(_log_sigmoid(x) / GATE_NORM)
    qe_all[...] = (q_ref[...] * scale * jnp.exp(b_all[...])).astype(BF16)

    mild = jnp.max(-b_all[rows - 1:rows, :]) < GLA_SAFE_DECAY

    @pl.when(mild)
    def _():
        causal = (lax.broadcasted_iota(jnp.int32, (rows, rows), 1)
                  <= lax.broadcasted_iota(jnp.int32, (rows, rows), 0))
        for h, ks, _ in heads:
            kinv = (k_ref[:, ks] * jnp.exp(-b_all[:, ks])).astype(BF16)
            a_scr[h] = jnp.where(causal, _dot(qe_all[:, ks], kinv, NT), 0.0)

    @pl.when(jnp.logical_not(mild))
    def _():
        for h, ks, _ in heads:
            b_scr[...] = b_all[:, ks]
            q_scr[...] = q_ref[:, ks] * scale
            k_scr[...] = k_ref[:, ks]
            _exact_scores(b_scr, q_scr, k_scr, a_scr.at[h])

    for h, ks, vs in heads:
        b = b_all[:, ks]
        b_last = b_all[rows - 1:rows, ks]
        vb = v_ref[:, vs]
        s_old = s_ref[h]
        o = _dot(qe_all[:, ks], s_old.astype(BF16)) + _dot(a_scr[h].astype(BF16), vb)
        o_ref[:, vs] = _head_out(o, g_ref[:, vs].astype(F32), gn_ref[h]).astype(o_ref.dtype)
        kd = (k_ref[:, ks] * jnp.exp(b_last - b)).astype(BF16)
        s_ref[h] = _column(jnp.exp(b_last)) * s_old + _dot(kd, vb, TN)


def _gla_prompt(qku, vg, alr, wa, ba, gn, nb, t_len):
    dkk = wa.shape[1]
    dk, dv = dkk // GLA_HEADS, gn.shape[2]
    dvv = GLA_HEADS * dv
    c = GLA_CHUNK
    n_chunks = t_len // c
    tok = lambda w, blk=0: pl.BlockSpec((c, w), lambda b, i: (b * n_chunks + i, blk))
    return pl.pallas_call(
        _gla_prompt_body,
        grid=(nb, n_chunks),
        in_specs=[tok(dkk), tok(dkk, 1), tok(dvv), tok(dvv, 1), tok(LANES),
                  pl.BlockSpec((LANES, dkk), lambda b, i: (0, 0)),
                  pl.BlockSpec((1, dkk), lambda b, i: (0, 0)),
                  pl.BlockSpec((GLA_HEADS, 1, dv), lambda b, i: (0, 0, 0))],
        out_specs=[tok(dvv),
                   pl.BlockSpec((None, GLA_HEADS, dk, dv), lambda b, i: (b, 0, 0, 0))],
        out_shape=[jax.ShapeDtypeStruct((nb * t_len, dvv), BF16),
                   jax.ShapeDtypeStruct((nb, GLA_HEADS, dk, dv), F32)],
        scratch_shapes=[pltpu.VMEM((c, dkk), F32), pltpu.VMEM((c, dkk), BF16),
                        pltpu.VMEM((c, dk), F32), pltpu.VMEM((c, dk), F32), pltpu.VMEM((c, dk), F32),
                        pltpu.VMEM((GLA_HEADS, c, c), F32)],
        compiler_params=_params(2),
        name="gla_prompt",
    )(qku, qku, vg, vg, alr, wa, ba, gn)


def _gla_decode_body(q_ref, k_ref, v_ref, g_ref, alr_ref, wa_ref, ba_ref, gn_ref, s_ref, o_ref, a_ref, kb_ref):
    dk = q_ref.shape[2] // GLA_HEADS
    dv = v_ref.shape[2] // GLA_HEADS
    for r in range(q_ref.shape[0]):
        alr = alr_ref[r].astype(BF16)
        for h in range(GLA_HEADS):
            ks, vs = slice(h * dk, (h + 1) * dk), slice(h * dv, (h + 1) * dv)
            q = q_ref[r, :, ks] * (dk ** -0.5)
            kb = k_ref[r, :, ks].astype(BF16).astype(F32)
            vb = v_ref[r, :, vs].astype(F32)
            x = _dot(alr, wa_ref[:, ks].astype(BF16)) + ba_ref[:, ks]
            a = jnp.exp(_log_sigmoid(x) / GATE_NORM)
            a_ref[r, :, ks] = a
            kb_ref[r, :, ks] = kb
            o = jnp.sum(_column(q * a) * s_ref[r, h], axis=0, keepdims=True)
            o += jnp.sum(q * kb, axis=-1, keepdims=True) * vb
            o_ref[r, :, vs] = _head_out(o, g_ref[r, :, vs].astype(F32), gn_ref[h]).astype(o_ref.dtype)


def _gla_decode(qku, vg, alr, wa, ba, gn, state, tb=2):
    nb = qku.shape[0]
    dkk = wa.shape[1]
    dk, dv = dkk // GLA_HEADS, gn.shape[2]
    dvv = GLA_HEADS * dv
    tok = lambda w, blk=0: pl.BlockSpec((tb, 1, w), lambda b: (b, 0, blk))
    return pl.pallas_call(
        _gla_decode_body,
        grid=(nb // tb,),
        in_specs=[tok(dkk), tok(dkk, 1), tok(dvv), tok(dvv, 1), tok(LANES),
                  pl.BlockSpec((LANES, dkk), lambda b: (0, 0)),
                  pl.BlockSpec((1, dkk), lambda b: (0, 0)),
                  pl.BlockSpec((GLA_HEADS, 1, dv), lambda b: (0, 0, 0)),
                  pl.BlockSpec((tb, GLA_HEADS, dk, dv), lambda b: (b, 0, 0, 0))],
        out_specs=[tok(dvv), tok(dkk), tok(dkk)],
        out_shape=[jax.ShapeDtypeStruct((nb, 1, dvv), BF16),
                   jax.ShapeDtypeStruct((nb, 1, dkk), F32),
                   jax.ShapeDtypeStruct((nb, 1, dkk), F32)],
        compiler_params=_params(1),
        name="gla_decode",
    )(qku, qku, vg, vg, alr, wa, ba, gn, state)


def _tiles(t_len):
    return min(1024, t_len), min(256, t_len), min(512, t_len)


def _layer(xp, xd, mod, lw, dims, nb, t_len, pool_state_t, gla_state):
    (g1, w_in_t, w_alr_t, w_pool, pool_scale, wa, ba, gn, w_out, g2, w_ff1, w_ff3, w_ff2) = lw
    d_pool, d_k, d_v = dims
    n_dec = xd.shape[0]
    tm, tms, tmn = _tiles(t_len)
    tpb, tpbs, tpbn = t_len // tm, t_len // tms, t_len // tmn
    tn = 512

    h1p, ap = _normmod(xp, g1, mod, n_dec, 1, 0, tmn, tpbn, w_alr_t)
    h1d, ad = _normmod(xd, g1, mod, n_dec, 1, 0, n_dec, None, w_alr_t)
    u_tiles, qk_tiles, vg_tiles = d_pool // tn, 2 * d_k // tn, 2 * d_v // tn
    qku_block = lambda j: jnp.where(j < qk_tiles, j + u_tiles, j - qk_tiles)
    qkup, qkud = _proj(h1p, h1d, w_in_t, qk_tiles + u_tiles, qku_block, tm, tn, F32)
    vgp, vgd = _proj(h1p, h1d, w_in_t, vg_tiles, lambda j: j + u_tiles + qk_tiles, tm, tn, BF16)
    ucol = 2 * d_k // d_pool

    pyp, pool_p = _pool_prompt(qkup, ucol, w_pool, pool_scale, nb, t_len, tms)
    op, gla_p = _gla_prompt(qkup, vgp, ap, wa, ba, gn, nb, t_len)
    pyd, pool_d_t = _pool_decode(qkud, ucol, pool_state_t, w_pool, pool_scale, 32)
    r3 = lambda t: t.reshape(t.shape[0], 1, t.shape[1])
    vgd3 = r3(vgd)
    od, a_d, kb_d = _gla_decode(r3(qkud), vgd3, r3(ad), wa, ba, gn, gla_state)
    od = od.reshape(n_dec, d_v)

    x1p, x1d = _outproj(pyp, op, pyd, od, w_out, xp, xd, mod, 2, tm, tn, tpb)
    h2p = _normmod(x1p, g2, mod, n_dec, 4, 3, tmn, tpbn)
    h2d = _normmod(x1d, g2, mod, n_dec, 4, 3, n_dec, None)
    actp, actd, gla_d = _ffn1(h2p, h2d, w_ff1, w_ff3, tm, 256, a_d, kb_d, vgd3, gla_state)
    x2p, x2d = _ffn2(actp, actd, w_ff2, x1p, x1d, mod, 5, tms, tn, tpbs)
    return x2p, x2d, pool_p, gla_p, pool_d_t, gla_d


def kernel(x_prompt, x_sample, state_pool, state_gla, c_prompt, c_sample, w_ada, b_ada, g_norm1, w_in,
           w_pool, pool_scale, w_a2, b_a, g_gla_out, w_out, g_norm2, w_ff1, w_ff3, w_ff2, g_final):
    bp, tp, d = x_prompt.shape
    bs, ts, _ = x_sample.shape
    depth = w_ada.shape[0]
    assert ts == 1, "the decode path handles one new token per sequence"
    assert bs % SUBLANES == 0 and bp <= SUBLANES
    d_pool = w_pool.shape[1] * w_pool.shape[2]
    d_k = w_a2.shape[2]
    d_v = g_gla_out.shape[1] * g_gla_out.shape[2]
    dims = (d_pool, d_k, d_v)
    d_main = d_pool + 2 * d_k + 2 * d_v

    c_all = jnp.concatenate([c_sample, c_prompt, jnp.zeros((2 * SUBLANES - bp, d), F32)], axis=0)

    hp = x_prompt.reshape(bp * tp, d)
    hd = x_sample.reshape(bs, d)
    pools_p, glas_p, pools_d, glas_d = [], [], [], []
    for l in range(depth):
        mod = _ada(c_all, w_ada[l], b_ada[l].reshape(1, -1))
        w_in_t = w_in[l].T
        w_alr_t = jnp.pad(w_in_t[d_main:], ((0, LANES - GATE_RANK), (0, 0)))
        wa = jnp.pad(w_a2[l], ((0, LANES - GATE_RANK), (0, 0)))
        lw = (g_norm1[l].reshape(1, d), w_in_t, w_alr_t, w_pool[l], pool_scale[l].reshape(1, -1), wa,
              b_a[l].reshape(1, -1), g_gla_out[l].reshape(GLA_HEADS, 1, -1), w_out[l],
              g_norm2[l].reshape(1, d), w_ff1[l], w_ff3[l], w_ff2[l])
        pool_state_t = jnp.transpose(state_pool[l], (1, 0, 2))
        hp, hd, pool_p, gla_p, pool_d_t, gla_d = _layer(hp, hd, mod, lw, dims, bp, tp, pool_state_t,
                                                        state_gla[l])
        pools_p.append(pool_p)
        glas_p.append(gla_p)
        pools_d.append(jnp.transpose(pool_d_t, (1, 0, 2)))
        glas_d.append(gla_d)
    gf = g_final.reshape(1, d)
    y_p = _norm(hp, gf, _tiles(tp)[2]).reshape(bp, tp, d)
    y_d = _norm(hd, gf, bs).reshape(bs, ts, d)
    return (y_p, y_d, jnp.stack(pools_p), jnp.stack(glas_p), jnp.stack(pools_d), jnp.stack(glas_d))
```

```python
import functools

import jax
import jax.numpy as jnp
from jax import lax
from jax.experimental import pallas as pl
from jax.experimental.pallas import tpu as pltpu

F32 = jnp.float32
BF16 = jnp.bfloat16

POOL_WINDOWS = (2, 4, 8, 16)
POOL_STATE = max(POOL_WINDOWS) - 1
GLA_HEADS = 4
GATE_RANK = 16
GATE_NORM = 16.0
N_MOD = 6
EPS = 1e-6
PAST_LEN = 16384

LANES = 128
SUBLANES = 8
VMEM_LIMIT = 56 * 1024 * 1024
GLA_CHUNK = 256
GLA_SUB = 16
GLA_SAFE_DECAY = 60.0

NT = (((1,), (1,)), ((), ()))
TN = (((0,), (0,)), ((), ()))


def _params(n_axes):
    return pltpu.CompilerParams(dimension_semantics=("arbitrary",) * n_axes, vmem_limit_bytes=VMEM_LIMIT)


def _silu(x):
    return x * jax.nn.sigmoid(x)


def _log_sigmoid(x):
    return jnp.minimum(x, 0.0) - jnp.log(1.0 + jnp.exp(-jnp.abs(x)))


def _dot(a, b, dims=None):
    if dims is None:
        return jnp.dot(a, b, preferred_element_type=F32)
    return lax.dot_general(a, b, dims, preferred_element_type=F32)


def _ada_body(c_ref, w_ref, b_ref, o_ref):
    s = _silu(c_ref[...]).astype(BF16)
    o_ref[...] = _dot(s, w_ref[...].astype(BF16)) + b_ref[...]


def _ada(c, w, b, tn=1024):
    m, d = c.shape
    n = w.shape[1]
    return pl.pallas_call(
        _ada_body,
        grid=(n // tn,),
        in_specs=[pl.BlockSpec((m, d), lambda j: (0, 0)),
                  pl.BlockSpec((d, tn), lambda j: (0, j)),
                  pl.BlockSpec((1, tn), lambda j: (0, j))],
        out_specs=pl.BlockSpec((m, tn), lambda j: (0, j)),
        out_shape=jax.ShapeDtypeStruct((m, n), F32),
        compiler_params=_params(1),
        name="ada",
    )(c, w, b)


def _prompt_mod_spec(n_dec, width, col_of):
    return pl.BlockSpec((SUBLANES, width), lambda *g: (n_dec // SUBLANES, col_of(*g)))


def _decode_mod_spec(n_dec, width, col_of):
    return pl.BlockSpec((n_dec, width), lambda *g: (0, col_of(*g)))


def _rms(x, gain):
    return x * lax.rsqrt(jnp.mean(x * x, axis=-1, keepdims=True) + EPS) * gain


def _normmod_body(x_ref, g_ref, sc_ref, sh_ref, *rest, tpb):
    if tpb is None:
        sc, sh = sc_ref[...], sh_ref[...]
    else:
        b = pl.program_id(0) // tpb
        sc, sh = sc_ref[pl.ds(b, 1), :], sh_ref[pl.ds(b, 1), :]
    h = (_rms(x_ref[...], g_ref[...]) * (1.0 + sc) + sh).astype(BF16)
    if len(rest) == 1:
        (o_ref,) = rest
    else:
        wt_ref, o_ref, side_ref = rest
        side_ref[...] = _dot(h, wt_ref[...].astype(BF16), NT)
    o_ref[...] = h


def _normmod(x, gain, mod, n_dec, sc_chunk, sh_chunk, tm, tpb, side_wt=None):
    m, d = x.shape
    spec = _decode_mod_spec if tpb is None else _prompt_mod_spec
    in_specs = [pl.BlockSpec((tm, d), lambda i: (i, 0)),
                pl.BlockSpec((1, d), lambda i: (0, 0)),
                spec(n_dec, d, lambda i: sc_chunk),
                spec(n_dec, d, lambda i: sh_chunk)]
    out_specs = pl.BlockSpec((tm, d), lambda i: (i, 0))
    out_shape = jax.ShapeDtypeStruct((m, d), BF16)
    args = (x, gain, mod, mod)
    if side_wt is not None:
        ns = side_wt.shape[0]
        in_specs.append(pl.BlockSpec((ns, d), lambda i: (0, 0)))
        out_specs = [out_specs, pl.BlockSpec((tm, ns), lambda i: (i, 0))]
        out_shape = [out_shape, jax.ShapeDtypeStruct((m, ns), F32)]
        args += (side_wt,)
    return pl.pallas_call(
        functools.partial(_normmod_body, tpb=tpb),
        grid=(m // tm,),
        in_specs=in_specs,
        out_specs=out_specs,
        out_shape=out_shape,
        compiler_params=_params(1),
        name="normmod",
    )(*args)


def _norm_body(x_ref, g_ref, o_ref):
    o_ref[...] = _rms(x_ref[...], g_ref[...])


def _norm(x, gain, tm):
    m, d = x.shape
    return pl.pallas_call(
        _norm_body,
        grid=(m // tm,),
        in_specs=[pl.BlockSpec((tm, d), lambda i: (i, 0)),
                  pl.BlockSpec((1, d), lambda i: (0, 0))],
        out_specs=pl.BlockSpec((tm, d), lambda i: (i, 0)),
        out_shape=jax.ShapeDtypeStruct((m, d), F32),
        compiler_params=_params(1),
        name="finalnorm",
    )(x, gain)


def _cast_once(w_ref, wbf_ref):
    @pl.when(pl.program_id(1) == 0)
    def _():
        wbf_ref[...] = w_ref[...].astype(BF16)


def _row_steps(prompt_fn, decode_fn):
    i = pl.program_id(1)
    pl.when(i == 0)(decode_fn)
    pl.when(i > 0)(prompt_fn)


def _prow(mt):
    def tile(j, i):
        t = jnp.maximum(i - 1, 0)
        return jnp.where(j % 2 == 0, t, mt - 1 - t)
    return tile


def _prompt_seq(mt, tpb):
    return _prow(mt)(pl.program_id(0), pl.program_id(1)) // tpb


def _proj_body(ap_ref, ad_ref, wt_ref, op_ref, od_ref, wbf_ref):
    _cast_once(wt_ref, wbf_ref)

    def prompt():
        op_ref[...] = _dot(ap_ref[...], wbf_ref[...], NT).astype(op_ref.dtype)

    def decode():
        od_ref[...] = _dot(ad_ref[...], wbf_ref[...], NT).astype(od_ref.dtype)

    _row_steps(prompt, decode)


def _proj(ap, ad, wt, n_tiles, wblock, tm, tn, out_dtype):
    mp, kdim = ap.shape
    md = ad.shape[0]
    mt = mp // tm
    pr = _prow(mt)
    return pl.pallas_call(
        _proj_body,
        grid=(n_tiles, mt + 1),
        in_specs=[pl.BlockSpec((tm, kdim), lambda j, i: (pr(j, i), 0)),
                  pl.BlockSpec((md, kdim), lambda j, i: (0, 0)),
                  pl.BlockSpec((tn, kdim), lambda j, i: (wblock(j), 0))],
        out_specs=[pl.BlockSpec((tm, tn), lambda j, i: (pr(j, i), j)),
                   pl.BlockSpec((md, tn), lambda j, i: (0, j))],
        out_shape=[jax.ShapeDtypeStruct((mp, n_tiles * tn), out_dtype),
                   jax.ShapeDtypeStruct((md, n_tiles * tn), out_dtype)],
        scratch_shapes=[pltpu.VMEM((tn, kdim), BF16)],
        compiler_params=_params(2),
        name="proj",
    )(ap, ad, wt)


def _outproj_body(pyp_ref, op_ref, pyd_ref, od_ref, w_ref, xp_ref, xd_ref, gp_ref, gd_ref,
                  outp_ref, outd_ref, wbf_ref, *, mt, tpb, d_pool):
    _cast_once(w_ref, wbf_ref)
    seq = _prompt_seq(mt, tpb)

    def mix(py_ref, o_ref):
        return _dot(py_ref[...], wbf_ref[:d_pool, :]) + _dot(o_ref[...], wbf_ref[d_pool:, :])

    def prompt():
        outp_ref[...] = xp_ref[...] + gp_ref[pl.ds(seq, 1), :] * mix(pyp_ref, op_ref)

    def decode():
        outd_ref[...] = xd_ref[...] + gd_ref[...] * mix(pyd_ref, od_ref)

    _row_steps(prompt, decode)


def _outproj(pyp, op, pyd, od, w, xp, xd, mod, gt_chunk, tm, tn, tpb):
    mp, d_pool = pyp.shape
    md = pyd.shape[0]
    d_v = op.shape[1]
    kdim, n = w.shape
    mt = mp // tm
    per = n // tn
    pr = _prow(mt)
    gcol = lambda j, i: gt_chunk * per + j
    return pl.pallas_call(
        functools.partial(_outproj_body, mt=mt, tpb=tpb, d_pool=d_pool),
        grid=(per, mt + 1),
        in_specs=[pl.BlockSpec((tm, d_pool), lambda j, i: (pr(j, i), 0)),
                  pl.BlockSpec((tm, d_v), lambda j, i: (pr(j, i), 0)),
                  pl.BlockSpec((md, d_pool), lambda j, i: (0, 0)),
                  pl.BlockSpec((md, d_v), lambda j, i: (0, 0)),
                  pl.BlockSpec((kdim, tn), lambda j, i: (0, j)),
                  pl.BlockSpec((tm, tn), lambda j, i: (pr(j, i), j)),
                  pl.BlockSpec((md, tn), lambda j, i: (0, j)),
                  _prompt_mod_spec(md, tn, gcol),
                  _decode_mod_spec(md, tn, gcol)],
        out_specs=[pl.BlockSpec((tm, tn), lambda j, i: (pr(j, i), j)),
                   pl.BlockSpec((md, tn), lambda j, i: (0, j))],
        out_shape=[jax.ShapeDtypeStruct((mp, n), F32), jax.ShapeDtypeStruct((md, n), F32)],
        scratch_shapes=[pltpu.VMEM((kdim, tn), BF16)],
        compiler_params=_params(2),
        name="outproj",
    )(pyp, op, pyd, od, w, xp, xd, mod, mod)


def _state_block_rows(n_rows, dk, n_steps):
    rb = -(-n_rows // n_steps)
    rb += -rb % LANES
    while n_rows % rb:
        rb += LANES
    assert dk % LANES == 0 and rb <= dk + LANES, "a row block may touch at most two (sequence, head) pairs"
    return rb


def _state_job(a_ref, kb_ref, v0_ref, v1_ref, s_ref, s_out_ref, rows_of_first_pair):
    in_first = lax.broadcasted_iota(jnp.int32, (s_ref.shape[0], 1), 0) < rows_of_first_pair
    v = jnp.where(in_first, v0_ref[...].astype(F32), v1_ref[...].astype(F32))
    s_out_ref[...] = _column(a_ref[...]) * s_ref[...] + _column(kb_ref[...]) * v


def _ffn1_body(hp_ref, hd_ref, w1_ref, w3_ref, a_ref, kb_ref, v0_ref, v1_ref, s_ref,
               op_ref, od_ref, s_out_ref, w1bf_ref, w3bf_ref, *, dk, n_blocks):
    _cast_once(w1_ref, w1bf_ref)
    _cast_once(w3_ref, w3bf_ref)
    blk = jnp.minimum(pl.program_id(0) * pl.num_programs(1) + pl.program_id(1), n_blocks - 1)
    rows_of_first_pair = dk - (blk * s_ref.shape[0]) % dk

    def swiglu(h_ref, o_ref):
        h = h_ref[...]
        o_ref[...] = (_silu(_dot(h, w1bf_ref[...])) * _dot(h, w3bf_ref[...])).astype(o_ref.dtype)
        _state_job(a_ref, kb_ref, v0_ref, v1_ref, s_ref, s_out_ref, rows_of_first_pair)

    _row_steps(functools.partial(swiglu, hp_ref, op_ref), functools.partial(swiglu, hd_ref, od_ref))


def _ffn1(hp, hd, w1, w3, tm, tn, a, kb, vg, state):
    mp, kdim = hp.shape
    md = hd.shape[0]
    n = w1.shape[1]
    mt = mp // tm
    pr = _prow(mt)
    n_seq, n_heads, dk, dv = state.shape
    n_rows = n_seq * n_heads * dk
    rb = _state_block_rows(n_rows, dk, (n // tn) * (mt + 1))
    n_blocks = n_rows // rb
    block = lambda j, i: jnp.minimum(j * (mt + 1) + i, n_blocks - 1)

    def value_row(which):
        def index(j, i):
            p = jnp.minimum(block(j, i) * rb // dk + which, n_seq * n_heads - 1)
            return p // n_heads, 0, p % n_heads
        return pl.BlockSpec((None, 1, dv), index)

    row_vec = pl.BlockSpec((None, 1, rb), lambda j, i: (block(j, i), 0, 0))
    st = pl.BlockSpec((rb, dv), lambda j, i: (block(j, i), 0))
    wspec = pl.BlockSpec((kdim, tn), lambda j, i: (0, j))
    act_p, act_d, state_new = pl.pallas_call(
        functools.partial(_ffn1_body, dk=dk, n_blocks=n_blocks),
        grid=(n // tn, mt + 1),
        in_specs=[pl.BlockSpec((tm, kdim), lambda j, i: (pr(j, i), 0)),
                  pl.BlockSpec((md, kdim), lambda j, i: (0, 0)),
                  wspec, wspec, row_vec, row_vec, value_row(0), value_row(1), st],
        out_specs=[pl.BlockSpec((tm, tn), lambda j, i: (pr(j, i), j)),
                   pl.BlockSpec((md, tn), lambda j, i: (0, j)),
                   st],
        out_shape=[jax.ShapeDtypeStruct((mp, n), BF16), jax.ShapeDtypeStruct((md, n), BF16),
                   jax.ShapeDtypeStruct((n_rows, dv), F32)],
        scratch_shapes=[pltpu.VMEM((kdim, tn), BF16), pltpu.VMEM((kdim, tn), BF16)],
        compiler_params=_params(2),
        name="ffn1",
    )(hp, hd, w1, w3, a.reshape(n_blocks, 1, rb), kb.reshape(n_blocks, 1, rb), vg, vg,
      state.reshape(n_rows, dv))
    return act_p, act_d, state_new.reshape(state.shape)


FFN2_WEIGHT_CHUNKS = 8


def _ffn2_body(ap_ref, ad_ref, w_hbm, xp_ref, xd_ref, gp_ref, gd_ref, outp_ref, outd_ref,
               wbf_ref, stage_ref, sem, *, mt, tpb):
    j, i = pl.program_id(0), pl.program_id(1)
    kc, tn = stage_ref.shape[1], stage_ref.shape[2]
    n_chunks = wbf_ref.shape[1] // kc
    slot = j % 2

    def chunk(col_tile, c):
        return pltpu.make_async_copy(w_hbm.at[pl.ds(c * kc, kc), pl.ds(col_tile * tn, tn)],
                                     stage_ref.at[c % 2], sem.at[c % 2])

    def land(dst_slot, c):
        wbf_ref[dst_slot, pl.ds(c * kc, kc), :] = stage_ref[c % 2].astype(BF16)

    @pl.when((j == 0) & (i == 0))
    def _():
        for c in range(n_chunks):
            chunk(0, c).start()
            chunk(0, c).wait()
            land(0, c)

    @pl.when(j + 1 < pl.num_programs(0))
    def _():
        @pl.when(i == 0)
        def _():
            chunk(j + 1, 0).start()

        for c in range(n_chunks):
            @pl.when(i == c + 1)
            def _(c=c):
                chunk(j + 1, c).wait()
                if c + 1 < n_chunks:
                    chunk(j + 1, c + 1).start()
                land(1 - slot, c)

    seq = _prompt_seq(mt, tpb)

    def prompt():
        outp_ref[...] = xp_ref[...] + gp_ref[pl.ds(seq, 1), :] * _dot(ap_ref[...], wbf_ref[slot])

    def decode():
        outd_ref[...] = xd_ref[...] + gd_ref[...] * _dot(ad_ref[...], wbf_ref[slot])

    _row_steps(prompt, decode)


def _ffn2(ap, ad, w, xp, xd, mod, gt_chunk, tm, tn, tpb):
    mp, kdim = ap.shape
    md = ad.shape[0]
    n = w.shape[1]
    mt = mp // tm
    per = n // tn
    pr = _prow(mt)
    kc = kdim // FFN2_WEIGHT_CHUNKS
    assert kc * FFN2_WEIGHT_CHUNKS == kdim and kc % 16 == 0 and mt >= FFN2_WEIGHT_CHUNKS
    gcol = lambda j, i: gt_chunk * per + j
    return pl.pallas_call(
        functools.partial(_ffn2_body, mt=mt, tpb=tpb),
        grid=(per, mt + 1),
        in_specs=[pl.BlockSpec((tm, kdim), lambda j, i: (pr(j, i), 0)),
                  pl.BlockSpec((md, kdim), lambda j, i: (0, 0), pipeline_mode=pl.Buffered(1)),
                  pl.BlockSpec(memory_space=pl.ANY),
                  pl.BlockSpec((tm, tn), lambda j, i: (pr(j, i), j)),
                  pl.BlockSpec((md, tn), lambda j, i: (0, j)),
                  _prompt_mod_spec(md, tn, gcol),
                  _decode_mod_spec(md, tn, gcol)],
        out_specs=[pl.BlockSpec((tm, tn), lambda j, i: (pr(j, i), j)),
                   pl.BlockSpec((md, tn), lambda j, i: (0, j))],
        out_shape=[jax.ShapeDtypeStruct((mp, n), F32), jax.ShapeDtypeStruct((md, n), F32)],
        scratch_shapes=[pltpu.VMEM((2, kdim, tn), BF16), pltpu.VMEM((2, kc, tn), F32),
                        pltpu.SemaphoreType.DMA((2,))],
        compiler_params=_params(2),
        name="ffn2",
    )(ap, ad, w, xp, xd, mod, mod)


def _pool_group_matmul(d, wp_ref, ps_ref, n_groups, gw):
    outs = []
    for gi in range(n_groups):
        dg = d[:, gi * gw:(gi + 1) * gw].astype(BF16)
        outs.append(_dot(dg, wp_ref[gi].astype(BF16)))
    return jnp.concatenate(outs, axis=-1) * ps_ref[...]


def _pool_prompt_tile(t, n_tiles, u_ref, wp_ref, ps_ref, y_ref, st_ref, buf_ref):
    tm = u_ref.shape[0]
    halo = POOL_STATE + 1
    n_groups = len(POOL_WINDOWS)
    gw = u_ref.shape[1] // n_groups

    @pl.when(t == 0)
    def _():
        buf_ref[:halo, :] = jnp.zeros((halo, buf_ref.shape[1]), F32)

    u = u_ref[...]
    buf_ref[halo:, :] = u
    pos = t * tm + lax.broadcasted_iota(jnp.int32, (tm, 1), 0)
    parts = []
    for gi, w in enumerate(POOL_WINDOWS):
        lo, hi = gi * gw, (gi + 1) * gw
        win = u[:, lo:hi]
        for s in range(1, w):
            win = win + buf_ref[halo - s:halo - s + tm, lo:hi]
        cnt = jnp.minimum(w, pos + 1).astype(F32)
        parts.append(win / cnt - u[:, lo:hi])
    d = jnp.concatenate(parts, axis=-1)
    y_ref[...] = _pool_group_matmul(d, wp_ref, ps_ref, n_groups, gw).astype(y_ref.dtype)

    @pl.when(t == n_tiles - 1)
    def _():
        st_ref[...] = buf_ref[halo + tm - POOL_STATE:, :]

    buf_ref[:halo, :] = buf_ref[tm:, :]


def _pool_decode_body(u_ref, sp_ref, wp_ref, ps_ref, y_ref, st_ref):
    n_groups = len(POOL_WINDOWS)
    gw = u_ref.shape[1] // n_groups
    u = u_ref[...]
    parts = []
    for gi, w in enumerate(POOL_WINDOWS):
        lo, hi = gi * gw, (gi + 1) * gw
        win = u[:, lo:hi]
        for s in range(1, w):
            win = win + sp_ref[POOL_STATE - s, :, lo:hi]
        cnt = float(min(w, PAST_LEN + 1))
        parts.append(win / cnt - u[:, lo:hi])
    d = jnp.concatenate(parts, axis=-1)
    y_ref[...] = _pool_group_matmul(d, wp_ref, ps_ref, n_groups, gw).astype(y_ref.dtype)
    for r in range(POOL_STATE - 1):
        st_ref[r] = sp_ref[r + 1]
    st_ref[POOL_STATE - 1] = u


def _pool_decode(u, ucol, state_t, w_pool, pool_scale, tb):
    nb = u.shape[0]
    g, gw, _ = w_pool.shape
    dp = g * gw
    return pl.pallas_call(
        _pool_decode_body,
        grid=(nb // tb,),
        in_specs=[pl.BlockSpec((tb, dp), lambda b: (b, ucol)),
                  pl.BlockSpec((POOL_STATE, tb, dp), lambda b: (0, b, 0)),
                  pl.BlockSpec((g, gw, gw), lambda b: (0, 0, 0)),
                  pl.BlockSpec((1, dp), lambda b: (0, 0))],
        out_specs=[pl.BlockSpec((tb, dp), lambda b: (b, 0)),
                   pl.BlockSpec((POOL_STATE, tb, dp), lambda b: (0, b, 0))],
        out_shape=[jax.ShapeDtypeStruct((nb, dp), BF16),
                   jax.ShapeDtypeStruct((POOL_STATE, nb, dp), F32)],
        compiler_params=_params(1),
        name="pool_decode",
    )(u, state_t, w_pool, pool_scale)


def _head_out(o, g, gain):
    return _silu(g) * _rms(o, gain)


def _column(row_vec):
    return jnp.broadcast_to(row_vec, (LANES, row_vec.shape[1])).T[:, :1]


def _cumsum_rows(x):
    n = x.shape[0]
    tri = (lax.broadcasted_iota(jnp.int32, (n, n), 0) >= lax.broadcasted_iota(jnp.int32, (n, n), 1))
    tri = jnp.where(tri, 1.0, 0.0).astype(BF16)
    hi = x.astype(BF16)
    r1 = x - hi.astype(F32)
    mid = r1.astype(BF16)
    lo = (r1 - mid.astype(F32)).astype(BF16)
    return _dot(tri, hi) + _dot(tri, mid) + _dot(tri, lo)


def _exact_scores(b_scr, q_scr, k_scr, a_scr):
    rows = b_scr.shape[0]

    def block(i, carry):
        r0 = pl.multiple_of(i * GLA_SUB, GLA_SUB)
        b_i = b_scr[pl.ds(r0, GLA_SUB), :]
        q_i = q_scr[pl.ds(r0, GLA_SUB), :]
        b_first = b_scr[pl.ds(r0, 1), :]
        qt = (q_i * jnp.exp(b_i - b_first)).astype(BF16)
        kt = (k_scr[...] * jnp.exp(jnp.minimum(b_first - b_scr[...], 0.0))).astype(BF16)
        panel = _dot(qt, kt, NT)
        col = lax.broadcasted_iota(jnp.int32, (GLA_SUB, rows), 1)
        row = lax.broadcasted_iota(jnp.int32, (GLA_SUB, rows), 0) + r0
        diag = jnp.zeros((GLA_SUB, rows), F32)
        for j in range(GLA_SUB):
            b_j = b_scr[pl.ds(r0 + j, 1), :]
            k_j = k_scr[pl.ds(r0 + j, 1), :]
            p = q_i * k_j * jnp.exp(jnp.minimum(b_i - b_j, 0.0))
            diag = jnp.where(col == r0 + j, jnp.sum(p, axis=-1, keepdims=True), diag)
        a_scr[pl.ds(r0, GLA_SUB), :] = jnp.where(col < r0, panel, jnp.where(col <= row, diag, 0.0))
        return carry

    lax.fori_loop(0, rows // GLA_SUB, block, 0)


def _gla_prompt_body(q_ref, k_ref, v_ref, g_ref, alr_ref, wa_ref, ba_ref, gn_ref, u_ref, wp_ref, ps_ref,
                     o_ref, s_ref, py_ref, pst_ref,
                     b_all, qe_all, b_scr, q_scr, k_scr, a_scr, pbuf_ref, *, n_chunks):
    _pool_prompt_tile(pl.program_id(1), n_chunks, u_ref, wp_ref, ps_ref, py_ref, pst_ref, pbuf_ref)
    rows = q_ref.shape[0]
    dk = q_ref.shape[1] // GLA_HEADS
    dv = v_ref.shape[1] // GLA_HEADS
    scale = dk ** -0.5
    heads = [(h, slice(h * dk, (h + 1) * dk), slice(h * dv, (h + 1) * dv)) for h in range(GLA_HEADS)]

    @pl.when(pl.program_id(1) == 0)
    def _():
        s_ref[...] = jnp.zeros(s_ref.shape, F32)

    x = _dot(alr_ref[...].astype(BF16), wa_ref[...].astype(BF16)) + ba_ref[...]
    b_all[...] = _cumsum_rows(_log_sigmoid(x) / GATE_NORM)
    qe_all[...] = (q_ref[...] * scale * jnp.exp(b_all[...])).astype(BF16)

    mild = jnp.max(-b_all[rows - 1:rows, :]) < GLA_SAFE_DECAY

    @pl.when(mild)
    def _():
        causal = (lax.broadcasted_iota(jnp.int32, (rows, rows), 1)
                  <= lax.broadcasted_iota(jnp.int32, (rows, rows), 0))
        for h, ks, _ in heads:
            kinv = (k_ref[:, ks] * jnp.exp(-b_all[:, ks])).astype(BF16)
            a_scr[h] = jnp.where(causal, _dot(qe_all[:, ks], kinv, NT), 0.0)

    @pl.when(jnp.logical_not(mild))
    def _():
        for h, ks, _ in heads:
            b_scr[...] = b_all[:, ks]
            q_scr[...] = q_ref[:, ks] * scale
            k_scr[...] = k_ref[:, ks]
            _exact_scores(b_scr, q_scr, k_scr, a_scr.at[h])

    for h, ks, vs in heads:
        b = b_all[:, ks]
        b_last = b_all[rows - 1:rows, ks]
        vb = v_ref[:, vs]
        s_old = s_ref[h]
        o = _dot(qe_all[:, ks], s_old.astype(BF16)) + _dot(a_scr[h].astype(BF16), vb)
        o_ref[:, vs] = _head_out(o, g_ref[:, vs].astype(F32), gn_ref[h]).astype(o_ref.dtype)
        kd = (k_ref[:, ks] * jnp.exp(b_last - b)).astype(BF16)
        s_ref[h] = _column(jnp.exp(b_last)) * s_old + _dot(kd, vb, TN)


def _mixers_prompt(qku, ucol, vg, alr, wa, ba, gn, w_pool, pool_scale, nb, t_len):
    dkk = wa.shape[1]
    dk, dv = dkk // GLA_HEADS, gn.shape[2]
    dvv = GLA_HEADS * dv
    g, gw, _ = w_pool.shape
    dp = g * gw
    c = GLA_CHUNK
    n_chunks = t_len // c
    tok = lambda w, blk=0: pl.BlockSpec((c, w), lambda b, i: (b * n_chunks + i, blk))
    return pl.pallas_call(
        functools.partial(_gla_prompt_body, n_chunks=n_chunks),
        grid=(nb, n_chunks),
        in_specs=[tok(dkk), tok(dkk, 1), tok(dvv), tok(dvv, 1), tok(LANES),
                  pl.BlockSpec((LANES, dkk), lambda b, i: (0, 0)),
                  pl.BlockSpec((1, dkk), lambda b, i: (0, 0)),
                  pl.BlockSpec((GLA_HEADS, 1, dv), lambda b, i: (0, 0, 0)),
                  tok(dp, ucol),
                  pl.BlockSpec((g, gw, gw), lambda b, i: (0, 0, 0)),
                  pl.BlockSpec((1, dp), lambda b, i: (0, 0))],
        out_specs=[tok(dvv),
                   pl.BlockSpec((None, GLA_HEADS, dk, dv), lambda b, i: (b, 0, 0, 0)),
                   tok(dp),
                   pl.BlockSpec((None, POOL_STATE, dp), lambda b, i: (b, 0, 0))],
        out_shape=[jax.ShapeDtypeStruct((nb * t_len, dvv), BF16),
                   jax.ShapeDtypeStruct((nb, GLA_HEADS, dk, dv), F32),
                   jax.ShapeDtypeStruct((nb * t_len, dp), BF16),
                   jax.ShapeDtypeStruct((nb, POOL_STATE, dp), F32)],
        scratch_shapes=[pltpu.VMEM((c, dkk), F32), pltpu.VMEM((c, dkk), BF16),
                        pltpu.VMEM((c, dk), F32), pltpu.VMEM((c, dk), F32), pltpu.VMEM((c, dk), F32),
                        pltpu.VMEM((GLA_HEADS, c, c), F32),
                        pltpu.VMEM((POOL_STATE + 1 + c, dp), F32)],
        compiler_params=_params(2),
        name="mixers_prompt",
    )(qku, qku, vg, vg, alr, wa, ba, gn, qku, w_pool, pool_scale)


def _gla_decode_body(q_ref, k_ref, v_ref, g_ref, alr_ref, wa_ref, ba_ref, gn_ref, s_ref, o_ref, a_ref, kb_ref):
    dk = q_ref.shape[2] // GLA_HEADS
    dv = v_ref.shape[2] // GLA_HEADS
    for r in range(q_ref.shape[0]):
        alr = alr_ref[r].astype(BF16)
        for h in range(GLA_HEADS):
            ks, vs = slice(h * dk, (h + 1) * dk), slice(h * dv, (h + 1) * dv)
            q = q_ref[r, :, ks] * (dk ** -0.5)
            kb = k_ref[r, :, ks].astype(BF16).astype(F32)
            vb = v_ref[r, :, vs].astype(F32)
            x = _dot(alr, wa_ref[:, ks].astype(BF16)) + ba_ref[:, ks]
            a = jnp.exp(_log_sigmoid(x) / GATE_NORM)
            a_ref[r, :, ks] = a
            kb_ref[r, :, ks] = kb
            o = jnp.sum(_column(q * a) * s_ref[r, h], axis=0, keepdims=True)
            o += jnp.sum(q * kb, axis=-1, keepdims=True) * vb
            o_ref[r, :, vs] = _head_out(o, g_ref[r, :, vs].astype(F32), gn_ref[h]).astype(o_ref.dtype)


def _gla_decode(qku, vg, alr, wa, ba, gn, state, tb=4):
    nb = qku.shape[0]
    dkk = wa.shape[1]
    dk, dv = dkk // GLA_HEADS, gn.shape[2]
    dvv = GLA_HEADS * dv
    tok = lambda w, blk=0: pl.BlockSpec((tb, 1, w), lambda b: (b, 0, blk))
    return pl.pallas_call(
        _gla_decode_body,
        grid=(nb // tb,),
        in_specs=[tok(dkk), tok(dkk, 1), tok(dvv), tok(dvv, 1), tok(LANES),
                  pl.BlockSpec((LANES, dkk), lambda b: (0, 0)),
                  pl.BlockSpec((1, dkk), lambda b: (0, 0)),
                  pl.BlockSpec((GLA_HEADS, 1, dv), lambda b: (0, 0, 0)),
                  pl.BlockSpec((tb, GLA_HEADS, dk, dv), lambda b: (b, 0, 0, 0))],
        out_specs=[tok(dvv), tok(dkk), tok(dkk)],
        out_shape=[jax.ShapeDtypeStruct((nb, 1, dvv), BF16),
                   jax.ShapeDtypeStruct((nb, 1, dkk), F32),
                   jax.ShapeDtypeStruct((nb, 1, dkk), F32)],
        compiler_params=_params(1),
        name="gla_decode",
    )(qku, qku, vg, vg, alr, wa, ba, gn, state)


def _tiles(t_len):
    return min(1024, t_len), min(256, t_len), min(512, t_len)


def _layer(xp, xd, mod, lw, dims, nb, t_len, pool_state_t, gla_state):
    (g1, w_in_t, w_alr_t, w_pool, pool_scale, wa, ba, gn, w_out, g2, w_ff1, w_ff3, w_ff2) = lw
    d_pool, d_k, d_v = dims
    n_dec = xd.shape[0]
    tm, tms, tmn = _tiles(t_len)
    tpb, tpbs, tpbn = t_len // tm, t_len // tms, t_len // tmn
    tn = 512

    h1p, ap = _normmod(xp, g1, mod, n_dec, 1, 0, tmn, tpbn, w_alr_t)
    h1d, ad = _normmod(xd, g1, mod, n_dec, 1, 0, n_dec, None, w_alr_t)
    u_tiles, qk_tiles, vg_tiles = d_pool // tn, 2 * d_k // tn, 2 * d_v // tn
    qku_block = lambda j: jnp.where(j < qk_tiles, j + u_tiles, j - qk_tiles)
    qkup, qkud = _proj(h1p, h1d, w_in_t, qk_tiles + u_tiles, qku_block, tm, tn, F32)
    vgp, vgd = _proj(h1p, h1d, w_in_t, vg_tiles, lambda j: j + u_tiles + qk_tiles, tm, tn, BF16)
    ucol = 2 * d_k // d_pool

    op, gla_p, pyp, pool_p = _mixers_prompt(qkup, ucol, vgp, ap, wa, ba, gn, w_pool, pool_scale, nb, t_len)
    pyd, pool_d_t = _pool_decode(qkud, ucol, pool_state_t, w_pool, pool_scale, 32)
    r3 = lambda t: t.reshape(t.shape[0], 1, t.shape[1])
    vgd3 = r3(vgd)
    od, a_d, kb_d = _gla_decode(r3(qkud), vgd3, r3(ad), wa, ba, gn, gla_state)
    od = od.reshape(n_dec, d_v)

    x1p, x1d = _outproj(pyp, op, pyd, od, w_out, xp, xd, mod, 2, tm, tn, tpb)
    h2p = _normmod(x1p, g2, mod, n_dec, 4, 3, tmn, tpbn)
    h2d = _normmod(x1d, g2, mod, n_dec, 4, 3, n_dec, None)
    actp, actd, gla_d = _ffn1(h2p, h2d, w_ff1, w_ff3, tm, 256, a_d, kb_d, vgd3, gla_state)
    x2p, x2d = _ffn2(actp, actd, w_ff2, x1p, x1d, mod, 5, tms, tn, tpbs)
    return x2p, x2d, pool_p, gla_p, pool_d_t, gla_d


def kernel(x_prompt, x_sample, state_pool, state_gla, c_prompt, c_sample, w_ada, b_ada, g_norm1, w_in,
           w_pool, pool_scale, w_a2, b_a, g_gla_out, w_out, g_norm2, w_ff1, w_ff3, w_ff2, g_final):
    bp, tp, d = x_prompt.shape
    bs, ts, _ = x_sample.shape
    depth = w_ada.shape[0]
    assert ts == 1, "the decode path handles one new token per sequence"
    assert bs % SUBLANES == 0 and bp <= SUBLANES
    d_pool = w_pool.shape[1] * w_pool.shape[2]
    d_k = w_a2.shape[2]
    d_v = g_gla_out.shape[1] * g_gla_out.shape[2]
    dims = (d_pool, d_k, d_v)
    d_main = d_pool + 2 * d_k + 2 * d_v

    c_all = jnp.concatenate([c_sample, c_prompt, jnp.zeros((2 * SUBLANES - bp, d), F32)], axis=0)

    hp = x_prompt.reshape(bp * tp, d)
    hd = x_sample.reshape(bs, d)
    pools_p, glas_p, pools_d, glas_d = [], [], [], []
    for l in range(depth):
        mod = _ada(c_all, w_ada[l], b_ada[l].reshape(1, -1))
        w_in_t = w_in[l].T
        w_alr_t = jnp.pad(w_in_t[d_main:], ((0, LANES - GATE_RANK), (0, 0)))
        wa = jnp.pad(w_a2[l], ((0, LANES - GATE_RANK), (0, 0)))
        lw = (g_norm1[l].reshape(1, d), w_in_t, w_alr_t, w_pool[l], pool_scale[l].reshape(1, -1), wa,
              b_a[l].reshape(1, -1), g_gla_out[l].reshape(GLA_HEADS, 1, -1), w_out[l],
              g_norm2[l].reshape(1, d), w_ff1[l], w_ff3[l], w_ff2[l])
        pool_state_t = jnp.transpose(state_pool[l], (1, 0, 2))
        hp, hd, pool_p, gla_p, pool_d_t, gla_d = _layer(hp, hd, mod, lw, dims, bp, tp, pool_state_t,
                                                        state_gla[l])
        pools_p.append(pool_p)
        glas_p.append(gla_p)
        pools_d.append(jnp.transpose(pool_d_t, (1, 0, 2)))
        glas_d.append(gla_d)
    gf = g_final.reshape(1, d)
    y_p = _norm(hp, gf, _tiles(tp)[2]).reshape(bp, tp, d)
    y_d = _norm(hd, gf, bs).reshape(bs, ts, d)
    return (y_p, y_d, jnp.stack(pools_p), jnp.stack(glas_p), jnp.stack(pools_d), jnp.stack(glas_d))
```

```python
import functools
import math

import jax
import jax.numpy as jnp
from jax import lax
from jax.experimental import pallas as pl
from jax.experimental.pallas import tpu as pltpu

F32 = jnp.float32
BF16 = jnp.bfloat16

POOL_WINDOWS = (2, 4, 8, 16)
POOL_STATE = max(POOL_WINDOWS) - 1
GLA_HEADS = 4
GATE_RANK = 16
GATE_NORM = 16.0
N_MOD = 6
EPS = 1e-6
PAST_LEN = 16384

LANES = 128
SUBLANES = 8
VMEM_LIMIT = 56 * 1024 * 1024
GLA_CHUNK = 256
GLA_SUB = 16
GLA_SAFE_DECAY = 60.0

NT = (((1,), (1,)), ((), ()))
TN = (((0,), (0,)), ((), ()))


def _params(n_axes):
    return pltpu.CompilerParams(dimension_semantics=("arbitrary",) * n_axes, vmem_limit_bytes=VMEM_LIMIT)


def _silu(x):
    return x * jax.nn.sigmoid(x)


def _log_sigmoid(x):
    return jnp.minimum(x, 0.0) - jnp.log(1.0 + jnp.exp(-jnp.abs(x)))


def _dot(a, b, dims=None):
    if dims is None:
        return jnp.dot(a, b, preferred_element_type=F32)
    return lax.dot_general(a, b, dims, preferred_element_type=F32)


def _ada_body(c_ref, w_ref, b_ref, o_ref):
    s = _silu(c_ref[...]).astype(BF16)
    o_ref[...] = _dot(s, w_ref[...].astype(BF16)) + b_ref[...]


def _ada(c, w, b, tn=1024):
    m, d = c.shape
    n = w.shape[1]
    return pl.pallas_call(
        _ada_body,
        grid=(n // tn,),
        in_specs=[pl.BlockSpec((m, d), lambda j: (0, 0)),
                  pl.BlockSpec((d, tn), lambda j: (0, j)),
                  pl.BlockSpec((1, tn), lambda j: (0, j))],
        out_specs=pl.BlockSpec((m, tn), lambda j: (0, j)),
        out_shape=jax.ShapeDtypeStruct((m, n), F32),
        compiler_params=_params(1),
        name="ada",
    )(c, w, b)


def _prompt_mod_spec(n_dec, width, col_of):
    return pl.BlockSpec((SUBLANES, width), lambda *g: (n_dec // SUBLANES, col_of(*g)))


def _decode_mod_spec(n_dec, width, col_of):
    return pl.BlockSpec((n_dec, width), lambda *g: (0, col_of(*g)))


def _rms(x, gain):
    return x * lax.rsqrt(jnp.mean(x * x, axis=-1, keepdims=True) + EPS) * gain


def _normmod_body(x_ref, g_ref, sc_ref, sh_ref, *rest, tpb):
    if tpb is None:
        sc, sh = sc_ref[...], sh_ref[...]
    else:
        b = pl.program_id(0) // tpb
        sc, sh = sc_ref[pl.ds(b, 1), :], sh_ref[pl.ds(b, 1), :]
    h = (_rms(x_ref[...], g_ref[...]) * (1.0 + sc) + sh).astype(BF16)
    if len(rest) == 1:
        (o_ref,) = rest
    else:
        wt_ref, o_ref, side_ref = rest
        side_ref[...] = _dot(h, wt_ref[...].astype(BF16), NT)
    o_ref[...] = h


def _normmod(x, gain, mod, n_dec, sc_chunk, sh_chunk, tm, tpb, side_wt=None):
    m, d = x.shape
    spec = _decode_mod_spec if tpb is None else _prompt_mod_spec
    in_specs = [pl.BlockSpec((tm, d), lambda i: (i, 0)),
                pl.BlockSpec((1, d), lambda i: (0, 0)),
                spec(n_dec, d, lambda i: sc_chunk),
                spec(n_dec, d, lambda i: sh_chunk)]
    out_specs = pl.BlockSpec((tm, d), lambda i: (i, 0))
    out_shape = jax.ShapeDtypeStruct((m, d), BF16)
    args = (x, gain, mod, mod)
    if side_wt is not None:
        ns = side_wt.shape[0]
        in_specs.append(pl.BlockSpec((ns, d), lambda i: (0, 0)))
        out_specs = [out_specs, pl.BlockSpec((tm, ns), lambda i: (i, 0))]
        out_shape = [out_shape, jax.ShapeDtypeStruct((m, ns), F32)]
        args += (side_wt,)
    return pl.pallas_call(
        functools.partial(_normmod_body, tpb=tpb),
        grid=(m // tm,),
        in_specs=in_specs,
        out_specs=out_specs,
        out_shape=out_shape,
        compiler_params=_params(1),
        name="normmod",
    )(*args)


def _norm_body(x_ref, g_ref, o_ref):
    o_ref[...] = _rms(x_ref[...], g_ref[...])


def _norm(x, gain, tm):
    m, d = x.shape
    return pl.pallas_call(
        _norm_body,
        grid=(m // tm,),
        in_specs=[pl.BlockSpec((tm, d), lambda i: (i, 0)),
                  pl.BlockSpec((1, d), lambda i: (0, 0))],
        out_specs=pl.BlockSpec((tm, d), lambda i: (i, 0)),
        out_shape=jax.ShapeDtypeStruct((m, d), F32),
        compiler_params=_params(1),
        name="finalnorm",
    )(x, gain)


def _cast_once(w_ref, wbf_ref):
    @pl.when(pl.program_id(1) == 0)
    def _():
        wbf_ref[...] = w_ref[...].astype(BF16)


def _row_steps(prompt_fn, decode_fn):
    i = pl.program_id(1)
    pl.when(i == 0)(decode_fn)
    pl.when(i > 0)(prompt_fn)


def _prow(mt):
    def tile(j, i):
        t = jnp.maximum(i - 1, 0)
        return jnp.where(j % 2 == 0, t, mt - 1 - t)
    return tile


def _prompt_seq(mt, tpb):
    return _prow(mt)(pl.program_id(0), pl.program_id(1)) // tpb


def _proj_body(ap_ref, ad_ref, wt_ref, op_ref, od_ref, wbf_ref):
    _cast_once(wt_ref, wbf_ref)

    def prompt():
        op_ref[...] = _dot(ap_ref[...], wbf_ref[...], NT).astype(op_ref.dtype)

    def decode():
        od_ref[...] = _dot(ad_ref[...], wbf_ref[...], NT).astype(od_ref.dtype)

    _row_steps(prompt, decode)


def _proj(ap, ad, wt, n_tiles, wblock, tm, tn, out_dtype):
    mp, kdim = ap.shape
    md = ad.shape[0]
    mt = mp // tm
    pr = _prow(mt)
    return pl.pallas_call(
        _proj_body,
        grid=(n_tiles, mt + 1),
        in_specs=[pl.BlockSpec((tm, kdim), lambda j, i: (pr(j, i), 0)),
                  pl.BlockSpec((md, kdim), lambda j, i: (0, 0)),
                  pl.BlockSpec((tn, kdim), lambda j, i: (wblock(j), 0))],
        out_specs=[pl.BlockSpec((tm, tn), lambda j, i: (pr(j, i), j)),
                   pl.BlockSpec((md, tn), lambda j, i: (0, j))],
        out_shape=[jax.ShapeDtypeStruct((mp, n_tiles * tn), out_dtype),
                   jax.ShapeDtypeStruct((md, n_tiles * tn), out_dtype)],
        scratch_shapes=[pltpu.VMEM((tn, kdim), BF16)],
        compiler_params=_params(2),
        name="proj",
    )(ap, ad, wt)


def _outproj_body(pyp_ref, op_ref, pyd_ref, od_ref, w_ref, xp_ref, xd_ref, gp_ref, gd_ref,
                  outp_ref, outd_ref, wbf_ref, *, mt, tpb, d_pool):
    _cast_once(w_ref, wbf_ref)
    seq = _prompt_seq(mt, tpb)

    def mix(py_ref, o_ref):
        return _dot(py_ref[...], wbf_ref[:d_pool, :]) + _dot(o_ref[...], wbf_ref[d_pool:, :])

    def prompt():
        outp_ref[...] = xp_ref[...] + gp_ref[pl.ds(seq, 1), :] * mix(pyp_ref, op_ref)

    def decode():
        outd_ref[...] = xd_ref[...] + gd_ref[...] * mix(pyd_ref, od_ref)

    _row_steps(prompt, decode)


def _outproj(pyp, op, pyd, od, w, xp, xd, mod, gt_chunk, tm, tn, tpb):
    mp, d_pool = pyp.shape
    md = pyd.shape[0]
    d_v = op.shape[1]
    kdim, n = w.shape
    mt = mp // tm
    per = n // tn
    pr = _prow(mt)
    gcol = lambda j, i: gt_chunk * per + j
    return pl.pallas_call(
        functools.partial(_outproj_body, mt=mt, tpb=tpb, d_pool=d_pool),
        grid=(per, mt + 1),
        in_specs=[pl.BlockSpec((tm, d_pool), lambda j, i: (pr(j, i), 0)),
                  pl.BlockSpec((tm, d_v), lambda j, i: (pr(j, i), 0)),
                  pl.BlockSpec((md, d_pool), lambda j, i: (0, 0)),
                  pl.BlockSpec((md, d_v), lambda j, i: (0, 0)),
                  pl.BlockSpec((kdim, tn), lambda j, i: (0, j)),
                  pl.BlockSpec((tm, tn), lambda j, i: (pr(j, i), j)),
                  pl.BlockSpec((md, tn), lambda j, i: (0, j)),
                  _prompt_mod_spec(md, tn, gcol),
                  _decode_mod_spec(md, tn, gcol)],
        out_specs=[pl.BlockSpec((tm, tn), lambda j, i: (pr(j, i), j)),
                   pl.BlockSpec((md, tn), lambda j, i: (0, j))],
        out_shape=[jax.ShapeDtypeStruct((mp, n), F32), jax.ShapeDtypeStruct((md, n), F32)],
        scratch_shapes=[pltpu.VMEM((kdim, tn), BF16)],
        compiler_params=_params(2),
        name="outproj",
    )(pyp, op, pyd, od, w, xp, xd, mod, mod)


def _state_block_rows(n_rows, dk, n_steps):
    rb = -(-n_rows // n_steps)
    rb += -rb % LANES
    while n_rows % rb:
        rb += LANES
    assert dk % LANES == 0 and rb <= dk + math.gcd(rb, dk), "a row block may touch at most two (seq, head) pairs"
    return rb


def _state_job_plan(a, kb, vg, state, n_cols, steps_per_col):
    n_seq, n_heads, dk, dv = state.shape
    n_rows = n_seq * n_heads * dk
    rb = _state_block_rows(n_rows, dk, n_cols * steps_per_col)
    n_blocks = n_rows // rb
    block = lambda j, i: jnp.minimum(j * steps_per_col + i, n_blocks - 1)

    def value_row(which):
        def index(j, i):
            p = jnp.minimum(block(j, i) * rb // dk + which, n_seq * n_heads - 1)
            return p // n_heads, 0, p % n_heads
        return pl.BlockSpec((None, 1, dv), index)

    row_vec = pl.BlockSpec((None, 1, rb), lambda j, i: (block(j, i), 0, 0))
    st = pl.BlockSpec((rb, dv), lambda j, i: (block(j, i), 0))
    in_specs = [row_vec, row_vec, value_row(0), value_row(1), st]
    operands = (a.reshape(n_blocks, 1, rb), kb.reshape(n_blocks, 1, rb), vg, vg, state.reshape(n_rows, dv))
    return in_specs, operands, st, jax.ShapeDtypeStruct((n_rows, dv), F32), n_blocks


def _state_job(a_ref, kb_ref, v0_ref, v1_ref, s_ref, s_out_ref, rows_of_first_pair):
    in_first = lax.broadcasted_iota(jnp.int32, (s_ref.shape[0], 1), 0) < rows_of_first_pair
    v = jnp.where(in_first, v0_ref[...].astype(F32), v1_ref[...].astype(F32))
    s_out_ref[...] = _column(a_ref[...]) * s_ref[...] + _column(kb_ref[...]) * v


def _ffn1_body(hp_ref, hd_ref, w1_ref, w3_ref, op_ref, od_ref, w1bf_ref, w3bf_ref):
    _cast_once(w1_ref, w1bf_ref)
    _cast_once(w3_ref, w3bf_ref)

    def swiglu(h_ref, o_ref):
        h = h_ref[...]
        o_ref[...] = (_silu(_dot(h, w1bf_ref[...])) * _dot(h, w3bf_ref[...])).astype(o_ref.dtype)

    _row_steps(functools.partial(swiglu, hp_ref, op_ref), functools.partial(swiglu, hd_ref, od_ref))


def _ffn1(hp, hd, w1, w3, tm, tn):
    mp, kdim = hp.shape
    md = hd.shape[0]
    n = w1.shape[1]
    mt = mp // tm
    pr = _prow(mt)
    wspec = pl.BlockSpec((kdim, tn), lambda j, i: (0, j))
    return pl.pallas_call(
        _ffn1_body,
        grid=(n // tn, mt + 1),
        in_specs=[pl.BlockSpec((tm, kdim), lambda j, i: (pr(j, i), 0)),
                  pl.BlockSpec((md, kdim), lambda j, i: (0, 0)),
                  wspec, wspec],
        out_specs=[pl.BlockSpec((tm, tn), lambda j, i: (pr(j, i), j)),
                   pl.BlockSpec((md, tn), lambda j, i: (0, j))],
        out_shape=[jax.ShapeDtypeStruct((mp, n), BF16), jax.ShapeDtypeStruct((md, n), BF16)],
        scratch_shapes=[pltpu.VMEM((kdim, tn), BF16), pltpu.VMEM((kdim, tn), BF16)],
        compiler_params=_params(2),
        name="ffn1",
    )(hp, hd, w1, w3)


FFN2_WEIGHT_CHUNKS = 8


def _ffn2_body(ap_ref, ad_ref, w_hbm, xp_ref, xd_ref, gp_ref, gd_ref, a_ref, kb_ref, v0_ref, v1_ref, s_ref,
               outp_ref, outd_ref, s_out_ref, wbf_ref, stage_ref, sem, *, mt, tpb, dk, n_blocks):
    j, i = pl.program_id(0), pl.program_id(1)
    blk = jnp.minimum(j * pl.num_programs(1) + i, n_blocks - 1)
    rows_of_first_pair = dk - (blk * s_ref.shape[0]) % dk
    state_job = functools.partial(_state_job, a_ref, kb_ref, v0_ref, v1_ref, s_ref, s_out_ref,
                                  rows_of_first_pair)
    kc, tn = stage_ref.shape[1], stage_ref.shape[2]
    n_chunks = wbf_ref.shape[1] // kc
    slot = j % 2

    def chunk(col_tile, c):
        return pltpu.make_async_copy(w_hbm.at[pl.ds(c * kc, kc), pl.ds(col_tile * tn, tn)],
                                     stage_ref.at[c % 2], sem.at[c % 2])

    def land(dst_slot, c):
        wbf_ref[dst_slot, pl.ds(c * kc, kc), :] = stage_ref[c % 2].astype(BF16)

    @pl.when((j == 0) & (i == 0))
    def _():
        for c in range(n_chunks):
            chunk(0, c).start()
            chunk(0, c).wait()
            land(0, c)

    @pl.when(j + 1 < pl.num_programs(0))
    def _():
        @pl.when(i == 0)
        def _():
            chunk(j + 1, 0).start()

        for c in range(n_chunks):
            @pl.when(i == c + 1)
            def _(c=c):
                chunk(j + 1, c).wait()
                if c + 1 < n_chunks:
                    chunk(j + 1, c + 1).start()
                land(1 - slot, c)

    seq = _prompt_seq(mt, tpb)

    def prompt():
        outp_ref[...] = xp_ref[...] + gp_ref[pl.ds(seq, 1), :] * _dot(ap_ref[...], wbf_ref[slot])
        state_job()

    def decode():
        outd_ref[...] = xd_ref[...] + gd_ref[...] * _dot(ad_ref[...], wbf_ref[slot])
        state_job()

    _row_steps(prompt, decode)


def _ffn2(ap, ad, w, xp, xd, mod, gt_chunk, tm, tn, tpb, a, kb, vg, state):
    mp, kdim = ap.shape
    md = ad.shape[0]
    n = w.shape[1]
    mt = mp // tm
    per = n // tn
    pr = _prow(mt)
    kc = kdim // FFN2_WEIGHT_CHUNKS
    assert kc * FFN2_WEIGHT_CHUNKS == kdim and kc % 16 == 0 and mt >= FFN2_WEIGHT_CHUNKS
    gcol = lambda j, i: gt_chunk * per + j
    job_in, job_operands, job_out, job_shape, n_blocks = _state_job_plan(a, kb, vg, state, per, mt + 1)
    x_p, x_d, state_new = pl.pallas_call(
        functools.partial(_ffn2_body, mt=mt, tpb=tpb, dk=state.shape[2], n_blocks=n_blocks),
        grid=(per, mt + 1),
        in_specs=[pl.BlockSpec((tm, kdim), lambda j, i: (pr(j, i), 0)),
                  pl.BlockSpec((md, kdim), lambda j, i: (0, 0), pipeline_mode=pl.Buffered(1)),
                  pl.BlockSpec(memory_space=pl.ANY),
                  pl.BlockSpec((tm, tn), lambda j, i: (pr(j, i), j)),
                  pl.BlockSpec((md, tn), lambda j, i: (0, j)),
                  _prompt_mod_spec(md, tn, gcol),
                  _decode_mod_spec(md, tn, gcol),
                  *job_in],
        out_specs=[pl.BlockSpec((tm, tn), lambda j, i: (pr(j, i), j)),
                   pl.BlockSpec((md, tn), lambda j, i: (0, j)),
                   job_out],
        out_shape=[jax.ShapeDtypeStruct((mp, n), F32), jax.ShapeDtypeStruct((md, n), F32), job_shape],
        scratch_shapes=[pltpu.VMEM((2, kdim, tn), BF16), pltpu.VMEM((2, kc, tn), F32),
                        pltpu.SemaphoreType.DMA((2,))],
        compiler_params=_params(2),
        name="ffn2",
    )(ap, ad, w, xp, xd, mod, mod, *job_operands)
    return x_p, x_d, state_new.reshape(state.shape)


def _pool_group_matmul(d, wp_ref, ps_ref, n_groups, gw):
    outs = []
    for gi in range(n_groups):
        dg = d[:, gi * gw:(gi + 1) * gw].astype(BF16)
        outs.append(_dot(dg, wp_ref[gi].astype(BF16)))
    return jnp.concatenate(outs, axis=-1) * ps_ref[...]


def _pool_prompt_tile(t, n_tiles, u_ref, wp_ref, ps_ref, y_ref, st_ref, buf_ref):
    tm = u_ref.shape[0]
    halo = POOL_STATE + 1
    n_groups = len(POOL_WINDOWS)
    gw = u_ref.shape[1] // n_groups

    @pl.when(t == 0)
    def _():
        buf_ref[:halo, :] = jnp.zeros((halo, buf_ref.shape[1]), F32)

    u = u_ref[...]
    buf_ref[halo:, :] = u
    pos = t * tm + lax.broadcasted_iota(jnp.int32, (tm, 1), 0)
    parts = []
    for gi, w in enumerate(POOL_WINDOWS):
        lo, hi = gi * gw, (gi + 1) * gw
        win = u[:, lo:hi]
        for s in range(1, w):
            win = win + buf_ref[halo - s:halo - s + tm, lo:hi]
        cnt = jnp.minimum(w, pos + 1).astype(F32)
        parts.append(win / cnt - u[:, lo:hi])
    d = jnp.concatenate(parts, axis=-1)
    y_ref[...] = _pool_group_matmul(d, wp_ref, ps_ref, n_groups, gw).astype(y_ref.dtype)

    @pl.when(t == n_tiles - 1)
    def _():
        st_ref[...] = buf_ref[halo + tm - POOL_STATE:, :]

    buf_ref[:halo, :] = buf_ref[tm:, :]


def _pool_decode_body(u_ref, sp_ref, wp_ref, ps_ref, y_ref, st_ref):
    n_groups = len(POOL_WINDOWS)
    gw = u_ref.shape[1] // n_groups
    u = u_ref[...]
    parts = []
    for gi, w in enumerate(POOL_WINDOWS):
        lo, hi = gi * gw, (gi + 1) * gw
        win = u[:, lo:hi]
        for s in range(1, w):
            win = win + sp_ref[POOL_STATE - s, :, lo:hi]
        cnt = float(min(w, PAST_LEN + 1))
        parts.append(win / cnt - u[:, lo:hi])
    d = jnp.concatenate(parts, axis=-1)
    y_ref[...] = _pool_group_matmul(d, wp_ref, ps_ref, n_groups, gw).astype(y_ref.dtype)
    for r in range(POOL_STATE - 1):
        st_ref[r] = sp_ref[r + 1]
    st_ref[POOL_STATE - 1] = u


def _pool_decode(u, ucol, state_t, w_pool, pool_scale, tb):
    nb = u.shape[0]
    g, gw, _ = w_pool.shape
    dp = g * gw
    return pl.pallas_call(
        _pool_decode_body,
        grid=(nb // tb,),
        in_specs=[pl.BlockSpec((tb, dp), lambda b: (b, ucol)),
                  pl.BlockSpec((POOL_STATE, tb, dp), lambda b: (0, b, 0)),
                  pl.BlockSpec((g, gw, gw), lambda b: (0, 0, 0)),
                  pl.BlockSpec((1, dp), lambda b: (0, 0))],
        out_specs=[pl.BlockSpec((tb, dp), lambda b: (b, 0)),
                   pl.BlockSpec((POOL_STATE, tb, dp), lambda b: (0, b, 0))],
        out_shape=[jax.ShapeDtypeStruct((nb, dp), BF16),
                   jax.ShapeDtypeStruct((POOL_STATE, nb, dp), F32)],
        compiler_params=_params(1),
        name="pool_decode",
    )(u, state_t, w_pool, pool_scale)


def _head_out(o, g, gain):
    return _silu(g) * _rms(o, gain)


def _column(row_vec):
    return jnp.broadcast_to(row_vec, (LANES, row_vec.shape[1])).T[:, :1]


def _cumsum_rows(x):
    n = x.shape[0]
    tri = (lax.broadcasted_iota(jnp.int32, (n, n), 0) >= lax.broadcasted_iota(jnp.int32, (n, n), 1))
    tri = jnp.where(tri, 1.0, 0.0).astype(BF16)
    hi = x.astype(BF16)
    r1 = x - hi.astype(F32)
    mid = r1.astype(BF16)
    lo = (r1 - mid.astype(F32)).astype(BF16)
    return _dot(tri, hi) + _dot(tri, mid) + _dot(tri, lo)


def _exact_scores(b_scr, q_scr, k_scr, a_scr):
    rows = b_scr.shape[0]

    def block(i, carry):
        r0 = pl.multiple_of(i * GLA_SUB, GLA_SUB)
        b_i = b_scr[pl.ds(r0, GLA_SUB), :]
        q_i = q_scr[pl.ds(r0, GLA_SUB), :]
        b_first = b_scr[pl.ds(r0, 1), :]
        qt = (q_i * jnp.exp(b_i - b_first)).astype(BF16)
        kt = (k_scr[...] * jnp.exp(jnp.minimum(b_first - b_scr[...], 0.0))).astype(BF16)
        panel = _dot(qt, kt, NT)
        col = lax.broadcasted_iota(jnp.int32, (GLA_SUB, rows), 1)
        row = lax.broadcasted_iota(jnp.int32, (GLA_SUB, rows), 0) + r0
        diag = jnp.zeros((GLA_SUB, rows), F32)
        for j in range(GLA_SUB):
            b_j = b_scr[pl.ds(r0 + j, 1), :]
            k_j = k_scr[pl.ds(r0 + j, 1), :]
            p = q_i * k_j * jnp.exp(jnp.minimum(b_i - b_j, 0.0))
            diag = jnp.where(col == r0 + j, jnp.sum(p, axis=-1, keepdims=True), diag)
        a_scr[pl.ds(r0, GLA_SUB), :] = jnp.where(col < r0, panel, jnp.where(col <= row, diag, 0.0))
        return carry

    lax.fori_loop(0, rows // GLA_SUB, block, 0)


def _gla_prompt_body(q_ref, k_ref, v_ref, g_ref, alr_ref, wa_ref, ba_ref, gn_ref, u_ref, wp_ref, ps_ref,
                     o_ref, s_ref, py_ref, pst_ref,
                     b_all, qe_all, b_scr, q_scr, k_scr, a_scr, pbuf_ref, *, n_chunks):
    _pool_prompt_tile(pl.program_id(1), n_chunks, u_ref, wp_ref, ps_ref, py_ref, pst_ref, pbuf_ref)
    rows = q_ref.shape[0]
    dk = q_ref.shape[1] // GLA_HEADS
    dv = v_ref.shape[1] // GLA_HEADS
    scale = dk ** -0.5
    heads = [(h, slice(h * dk, (h + 1) * dk), slice(h * dv, (h + 1) * dv)) for h in range(GLA_HEADS)]

    @pl.when(pl.program_id(1) == 0)
    def _():
        s_ref[...] = jnp.zeros(s_ref.shape, F32)

    x = _dot(alr_ref[...].astype(BF16), wa_ref[...].astype(BF16)) + ba_ref[...]
    b_all[...] = _cumsum_rows(_log_sigmoid(x) / GATE_NORM)
    qe_all[...] = (q_ref[...] * scale * jnp.exp(b_all[...])).astype(BF16)

    mild = jnp.max(-b_all[rows - 1:rows, :]) < GLA_SAFE_DECAY

    @pl.when(mild)
    def _():
        causal = (lax.broadcasted_iota(jnp.int32, (rows, rows), 1)
                  <= lax.broadcasted_iota(jnp.int32, (rows, rows), 0))
        for h, ks, _ in heads:
            kinv = (k_ref[:, ks] * jnp.exp(-b_all[:, ks])).astype(BF16)
            a_scr[h] = jnp.where(causal, _dot(qe_all[:, ks], kinv, NT), 0.0)

    @pl.when(jnp.logical_not(mild))
    def _():
        for h, ks, _ in heads:
            b_scr[...] = b_all[:, ks]
            q_scr[...] = q_ref[:, ks] * scale
            k_scr[...] = k_ref[:, ks]
            _exact_scores(b_scr, q_scr, k_scr, a_scr.at[h])

    for h, ks, vs in heads:
        b = b_all[:, ks]
        b_last = b_all[rows - 1:rows, ks]
        vb = v_ref[:, vs]
        s_old = s_ref[h]
        o = _dot(qe_all[:, ks], s_old.astype(BF16)) + _dot(a_scr[h].astype(BF16), vb)
        o_ref[:, vs] = _head_out(o, g_ref[:, vs].astype(F32), gn_ref[h]).astype(o_ref.dtype)
        kd = (k_ref[:, ks] * jnp.exp(b_last - b)).astype(BF16)
        s_ref[h] = _column(jnp.exp(b_last)) * s_old + _dot(kd, vb, TN)


def _mixers_prompt(qku, ucol, vg, alr, wa, ba, gn, w_pool, pool_scale, nb, t_len):
    dkk = wa.shape[1]
    dk, dv = dkk // GLA_HEADS, gn.shape[2]
    dvv = GLA_HEADS * dv
    g, gw, _ = w_pool.shape
    dp = g * gw
    c = GLA_CHUNK
    n_chunks = t_len // c
    tok = lambda w, blk=0: pl.BlockSpec((c, w), lambda b, i: (b * n_chunks + i, blk))
    return pl.pallas_call(
        functools.partial(_gla_prompt_body, n_chunks=n_chunks),
        grid=(nb, n_chunks),
        in_specs=[tok(dkk), tok(dkk, 1), tok(dvv), tok(dvv, 1), tok(LANES),
                  pl.BlockSpec((LANES, dkk), lambda b, i: (0, 0)),
                  pl.BlockSpec((1, dkk), lambda b, i: (0, 0)),
                  pl.BlockSpec((GLA_HEADS, 1, dv), lambda b, i: (0, 0, 0)),
                  tok(dp, ucol),
                  pl.BlockSpec((g, gw, gw), lambda b, i: (0, 0, 0)),
                  pl.BlockSpec((1, dp), lambda b, i: (0, 0))],
        out_specs=[tok(dvv),
                   pl.BlockSpec((None, GLA_HEADS, dk, dv), lambda b, i: (b, 0, 0, 0)),
                   tok(dp),
                   pl.BlockSpec((None, POOL_STATE, dp), lambda b, i: (b, 0, 0))],
        out_shape=[jax.ShapeDtypeStruct((nb * t_len, dvv), BF16),
                   jax.ShapeDtypeStruct((nb, GLA_HEADS, dk, dv), F32),
                   jax.ShapeDtypeStruct((nb * t_len, dp), BF16),
                   jax.ShapeDtypeStruct((nb, POOL_STATE, dp), F32)],
        scratch_shapes=[pltpu.VMEM((c, dkk), F32), pltpu.VMEM((c, dkk), BF16),
                        pltpu.VMEM((c, dk), F32), pltpu.VMEM((c, dk), F32), pltpu.VMEM((c, dk), F32),
                        pltpu.VMEM((GLA_HEADS, c, c), F32),
                        pltpu.VMEM((POOL_STATE + 1 + c, dp), F32)],
        compiler_params=_params(2),
        name="mixers_prompt",
    )(qku, qku, vg, vg, alr, wa, ba, gn, qku, w_pool, pool_scale)


def _gla_decode_body(q_ref, k_ref, v_ref, g_ref, alr_ref, wa_ref, ba_ref, gn_ref, s_ref, o_ref, a_ref, kb_ref):
    dk = q_ref.shape[2] // GLA_HEADS
    dv = v_ref.shape[2] // GLA_HEADS
    for r in range(q_ref.shape[0]):
        alr = alr_ref[r].astype(BF16)
        for h in range(GLA_HEADS):
            ks, vs = slice(h * dk, (h + 1) * dk), slice(h * dv, (h + 1) * dv)
            q = q_ref[r, :, ks] * (dk ** -0.5)
            kb = k_ref[r, :, ks].astype(BF16).astype(F32)
            vb = v_ref[r, :, vs].astype(F32)
            x = _dot(alr, wa_ref[:, ks].astype(BF16)) + ba_ref[:, ks]
            a = jnp.exp(_log_sigmoid(x) / GATE_NORM)
            a_ref[r, :, ks] = a
            kb_ref[r, :, ks] = kb
            o = jnp.sum(_column(q * a) * s_ref[r, h], axis=0, keepdims=True)
            o += jnp.sum(q * kb, axis=-1, keepdims=True) * vb
            o_ref[r, :, vs] = _head_out(o, g_ref[r, :, vs].astype(F32), gn_ref[h]).astype(o_ref.dtype)


def _gla_decode(qku, vg, alr, wa, ba, gn, state, tb=4):
    nb = qku.shape[0]
    dkk = wa.shape[1]
    dk, dv = dkk // GLA_HEADS, gn.shape[2]
    dvv = GLA_HEADS * dv
    tok = lambda w, blk=0: pl.BlockSpec((tb, 1, w), lambda b: (b, 0, blk))
    return pl.pallas_call(
        _gla_decode_body,
        grid=(nb // tb,),
        in_specs=[tok(dkk), tok(dkk, 1), tok(dvv), tok(dvv, 1), tok(LANES),
                  pl.BlockSpec((LANES, dkk), lambda b: (0, 0)),
                  pl.BlockSpec((1, dkk), lambda b: (0, 0)),
                  pl.BlockSpec((GLA_HEADS, 1, dv), lambda b: (0, 0, 0)),
                  pl.BlockSpec((tb, GLA_HEADS, dk, dv), lambda b: (b, 0, 0, 0))],
        out_specs=[tok(dvv), tok(dkk), tok(dkk)],
        out_shape=[jax.ShapeDtypeStruct((nb, 1, dvv), BF16),
                   jax.ShapeDtypeStruct((nb, 1, dkk), F32),
                   jax.ShapeDtypeStruct((nb, 1, dkk), F32)],
        compiler_params=_params(1),
        name="gla_decode",
    )(qku, qku, vg, vg, alr, wa, ba, gn, state)


def _tiles(t_len):
    return min(1024, t_len), min(256, t_len), min(512, t_len)


def _layer(xp, xd, mod, lw, dims, nb, t_len, pool_state_t, gla_state):
    (g1, w_in_t, w_alr_t, w_pool, pool_scale, wa, ba, gn, w_out, g2, w_ff1, w_ff3, w_ff2) = lw
    d_pool, d_k, d_v = dims
    n_dec = xd.shape[0]
    tm, tms, tmn = _tiles(t_len)
    tpb, tpbs, tpbn = t_len // tm, t_len // tms, t_len // tmn
    tn = 512

    h1p, ap = _normmod(xp, g1, mod, n_dec, 1, 0, tmn, tpbn, w_alr_t)
    h1d, ad = _normmod(xd, g1, mod, n_dec, 1, 0, n_dec, None, w_alr_t)
    u_tiles, qk_tiles, vg_tiles = d_pool // tn, 2 * d_k // tn, 2 * d_v // tn
    qku_block = lambda j: jnp.where(j < qk_tiles, j + u_tiles, j - qk_tiles)
    qkup, qkud = _proj(h1p, h1d, w_in_t, qk_tiles + u_tiles, qku_block, tm, tn, F32)
    vgp, vgd = _proj(h1p, h1d, w_in_t, vg_tiles, lambda j: j + u_tiles + qk_tiles, tm, tn, BF16)
    ucol = 2 * d_k // d_pool

    op, gla_p, pyp, pool_p = _mixers_prompt(qkup, ucol, vgp, ap, wa, ba, gn, w_pool, pool_scale, nb, t_len)
    pyd, pool_d_t = _pool_decode(qkud, ucol, pool_state_t, w_pool, pool_scale, 32)
    r3 = lambda t: t.reshape(t.shape[0], 1, t.shape[1])
    vgd3 = r3(vgd)
    od, a_d, kb_d = _gla_decode(r3(qkud), vgd3, r3(ad), wa, ba, gn, gla_state)
    od = od.reshape(n_dec, d_v)

    x1p, x1d = _outproj(pyp, op, pyd, od, w_out, xp, xd, mod, 2, tm, tn, tpb)
    h2p = _normmod(x1p, g2, mod, n_dec, 4, 3, tmn, tpbn)
    h2d = _normmod(x1d, g2, mod, n_dec, 4, 3, n_dec, None)
    actp, actd = _ffn1(h2p, h2d, w_ff1, w_ff3, tm, 256)
    x2p, x2d, gla_d = _ffn2(actp, actd, w_ff2, x1p, x1d, mod, 5, tms, tn, tpbs, a_d, kb_d, vgd3, gla_state)
    return x2p, x2d, pool_p, gla_p, pool_d_t, gla_d


def kernel(x_prompt, x_sample, state_pool, state_gla, c_prompt, c_sample, w_ada, b_ada, g_norm1, w_in,
           w_pool, pool_scale, w_a2, b_a, g_gla_out, w_out, g_norm2, w_ff1, w_ff3, w_ff2, g_final):
    bp, tp, d = x_prompt.shape
    bs, ts, _ = x_sample.shape
    depth = w_ada.shape[0]
    assert ts == 1, "the decode path handles one new token per sequence"
    assert bs % SUBLANES == 0 and bp <= SUBLANES
    d_pool = w_pool.shape[1] * w_pool.shape[2]
    d_k = w_a2.shape[2]
    d_v = g_gla_out.shape[1] * g_gla_out.shape[2]
    dims = (d_pool, d_k, d_v)
    d_main = d_pool + 2 * d_k + 2 * d_v

    c_all = jnp.concatenate([c_sample, c_prompt, jnp.zeros((2 * SUBLANES - bp, d), F32)], axis=0)

    hp = x_prompt.reshape(bp * tp, d)
    hd = x_sample.reshape(bs, d)
    pools_p, glas_p, pools_d, glas_d = [], [], [], []
    for l in range(depth):
        mod = _ada(c_all, w_ada[l], b_ada[l].reshape(1, -1))
        w_in_t = w_in[l].T
        w_alr_t = jnp.pad(w_in_t[d_main:], ((0, LANES - GATE_RANK), (0, 0)))
        wa = jnp.pad(w_a2[l], ((0, LANES - GATE_RANK), (0, 0)))
        lw = (g_norm1[l].reshape(1, d), w_in_t, w_alr_t, w_pool[l], pool_scale[l].reshape(1, -1), wa,
              b_a[l].reshape(1, -1), g_gla_out[l].reshape(GLA_HEADS, 1, -1), w_out[l],
              g_norm2[l].reshape(1, d), w_ff1[l], w_ff3[l], w_ff2[l])
        pool_state_t = jnp.transpose(state_pool[l], (1, 0, 2))
        hp, hd, pool_p, gla_p, pool_d_t, gla_d = _layer(hp, hd, mod, lw, dims, bp, tp, pool_state_t,
                                                        state_gla[l])
        pools_p.append(pool_p)
        glas_p.append(gla_p)
        pools_d.append(jnp.transpose(pool_d_t, (1, 0, 2)))
        glas_d.append(gla_d)
    gf = g_final.reshape(1, d)
    y_p = _norm(hp, gf, _tiles(tp)[2]).reshape(bp, tp, d)
    y_d = _norm(hd, gf, bs).reshape(bs, ts, d)
    return (y_p, y_d, jnp.stack(pools_p), jnp.stack(glas_p), jnp.stack(pools_d), jnp.stack(glas_d))
```

```python
import functools

import jax
import jax.numpy as jnp
from jax import lax
from jax.experimental import pallas as pl
from jax.experimental.pallas import tpu as pltpu

F32 = jnp.float32
BF16 = jnp.bfloat16

POOL_WINDOWS = (2, 4, 8, 16)
POOL_STATE = max(POOL_WINDOWS) - 1
GLA_HEADS = 4
GATE_RANK = 16
GATE_NORM = 16.0
N_MOD = 6
EPS = 1e-6
PAST_LEN = 16384

LANES = 128
SUBLANES = 8
VMEM_LIMIT = 56 * 1024 * 1024
GLA_CHUNK = 256
GLA_SUB = 16
GLA_SAFE_DECAY = 60.0

NT = (((1,), (1,)), ((), ()))
TN = (((0,), (0,)), ((), ()))


def _params(n_axes, vmem_limit=VMEM_LIMIT):
    return pltpu.CompilerParams(dimension_semantics=("arbitrary",) * n_axes, vmem_limit_bytes=vmem_limit)


def _silu(x):
    return x * jax.nn.sigmoid(x)


def _log_sigmoid(x):
    return jnp.minimum(x, 0.0) - jnp.log(1.0 + jnp.exp(-jnp.abs(x)))


def _dot(a, b, dims=None):
    if dims is None:
        return jnp.dot(a, b, preferred_element_type=F32)
    return lax.dot_general(a, b, dims, preferred_element_type=F32)


def _ada_body(c_ref, w_ref, b_ref, o_ref):
    s = _silu(c_ref[...]).astype(BF16)
    o_ref[...] = _dot(s, w_ref[...].astype(BF16)) + b_ref[...]


def _ada(c, w, b, tn=1024):
    m, d = c.shape
    n = w.shape[1]
    return pl.pallas_call(
        _ada_body,
        grid=(n // tn,),
        in_specs=[pl.BlockSpec((m, d), lambda j: (0, 0)),
                  pl.BlockSpec((d, tn), lambda j: (0, j)),
                  pl.BlockSpec((1, tn), lambda j: (0, j))],
        out_specs=pl.BlockSpec((m, tn), lambda j: (0, j)),
        out_shape=jax.ShapeDtypeStruct((m, n), F32),
        compiler_params=_params(1),
        name="ada",
    )(c, w, b)


def _prompt_mod_spec(n_dec, width, col_of):
    return pl.BlockSpec((SUBLANES, width), lambda *g: (n_dec // SUBLANES, col_of(*g)))


def _decode_mod_spec(n_dec, width, col_of):
    return pl.BlockSpec((n_dec, width), lambda *g: (0, col_of(*g)))


def _rms(x, gain):
    return x * lax.rsqrt(jnp.mean(x * x, axis=-1, keepdims=True) + EPS) * gain


def _normmod_body(x_ref, g_ref, sc_ref, sh_ref, *rest, tpb):
    if tpb is None:
        sc, sh = sc_ref[...], sh_ref[...]
    else:
        b = pl.program_id(0) // tpb
        sc, sh = sc_ref[pl.ds(b, 1), :], sh_ref[pl.ds(b, 1), :]
    h = (_rms(x_ref[...], g_ref[...]) * (1.0 + sc) + sh).astype(BF16)
    if len(rest) == 1:
        (o_ref,) = rest
    else:
        wt_ref, o_ref, side_ref = rest
        side_ref[...] = _dot(h, wt_ref[...].astype(BF16), NT)
    o_ref[...] = h


def _normmod(x, gain, mod, n_dec, sc_chunk, sh_chunk, tm, tpb, side_wt=None):
    m, d = x.shape
    spec = _decode_mod_spec if tpb is None else _prompt_mod_spec
    in_specs = [pl.BlockSpec((tm, d), lambda i: (i, 0)),
                pl.BlockSpec((1, d), lambda i: (0, 0)),
                spec(n_dec, d, lambda i: sc_chunk),
                spec(n_dec, d, lambda i: sh_chunk)]
    out_specs = pl.BlockSpec((tm, d), lambda i: (i, 0))
    out_shape = jax.ShapeDtypeStruct((m, d), BF16)
    args = (x, gain, mod, mod)
    if side_wt is not None:
        ns = side_wt.shape[0]
        in_specs.append(pl.BlockSpec((ns, d), lambda i: (0, 0)))
        out_specs = [out_specs, pl.BlockSpec((tm, ns), lambda i: (i, 0))]
        out_shape = [out_shape, jax.ShapeDtypeStruct((m, ns), F32)]
        args += (side_wt,)
    return pl.pallas_call(
        functools.partial(_normmod_body, tpb=tpb),
        grid=(m // tm,),
        in_specs=in_specs,
        out_specs=out_specs,
        out_shape=out_shape,
        compiler_params=_params(1),
        name="normmod",
    )(*args)


def _norm_body(x_ref, g_ref, o_ref):
    o_ref[...] = _rms(x_ref[...], g_ref[...])


def _norm(x, gain, tm):
    m, d = x.shape
    return pl.pallas_call(
        _norm_body,
        grid=(m // tm,),
        in_specs=[pl.BlockSpec((tm, d), lambda i: (i, 0)),
                  pl.BlockSpec((1, d), lambda i: (0, 0))],
        out_specs=pl.BlockSpec((tm, d), lambda i: (i, 0)),
        out_shape=jax.ShapeDtypeStruct((m, d), F32),
        compiler_params=_params(1),
        name="finalnorm",
    )(x, gain)


def _cast_once(w_ref, wbf_ref):
    @pl.when(pl.program_id(1) == 0)
    def _():
        wbf_ref[...] = w_ref[...].astype(BF16)


def _row_steps(prompt_fn, decode_fn):
    i = pl.program_id(1)
    pl.when(i == 0)(decode_fn)
    pl.when(i > 0)(prompt_fn)


def _prow(mt):
    def tile(j, i):
        t = jnp.maximum(i - 1, 0)
        return jnp.where(j % 2 == 0, t, mt - 1 - t)
    return tile


def _prompt_seq(mt, tpb):
    return _prow(mt)(pl.program_id(0), pl.program_id(1)) // tpb


def _proj_body(ap_ref, ad_ref, wt_ref, op_ref, od_ref, wbf_ref):
    _cast_once(wt_ref, wbf_ref)

    def prompt():
        op_ref[...] = _dot(ap_ref[...], wbf_ref[...], NT).astype(op_ref.dtype)

    def decode():
        od_ref[...] = _dot(ad_ref[...], wbf_ref[...], NT).astype(od_ref.dtype)

    _row_steps(prompt, decode)


def _proj(ap, ad, wt, n_tiles, wblock, tm, tn, out_dtype):
    mp, kdim = ap.shape
    md = ad.shape[0]
    mt = mp // tm
    pr = _prow(mt)
    return pl.pallas_call(
        _proj_body,
        grid=(n_tiles, mt + 1),
        in_specs=[pl.BlockSpec((tm, kdim), lambda j, i: (pr(j, i), 0)),
                  pl.BlockSpec((md, kdim), lambda j, i: (0, 0)),
                  pl.BlockSpec((tn, kdim), lambda j, i: (wblock(j), 0))],
        out_specs=[pl.BlockSpec((tm, tn), lambda j, i: (pr(j, i), j)),
                   pl.BlockSpec((md, tn), lambda j, i: (0, j))],
        out_shape=[jax.ShapeDtypeStruct((mp, n_tiles * tn), out_dtype),
                   jax.ShapeDtypeStruct((md, n_tiles * tn), out_dtype)],
        scratch_shapes=[pltpu.VMEM((tn, kdim), BF16)],
        compiler_params=_params(2),
        name="proj",
    )(ap, ad, wt)


def _outproj_body(pyp_ref, op_ref, pyd_ref, od_ref, w_ref, xp_ref, xd_ref, gp_ref, gd_ref,
                  outp_ref, outd_ref, wbf_ref, *, mt, tpb, d_pool):
    _cast_once(w_ref, wbf_ref)
    seq = _prompt_seq(mt, tpb)

    def mix(py_ref, o_ref):
        return _dot(py_ref[...], wbf_ref[:d_pool, :]) + _dot(o_ref[...], wbf_ref[d_pool:, :])

    def prompt():
        outp_ref[...] = xp_ref[...] + gp_ref[pl.ds(seq, 1), :] * mix(pyp_ref, op_ref)

    def decode():
        outd_ref[...] = xd_ref[...] + gd_ref[...] * mix(pyd_ref, od_ref)

    _row_steps(prompt, decode)


def _outproj(pyp, op, pyd, od, w, xp, xd, mod, gt_chunk, tm, tn, tpb):
    mp, d_pool = pyp.shape
    md = pyd.shape[0]
    d_v = op.shape[1]
    kdim, n = w.shape
    mt = mp // tm
    per = n // tn
    pr = _prow(mt)
    gcol = lambda j, i: gt_chunk * per + j
    return pl.pallas_call(
        functools.partial(_outproj_body, mt=mt, tpb=tpb, d_pool=d_pool),
        grid=(per, mt + 1),
        in_specs=[pl.BlockSpec((tm, d_pool), lambda j, i: (pr(j, i), 0)),
                  pl.BlockSpec((tm, d_v), lambda j, i: (pr(j, i), 0)),
                  pl.BlockSpec((md, d_pool), lambda j, i: (0, 0)),
                  pl.BlockSpec((md, d_v), lambda j, i: (0, 0)),
                  pl.BlockSpec((kdim, tn), lambda j, i: (0, j)),
                  pl.BlockSpec((tm, tn), lambda j, i: (pr(j, i), j)),
                  pl.BlockSpec((md, tn), lambda j, i: (0, j)),
                  _prompt_mod_spec(md, tn, gcol),
                  _decode_mod_spec(md, tn, gcol)],
        out_specs=[pl.BlockSpec((tm, tn), lambda j, i: (pr(j, i), j)),
                   pl.BlockSpec((md, tn), lambda j, i: (0, j))],
        out_shape=[jax.ShapeDtypeStruct((mp, n), F32), jax.ShapeDtypeStruct((md, n), F32)],
        scratch_shapes=[pltpu.VMEM((kdim, tn), BF16)],
        compiler_params=_params(2),
        name="outproj",
    )(pyp, op, pyd, od, w, xp, xd, mod, mod)


def _state_block_rows(n_rows, dk, n_steps):
    rb = -(-n_rows // n_steps)
    rb += -rb % LANES
    while n_rows % rb:
        rb += LANES
    assert dk % LANES == 0 and rb <= dk + LANES, "a row block may touch at most two (sequence, head) pairs"
    return rb


def _state_job(a_ref, kb_ref, v0_ref, v1_ref, s_ref, s_out_ref, rows_of_first_pair):
    in_first = lax.broadcasted_iota(jnp.int32, (s_ref.shape[0], 1), 0) < rows_of_first_pair
    v = jnp.where(in_first, v0_ref[...].astype(F32), v1_ref[...].astype(F32))
    s_out_ref[...] = _column(a_ref[...]) * s_ref[...] + _column(kb_ref[...]) * v


def _ffn1_body(hp_ref, hd_ref, w1_ref, w3_ref, a_ref, kb_ref, v0_ref, v1_ref, s_ref,
               op_ref, od_ref, s_out_ref, w1bf_ref, w3bf_ref, *, dk, n_blocks):
    _cast_once(w1_ref, w1bf_ref)
    _cast_once(w3_ref, w3bf_ref)
    blk = jnp.minimum(pl.program_id(0) * pl.num_programs(1) + pl.program_id(1), n_blocks - 1)
    rows_of_first_pair = dk - (blk * s_ref.shape[0]) % dk

    def swiglu(h_ref, o_ref):
        h = h_ref[...]
        o_ref[...] = (_silu(_dot(h, w1bf_ref[...])) * _dot(h, w3bf_ref[...])).astype(o_ref.dtype)
        _state_job(a_ref, kb_ref, v0_ref, v1_ref, s_ref, s_out_ref, rows_of_first_pair)

    _row_steps(functools.partial(swiglu, hp_ref, op_ref), functools.partial(swiglu, hd_ref, od_ref))


def _ffn1(hp, hd, w1, w3, tm, tn, a, kb, vg, state):
    mp, kdim = hp.shape
    md = hd.shape[0]
    n = w1.shape[1]
    mt = mp // tm
    pr = _prow(mt)
    n_seq, n_heads, dk, dv = state.shape
    n_rows = n_seq * n_heads * dk
    rb = _state_block_rows(n_rows, dk, (n // tn) * (mt + 1))
    n_blocks = n_rows // rb
    block = lambda j, i: jnp.minimum(j * (mt + 1) + i, n_blocks - 1)

    def value_row(which):
        def index(j, i):
            p = jnp.minimum(block(j, i) * rb // dk + which, n_seq * n_heads - 1)
            return p // n_heads, 0, p % n_heads
        return pl.BlockSpec((None, 1, dv), index)

    row_vec = pl.BlockSpec((None, 1, rb), lambda j, i: (block(j, i), 0, 0))
    st = pl.BlockSpec((rb, dv), lambda j, i: (block(j, i), 0))
    wspec = pl.BlockSpec((kdim, tn), lambda j, i: (0, j))
    act_p, act_d, state_new = pl.pallas_call(
        functools.partial(_ffn1_body, dk=dk, n_blocks=n_blocks),
        grid=(n // tn, mt + 1),
        in_specs=[pl.BlockSpec((tm, kdim), lambda j, i: (pr(j, i), 0)),
                  pl.BlockSpec((md, kdim), lambda j, i: (0, 0)),
                  wspec, wspec, row_vec, row_vec, value_row(0), value_row(1), st],
        out_specs=[pl.BlockSpec((tm, tn), lambda j, i: (pr(j, i), j)),
                   pl.BlockSpec((md, tn), lambda j, i: (0, j)),
                   st],
        out_shape=[jax.ShapeDtypeStruct((mp, n), BF16), jax.ShapeDtypeStruct((md, n), BF16),
                   jax.ShapeDtypeStruct((n_rows, dv), F32)],
        scratch_shapes=[pltpu.VMEM((kdim, tn), BF16), pltpu.VMEM((kdim, tn), BF16)],
        compiler_params=_params(2),
        name="ffn1",
    )(hp, hd, w1, w3, a.reshape(n_blocks, 1, rb), kb.reshape(n_blocks, 1, rb), vg, vg,
      state.reshape(n_rows, dv))
    return act_p, act_d, state_new.reshape(state.shape)


FFN2_WEIGHT_CHUNKS = 8
FFN2_VMEM_LIMIT = 61 * 1024 * 1024


def _ffn2_body(ap_ref, ad_ref, w_hbm, xp_ref, xd_ref, gp_ref, gd_ref, outp_ref, outd_ref,
               wbf_ref, stage_ref, sem, *, mt, tpb):
    j, i = pl.program_id(0), pl.program_id(1)
    kc, tn = stage_ref.shape[1], stage_ref.shape[2]
    n_chunks = wbf_ref.shape[1] // kc
    slot = j % 2

    def chunk(col_tile, c):
        return pltpu.make_async_copy(w_hbm.at[pl.ds(c * kc, kc), pl.ds(col_tile * tn, tn)],
                                     stage_ref.at[c % 2], sem.at[c % 2])

    def land(dst_slot, c):
        wbf_ref[dst_slot, pl.ds(c * kc, kc), :] = stage_ref[c % 2].astype(BF16)

    @pl.when((j == 0) & (i == 0))
    def _():
        for c in range(n_chunks):
            chunk(0, c).start()
            chunk(0, c).wait()
            land(0, c)

    @pl.when(j + 1 < pl.num_programs(0))
    def _():
        @pl.when(i == 0)
        def _():
            chunk(j + 1, 0).start()

        for c in range(n_chunks):
            @pl.when(i == c + 1)
            def _(c=c):
                chunk(j + 1, c).wait()
                if c + 1 < n_chunks:
                    chunk(j + 1, c + 1).start()
                land(1 - slot, c)

    seq = _prompt_seq(mt, tpb)

    def prompt():
        outp_ref[...] = xp_ref[...] + gp_ref[pl.ds(seq, 1), :] * _dot(ap_ref[...], wbf_ref[slot])

    def decode():
        outd_ref[...] = xd_ref[...] + gd_ref[...] * _dot(ad_ref[...], wbf_ref[slot])

    _row_steps(prompt, decode)


def _ffn2(ap, ad, w, xp, xd, mod, gt_chunk, tm, tn, tpb):
    mp, kdim = ap.shape
    md = ad.shape[0]
    n = w.shape[1]
    mt = mp // tm
    per = n // tn
    pr = _prow(mt)
    kc = kdim // FFN2_WEIGHT_CHUNKS
    assert kc * FFN2_WEIGHT_CHUNKS == kdim and kc % 16 == 0 and mt >= FFN2_WEIGHT_CHUNKS
    gcol = lambda j, i: gt_chunk * per + j
    return pl.pallas_call(
        functools.partial(_ffn2_body, mt=mt, tpb=tpb),
        grid=(per, mt + 1),
        in_specs=[pl.BlockSpec((tm, kdim), lambda j, i: (pr(j, i), 0)),
                  pl.BlockSpec((md, kdim), lambda j, i: (0, 0), pipeline_mode=pl.Buffered(1)),
                  pl.BlockSpec(memory_space=pl.ANY),
                  pl.BlockSpec((tm, tn), lambda j, i: (pr(j, i), j)),
                  pl.BlockSpec((md, tn), lambda j, i: (0, j)),
                  _prompt_mod_spec(md, tn, gcol),
                  _decode_mod_spec(md, tn, gcol)],
        out_specs=[pl.BlockSpec((tm, tn), lambda j, i: (pr(j, i), j)),
                   pl.BlockSpec((md, tn), lambda j, i: (0, j))],
        out_shape=[jax.ShapeDtypeStruct((mp, n), F32), jax.ShapeDtypeStruct((md, n), F32)],
        scratch_shapes=[pltpu.VMEM((2, kdim, tn), BF16), pltpu.VMEM((2, kc, tn), F32),
                        pltpu.SemaphoreType.DMA((2,))],
        compiler_params=_params(2, FFN2_VMEM_LIMIT),
        name="ffn2",
    )(ap, ad, w, xp, xd, mod, mod)


def _pool_group_matmul(d, wp_ref, ps_ref, n_groups, gw):
    outs = []
    for gi in range(n_groups):
        dg = d[:, gi * gw:(gi + 1) * gw].astype(BF16)
        outs.append(_dot(dg, wp_ref[gi].astype(BF16)))
    return jnp.concatenate(outs, axis=-1) * ps_ref[...]


def _pool_prompt_tile(t, n_tiles, u_ref, wp_ref, ps_ref, y_ref, st_ref, buf_ref):
    tm = u_ref.shape[0]
    halo = POOL_STATE + 1
    n_groups = len(POOL_WINDOWS)
    gw = u_ref.shape[1] // n_groups

    @pl.when(t == 0)
    def _():
        buf_ref[:halo, :] = jnp.zeros((halo, buf_ref.shape[1]), F32)

    u = u_ref[...]
    buf_ref[halo:, :] = u
    pos = t * tm + lax.broadcasted_iota(jnp.int32, (tm, 1), 0)
    parts = []
    for gi, w in enumerate(POOL_WINDOWS):
        lo, hi = gi * gw, (gi + 1) * gw
        win = u[:, lo:hi]
        for s in range(1, w):
            win = win + buf_ref[halo - s:halo - s + tm, lo:hi]
        cnt = jnp.minimum(w, pos + 1).astype(F32)
        parts.append(win / cnt - u[:, lo:hi])
    d = jnp.concatenate(parts, axis=-1)
    y_ref[...] = _pool_group_matmul(d, wp_ref, ps_ref, n_groups, gw).astype(y_ref.dtype)

    @pl.when(t == n_tiles - 1)
    def _():
        st_ref[...] = buf_ref[halo + tm - POOL_STATE:, :]

    buf_ref[:halo, :] = buf_ref[tm:, :]


def _pool_decode_body(u_ref, sp_ref, wp_ref, ps_ref, y_ref, st_ref):
    n_groups = len(POOL_WINDOWS)
    gw = u_ref.shape[1] // n_groups
    u = u_ref[...]
    parts = []
    for gi, w in enumerate(POOL_WINDOWS):
        lo, hi = gi * gw, (gi + 1) * gw
        win = u[:, lo:hi]
        for s in range(1, w):
            win = win + sp_ref[POOL_STATE - s, :, lo:hi]
        cnt = float(min(w, PAST_LEN + 1))
        parts.append(win / cnt - u[:, lo:hi])
    d = jnp.concatenate(parts, axis=-1)
    y_ref[...] = _pool_group_matmul(d, wp_ref, ps_ref, n_groups, gw).astype(y_ref.dtype)
    for r in range(POOL_STATE - 1):
        st_ref[r] = sp_ref[r + 1]
    st_ref[POOL_STATE - 1] = u


def _pool_decode(u, ucol, state_t, w_pool, pool_scale, tb):
    nb = u.shape[0]
    g, gw, _ = w_pool.shape
    dp = g * gw
    return pl.pallas_call(
        _pool_decode_body,
        grid=(nb // tb,),
        in_specs=[pl.BlockSpec((tb, dp), lambda b: (b, ucol)),
                  pl.BlockSpec((POOL_STATE, tb, dp), lambda b: (0, b, 0)),
                  pl.BlockSpec((g, gw, gw), lambda b: (0, 0, 0)),
                  pl.BlockSpec((1, dp), lambda b: (0, 0))],
        out_specs=[pl.BlockSpec((tb, dp), lambda b: (b, 0)),
                   pl.BlockSpec((POOL_STATE, tb, dp), lambda b: (0, b, 0))],
        out_shape=[jax.ShapeDtypeStruct((nb, dp), BF16),
                   jax.ShapeDtypeStruct((POOL_STATE, nb, dp), F32)],
        compiler_params=_params(1),
        name="pool_decode",
    )(u, state_t, w_pool, pool_scale)


def _head_out(o, g, gain):
    return _silu(g) * _rms(o, gain)


def _column(row_vec):
    return jnp.broadcast_to(row_vec, (LANES, row_vec.shape[1])).T[:, :1]


def _cumsum_rows(x):
    n = x.shape[0]
    tri = (lax.broadcasted_iota(jnp.int32, (n, n), 0) >= lax.broadcasted_iota(jnp.int32, (n, n), 1))
    tri = jnp.where(tri, 1.0, 0.0).astype(BF16)
    hi = x.astype(BF16)
    r1 = x - hi.astype(F32)
    mid = r1.astype(BF16)
    lo = (r1 - mid.astype(F32)).astype(BF16)
    return _dot(tri, hi) + _dot(tri, mid) + _dot(tri, lo)


def _exact_scores(b_scr, q_scr, k_scr, a_scr):
    rows = b_scr.shape[0]

    def block(i, carry):
        r0 = pl.multiple_of(i * GLA_SUB, GLA_SUB)
        b_i = b_scr[pl.ds(r0, GLA_SUB), :]
        q_i = q_scr[pl.ds(r0, GLA_SUB), :]
        b_first = b_scr[pl.ds(r0, 1), :]
        qt = (q_i * jnp.exp(b_i - b_first)).astype(BF16)
        kt = (k_scr[...] * jnp.exp(jnp.minimum(b_first - b_scr[...], 0.0))).astype(BF16)
        panel = _dot(qt, kt, NT)
        col = lax.broadcasted_iota(jnp.int32, (GLA_SUB, rows), 1)
        row = lax.broadcasted_iota(jnp.int32, (GLA_SUB, rows), 0) + r0
        diag = jnp.zeros((GLA_SUB, rows), F32)
        for j in range(GLA_SUB):
            b_j = b_scr[pl.ds(r0 + j, 1), :]
            k_j = k_scr[pl.ds(r0 + j, 1), :]
            p = q_i * k_j * jnp.exp(jnp.minimum(b_i - b_j, 0.0))
            diag = jnp.where(col == r0 + j, jnp.sum(p, axis=-1, keepdims=True), diag)
        a_scr[pl.ds(r0, GLA_SUB), :] = jnp.where(col < r0, panel, jnp.where(col <= row, diag, 0.0))
        return carry

    lax.fori_loop(0, rows // GLA_SUB, block, 0)


def _gla_prompt_body(q_ref, k_ref, v_ref, g_ref, alr_ref, wa_ref, ba_ref, gn_ref, u_ref, wp_ref, ps_ref,
                     o_ref, s_ref, py_ref, pst_ref,
                     b_all, qe_all, b_scr, q_scr, k_scr, a_scr, pbuf_ref, *, n_chunks):
    _pool_prompt_tile(pl.program_id(1), n_chunks, u_ref, wp_ref, ps_ref, py_ref, pst_ref, pbuf_ref)
    rows = q_ref.shape[0]
    dk = q_ref.shape[1] // GLA_HEADS
    dv = v_ref.shape[1] // GLA_HEADS
    scale = dk ** -0.5
    heads = [(h, slice(h * dk, (h + 1) * dk), slice(h * dv, (h + 1) * dv)) for h in range(GLA_HEADS)]

    @pl.when(pl.program_id(1) == 0)
    def _():
        s_ref[...] = jnp.zeros(s_ref.shape, F32)

    x = _dot(alr_ref[...].astype(BF16), wa_ref[...].astype(BF16)) + ba_ref[...]
    b_all[...] = _cumsum_rows(_log_sigmoid(x) / GATE_NORM)
    qe_all[...] = (q_ref[...] * scale * jnp.exp(b_all[...])).astype(BF16)

    mild = jnp.max(-b_all[rows - 1:rows, :]) < GLA_SAFE_DECAY

    @pl.when(mild)
    def _():
        causal = (lax.broadcasted_iota(jnp.int32, (rows, rows), 1)
                  <= lax.broadcasted_iota(jnp.int32, (rows, rows), 0))
        for h, ks, _ in heads:
            kinv = (k_ref[:, ks] * jnp.exp(-b_all[:, ks])).astype(BF16)
            a_scr[h] = jnp.where(causal, _dot(qe_all[:, ks], kinv, NT), 0.0)

    @pl.when(jnp.logical_not(mild))
    def _():
        for h, ks, _ in heads:
            b_scr[...] = b_all[:, ks]
            q_scr[...] = q_ref[:, ks] * scale
            k_scr[...] = k_ref[:, ks]
            _exact_scores(b_scr, q_scr, k_scr, a_scr.at[h])

    for h, ks, vs in heads:
        b = b_all[:, ks]
        b_last = b_all[rows - 1:rows, ks]
        vb = v_ref[:, vs]
        s_old = s_ref[h]
        o = _dot(qe_all[:, ks], s_old.astype(BF16)) + _dot(a_scr[h].astype(BF16), vb)
        o_ref[:, vs] = _head_out(o, g_ref[:, vs].astype(F32), gn_ref[h]).astype(o_ref.dtype)
        kd = (k_ref[:, ks] * jnp.exp(b_last - b)).astype(BF16)
        s_ref[h] = _column(jnp.exp(b_last)) * s_old + _dot(kd, vb, TN)


def _mixers_prompt(qku, ucol, vg, alr, wa, ba, gn, w_pool, pool_scale, nb, t_len):
    dkk = wa.shape[1]
    dk, dv = dkk // GLA_HEADS, gn.shape[2]
    dvv = GLA_HEADS * dv
    g, gw, _ = w_pool.shape
    dp = g * gw
    c = GLA_CHUNK
    n_chunks = t_len // c
    tok = lambda w, blk=0: pl.BlockSpec((c, w), lambda b, i: (b * n_chunks + i, blk))
    return pl.pallas_call(
        functools.partial(_gla_prompt_body, n_chunks=n_chunks),
        grid=(nb, n_chunks),
        in_specs=[tok(dkk), tok(dkk, 1), tok(dvv), tok(dvv, 1), tok(LANES),
                  pl.BlockSpec((LANES, dkk), lambda b, i: (0, 0)),
                  pl.BlockSpec((1, dkk), lambda b, i: (0, 0)),
                  pl.BlockSpec((GLA_HEADS, 1, dv), lambda b, i: (0, 0, 0)),
                  tok(dp, ucol),
                  pl.BlockSpec((g, gw, gw), lambda b, i: (0, 0, 0)),
                  pl.BlockSpec((1, dp), lambda b, i: (0, 0))],
        out_specs=[tok(dvv),
                   pl.BlockSpec((None, GLA_HEADS, dk, dv), lambda b, i: (b, 0, 0, 0)),
                   tok(dp),
                   pl.BlockSpec((None, POOL_STATE, dp), lambda b, i: (b, 0, 0))],
        out_shape=[jax.ShapeDtypeStruct((nb * t_len, dvv), BF16),
                   jax.ShapeDtypeStruct((nb, GLA_HEADS, dk, dv), F32),
                   jax.ShapeDtypeStruct((nb * t_len, dp), BF16),
                   jax.ShapeDtypeStruct((nb, POOL_STATE, dp), F32)],
        scratch_shapes=[pltpu.VMEM((c, dkk), F32), pltpu.VMEM((c, dkk), BF16),
                        pltpu.VMEM((c, dk), F32), pltpu.VMEM((c, dk), F32), pltpu.VMEM((c, dk), F32),
                        pltpu.VMEM((GLA_HEADS, c, c), F32),
                        pltpu.VMEM((POOL_STATE + 1 + c, dp), F32)],
        compiler_params=_params(2),
        name="mixers_prompt",
    )(qku, qku, vg, vg, alr, wa, ba, gn, qku, w_pool, pool_scale)


def _gla_decode_body(q_ref, k_ref, v_ref, g_ref, alr_ref, wa_ref, ba_ref, gn_ref, s_ref, o_ref, a_ref, kb_ref):
    dk = q_ref.shape[2] // GLA_HEADS
    dv = v_ref.shape[2] // GLA_HEADS
    for r in range(q_ref.shape[0]):
        alr = alr_ref[r].astype(BF16)
        for h in range(GLA_HEADS):
            ks, vs = slice(h * dk, (h + 1) * dk), slice(h * dv, (h + 1) * dv)
            q = q_ref[r, :, ks] * (dk ** -0.5)
            kb = k_ref[r, :, ks].astype(BF16).astype(F32)
            vb = v_ref[r, :, vs].astype(F32)
            x = _dot(alr, wa_ref[:, ks].astype(BF16)) + ba_ref[:, ks]
            a = jnp.exp(_log_sigmoid(x) / GATE_NORM)
            a_ref[r, :, ks] = a
            kb_ref[r, :, ks] = kb
            o = jnp.sum(_column(q * a) * s_ref[r, h], axis=0, keepdims=True)
            o += jnp.sum(q * kb, axis=-1, keepdims=True) * vb
            o_ref[r, :, vs] = _head_out(o, g_ref[r, :, vs].astype(F32), gn_ref[h]).astype(o_ref.dtype)


def _gla_decode(qku, vg, alr, wa, ba, gn, state, tb=4):
    nb = qku.shape[0]
    dkk = wa.shape[1]
    dk, dv = dkk // GLA_HEADS, gn.shape[2]
    dvv = GLA_HEADS * dv
    tok = lambda w, blk=0: pl.BlockSpec((tb, 1, w), lambda b: (b, 0, blk))
    return pl.pallas_call(
        _gla_decode_body,
        grid=(nb // tb,),
        in_specs=[tok(dkk), tok(dkk, 1), tok(dvv), tok(dvv, 1), tok(LANES),
                  pl.BlockSpec((LANES, dkk), lambda b: (0, 0)),
                  pl.BlockSpec((1, dkk), lambda b: (0, 0)),
                  pl.BlockSpec((GLA_HEADS, 1, dv), lambda b: (0, 0, 0)),
                  pl.BlockSpec((tb, GLA_HEADS, dk, dv), lambda b: (b, 0, 0, 0))],
        out_specs=[tok(dvv), tok(dkk), tok(dkk)],
        out_shape=[jax.ShapeDtypeStruct((nb, 1, dvv), BF16),
                   jax.ShapeDtypeStruct((nb, 1, dkk), F32),
                   jax.ShapeDtypeStruct((nb, 1, dkk), F32)],
        compiler_params=_params(1),
        name="gla_decode",
    )(qku, qku, vg, vg, alr, wa, ba, gn, state)


def _tiles(t_len):
    return min(1024, t_len), min(512, t_len), min(512, t_len)


def _layer(xp, xd, mod, lw, dims, nb, t_len, pool_state_t, gla_state):
    (g1, w_in_t, w_alr_t, w_pool, pool_scale, wa, ba, gn, w_out, g2, w_ff1, w_ff3, w_ff2) = lw
    d_pool, d_k, d_v = dims
    n_dec = xd.shape[0]
    tm, tms, tmn = _tiles(t_len)
    tpb, tpbs, tpbn = t_len // tm, t_len // tms, t_len // tmn
    tn = 512

    h1p, ap = _normmod(xp, g1, mod, n_dec, 1, 0, tmn, tpbn, w_alr_t)
    h1d, ad = _normmod(xd, g1, mod, n_dec, 1, 0, n_dec, None, w_alr_t)
    u_tiles, qk_tiles, vg_tiles = d_pool // tn, 2 * d_k // tn, 2 * d_v // tn
    qku_block = lambda j: jnp.where(j < qk_tiles, j + u_tiles, j - qk_tiles)
    qkup, qkud = _proj(h1p, h1d, w_in_t, qk_tiles + u_tiles, qku_block, tm, tn, F32)
    vgp, vgd = _proj(h1p, h1d, w_in_t, vg_tiles, lambda j: j + u_tiles + qk_tiles, tm, tn, BF16)
    ucol = 2 * d_k // d_pool

    op, gla_p, pyp, pool_p = _mixers_prompt(qkup, ucol, vgp, ap, wa, ba, gn, w_pool, pool_scale, nb, t_len)
    pyd, pool_d_t = _pool_decode(qkud, ucol, pool_state_t, w_pool, pool_scale, 32)
    r3 = lambda t: t.reshape(t.shape[0], 1, t.shape[1])
    vgd3 = r3(vgd)
    od, a_d, kb_d = _gla_decode(r3(qkud), vgd3, r3(ad), wa, ba, gn, gla_state)
    od = od.reshape(n_dec, d_v)

    x1p, x1d = _outproj(pyp, op, pyd, od, w_out, xp, xd, mod, 2, tm, tn, tpb)
    h2p = _normmod(x1p, g2, mod, n_dec, 4, 3, tmn, tpbn)
    h2d = _normmod(x1d, g2, mod, n_dec, 4, 3, n_dec, None)
    actp, actd, gla_d = _ffn1(h2p, h2d, w_ff1, w_ff3, tm, 256, a_d, kb_d, vgd3, gla_state)
    x2p, x2d = _ffn2(actp, actd, w_ff2, x1p, x1d, mod, 5, tms, tn, tpbs)
    return x2p, x2d, pool_p, gla_p, pool_d_t, gla_d


def kernel(x_prompt, x_sample, state_pool, state_gla, c_prompt, c_sample, w_ada, b_ada, g_norm1, w_in,
           w_pool, pool_scale, w_a2, b_a, g_gla_out, w_out, g_norm2, w_ff1, w_ff3, w_ff2, g_final):
    bp, tp, d = x_prompt.shape
    bs, ts, _ = x_sample.shape
    depth = w_ada.shape[0]
    assert ts == 1, "the decode path handles one new token per sequence"
    assert bs % SUBLANES == 0 and bp <= SUBLANES
    d_pool = w_pool.shape[1] * w_pool.shape[2]
    d_k = w_a2.shape[2]
    d_v = g_gla_out.shape[1] * g_gla_out.shape[2]
    dims = (d_pool, d_k, d_v)
    d_main = d_pool + 2 * d_k + 2 * d_v

    c_all = jnp.concatenate([c_sample, c_prompt, jnp.zeros((2 * SUBLANES - bp, d), F32)], axis=0)

    hp = x_prompt.reshape(bp * tp, d)
    hd = x_sample.reshape(bs, d)
    pools_p, glas_p, pools_d, glas_d = [], [], [], []
    for l in range(depth):
        mod = _ada(c_all, w_ada[l], b_ada[l].reshape(1, -1))
        w_in_t = w_in[l].T
        w_alr_t = jnp.pad(w_in_t[d_main:], ((0, LANES - GATE_RANK), (0, 0)))
        wa = jnp.pad(w_a2[l], ((0, LANES - GATE_RANK), (0, 0)))
        lw = (g_norm1[l].reshape(1, d), w_in_t, w_alr_t, w_pool[l], pool_scale[l].reshape(1, -1), wa,
              b_a[l].reshape(1, -1), g_gla_out[l].reshape(GLA_HEADS, 1, -1), w_out[l],
              g_norm2[l].reshape(1, d), w_ff1[l], w_ff3[l], w_ff2[l])
        pool_state_t = jnp.transpose(state_pool[l], (1, 0, 2))
        hp, hd, pool_p, gla_p, pool_d_t, gla_d = _layer(hp, hd, mod, lw, dims, bp, tp, pool_state_t,
                                                        state_gla[l])
        pools_p.append(pool_p)
        glas_p.append(gla_p)
        pools_d.append(jnp.transpose(pool_d_t, (1, 0, 2)))
        glas_d.append(gla_d)
    gf = g_final.reshape(1, d)
    y_p = _norm(hp, gf, _tiles(tp)[2]).reshape(bp, tp, d)
    y_d = _norm(hd, gf, bs).reshape(bs, ts, d)
    return (y_p, y_d, jnp.stack(pools_p), jnp.stack(glas_p), jnp.stack(pools_d), jnp.stack(glas_d))
```

```python
import functools

import jax
import jax.numpy as jnp
from jax import lax
from jax.experimental import pallas as pl
from jax.experimental.pallas import tpu as pltpu

F32 = jnp.float32
BF16 = jnp.bfloat16

POOL_WINDOWS = (2, 4, 8, 16)
POOL_STATE = max(POOL_WINDOWS) - 1
GLA_HEADS = 4
GATE_RANK = 16
GATE_NORM = 16.0
N_MOD = 6
EPS = 1e-6
PAST_LEN = 16384

LANES = 128
SUBLANES = 8
VMEM_LIMIT = 56 * 1024 * 1024
GLA_CHUNK = 256
GLA_SUB = 16
GLA_SAFE_DECAY = 60.0

NT = (((1,), (1,)), ((), ()))
TN = (((0,), (0,)), ((), ()))


def _params(n_axes, vmem_limit=VMEM_LIMIT):
    return pltpu.CompilerParams(dimension_semantics=("arbitrary",) * n_axes, vmem_limit_bytes=vmem_limit)


def _silu(x):
    return x * jax.nn.sigmoid(x)


def _log_sigmoid(x):
    return jnp.minimum(x, 0.0) - jnp.log(1.0 + jnp.exp(-jnp.abs(x)))


def _dot(a, b, dims=None):
    if dims is None:
        return jnp.dot(a, b, preferred_element_type=F32)
    return lax.dot_general(a, b, dims, preferred_element_type=F32)


def _ada_body(c_ref, w_ref, b_ref, o_ref):
    s = _silu(c_ref[...]).astype(BF16)
    o_ref[...] = _dot(s, w_ref[...].astype(BF16)) + b_ref[...]


def _ada(c, w, b, tn=1024):
    m, d = c.shape
    n = w.shape[1]
    return pl.pallas_call(
        _ada_body,
        grid=(n // tn,),
        in_specs=[pl.BlockSpec((m, d), lambda j: (0, 0)),
                  pl.BlockSpec((d, tn), lambda j: (0, j)),
                  pl.BlockSpec((1, tn), lambda j: (0, j))],
        out_specs=pl.BlockSpec((m, tn), lambda j: (0, j)),
        out_shape=jax.ShapeDtypeStruct((m, n), F32),
        compiler_params=_params(1),
        name="ada",
    )(c, w, b)


def _prompt_mod_spec(n_dec, width, col_of):
    return pl.BlockSpec((SUBLANES, width), lambda *g: (n_dec // SUBLANES, col_of(*g)))


def _decode_mod_spec(n_dec, width, col_of):
    return pl.BlockSpec((n_dec, width), lambda *g: (0, col_of(*g)))


def _rms(x, gain):
    return x * lax.rsqrt(jnp.mean(x * x, axis=-1, keepdims=True) + EPS) * gain


def _normmod_body(x_ref, g_ref, sc_ref, sh_ref, *rest, tpb):
    if tpb is None:
        sc, sh = sc_ref[...], sh_ref[...]
    else:
        b = pl.program_id(0) // tpb
        sc, sh = sc_ref[pl.ds(b, 1), :], sh_ref[pl.ds(b, 1), :]
    h = (_rms(x_ref[...], g_ref[...]) * (1.0 + sc) + sh).astype(BF16)
    if len(rest) == 1:
        (o_ref,) = rest
    else:
        wt_ref, o_ref, side_ref = rest
        side_ref[...] = _dot(h, wt_ref[...].astype(BF16), NT)
    o_ref[...] = h


def _normmod(x, gain, mod, n_dec, sc_chunk, sh_chunk, tm, tpb, side_wt=None):
    m, d = x.shape
    spec = _decode_mod_spec if tpb is None else _prompt_mod_spec
    in_specs = [pl.BlockSpec((tm, d), lambda i: (i, 0)),
                pl.BlockSpec((1, d), lambda i: (0, 0)),
                spec(n_dec, d, lambda i: sc_chunk),
                spec(n_dec, d, lambda i: sh_chunk)]
    out_specs = pl.BlockSpec((tm, d), lambda i: (i, 0))
    out_shape = jax.ShapeDtypeStruct((m, d), BF16)
    args = (x, gain, mod, mod)
    if side_wt is not None:
        ns = side_wt.shape[0]
        in_specs.append(pl.BlockSpec((ns, d), lambda i: (0, 0)))
        out_specs = [out_specs, pl.BlockSpec((tm, ns), lambda i: (i, 0))]
        out_shape = [out_shape, jax.ShapeDtypeStruct((m, ns), F32)]
        args += (side_wt,)
    return pl.pallas_call(
        functools.partial(_normmod_body, tpb=tpb),
        grid=(m // tm,),
        in_specs=in_specs,
        out_specs=out_specs,
        out_shape=out_shape,
        compiler_params=_params(1),
        name="normmod",
    )(*args)


def _norm_body(x_ref, g_ref, o_ref):
    o_ref[...] = _rms(x_ref[...], g_ref[...])


def _norm(x, gain, tm):
    m, d = x.shape
    return pl.pallas_call(
        _norm_body,
        grid=(m // tm,),
        in_specs=[pl.BlockSpec((tm, d), lambda i: (i, 0)),
                  pl.BlockSpec((1, d), lambda i: (0, 0))],
        out_specs=pl.BlockSpec((tm, d), lambda i: (i, 0)),
        out_shape=jax.ShapeDtypeStruct((m, d), F32),
        compiler_params=_params(1),
        name="finalnorm",
    )(x, gain)


def _cast_once(w_ref, wbf_ref):
    @pl.when(pl.program_id(1) == 0)
    def _():
        wbf_ref[...] = w_ref[...].astype(BF16)


def _row_steps(prompt_fn, decode_fn):
    i = pl.program_id(1)
    pl.when(i == 0)(decode_fn)
    pl.when(i > 0)(prompt_fn)


def _prow(mt):
    def tile(j, i):
        t = jnp.maximum(i - 1, 0)
        return jnp.where(j % 2 == 0, t, mt - 1 - t)
    return tile


def _prompt_seq(mt, tpb):
    return _prow(mt)(pl.program_id(0), pl.program_id(1)) // tpb


def _proj_body(ap_ref, ad_ref, wt_ref, op_ref, od_ref, wbf_ref):
    _cast_once(wt_ref, wbf_ref)

    def prompt():
        op_ref[...] = _dot(ap_ref[...], wbf_ref[...], NT).astype(op_ref.dtype)

    def decode():
        od_ref[...] = _dot(ad_ref[...], wbf_ref[...], NT).astype(od_ref.dtype)

    _row_steps(prompt, decode)


def _proj(ap, ad, wt, n_tiles, wblock, tm, tn, out_dtype, vmem_limit=VMEM_LIMIT):
    mp, kdim = ap.shape
    md = ad.shape[0]
    mt = mp // tm
    pr = _prow(mt)
    return pl.pallas_call(
        _proj_body,
        grid=(n_tiles, mt + 1),
        in_specs=[pl.BlockSpec((tm, kdim), lambda j, i: (pr(j, i), 0)),
                  pl.BlockSpec((md, kdim), lambda j, i: (0, 0), pipeline_mode=pl.Buffered(1)),
                  pl.BlockSpec((tn, kdim), lambda j, i: (wblock(j), 0))],
        out_specs=[pl.BlockSpec((tm, tn), lambda j, i: (pr(j, i), j)),
                   pl.BlockSpec((md, tn), lambda j, i: (0, j))],
        out_shape=[jax.ShapeDtypeStruct((mp, n_tiles * tn), out_dtype),
                   jax.ShapeDtypeStruct((md, n_tiles * tn), out_dtype)],
        scratch_shapes=[pltpu.VMEM((tn, kdim), BF16)],
        compiler_params=_params(2, vmem_limit),
        name="proj",
    )(ap, ad, wt)


def _outproj_body(pyp_ref, op_ref, pyd_ref, od_ref, w_ref, xp_ref, xd_ref, gp_ref, gd_ref,
                  outp_ref, outd_ref, wbf_ref, *, mt, tpb, d_pool):
    _cast_once(w_ref, wbf_ref)
    seq = _prompt_seq(mt, tpb)

    def mix(py_ref, o_ref):
        return _dot(py_ref[...], wbf_ref[:d_pool, :]) + _dot(o_ref[...], wbf_ref[d_pool:, :])

    def prompt():
        outp_ref[...] = xp_ref[...] + gp_ref[pl.ds(seq, 1), :] * mix(pyp_ref, op_ref)

    def decode():
        outd_ref[...] = xd_ref[...] + gd_ref[...] * mix(pyd_ref, od_ref)

    _row_steps(prompt, decode)


def _outproj(pyp, op, pyd, od, w, xp, xd, mod, gt_chunk, tm, tn, tpb):
    mp, d_pool = pyp.shape
    md = pyd.shape[0]
    d_v = op.shape[1]
    kdim, n = w.shape
    mt = mp // tm
    per = n // tn
    pr = _prow(mt)
    gcol = lambda j, i: gt_chunk * per + j
    return pl.pallas_call(
        functools.partial(_outproj_body, mt=mt, tpb=tpb, d_pool=d_pool),
        grid=(per, mt + 1),
        in_specs=[pl.BlockSpec((tm, d_pool), lambda j, i: (pr(j, i), 0)),
                  pl.BlockSpec((tm, d_v), lambda j, i: (pr(j, i), 0)),
                  pl.BlockSpec((md, d_pool), lambda j, i: (0, 0)),
                  pl.BlockSpec((md, d_v), lambda j, i: (0, 0)),
                  pl.BlockSpec((kdim, tn), lambda j, i: (0, j)),
                  pl.BlockSpec((tm, tn), lambda j, i: (pr(j, i), j)),
                  pl.BlockSpec((md, tn), lambda j, i: (0, j)),
                  _prompt_mod_spec(md, tn, gcol),
                  _decode_mod_spec(md, tn, gcol)],
        out_specs=[pl.BlockSpec((tm, tn), lambda j, i: (pr(j, i), j)),
                   pl.BlockSpec((md, tn), lambda j, i: (0, j))],
        out_shape=[jax.ShapeDtypeStruct((mp, n), F32), jax.ShapeDtypeStruct((md, n), F32)],
        scratch_shapes=[pltpu.VMEM((kdim, tn), BF16)],
        compiler_params=_params(2),
        name="outproj",
    )(pyp, op, pyd, od, w, xp, xd, mod, mod)


def _state_block_rows(n_rows, dk, n_steps):
    rb = -(-n_rows // n_steps)
    rb += -rb % LANES
    while n_rows % rb:
        rb += LANES
    assert dk % LANES == 0 and rb <= dk + LANES, "a row block may touch at most two (sequence, head) pairs"
    return rb


def _state_job(a_ref, kb_ref, v0_ref, v1_ref, s_ref, s_out_ref, rows_of_first_pair):
    in_first = lax.broadcasted_iota(jnp.int32, (s_ref.shape[0], 1), 0) < rows_of_first_pair
    v = jnp.where(in_first, v0_ref[...].astype(F32), v1_ref[...].astype(F32))
    s_out_ref[...] = _column(a_ref[...]) * s_ref[...] + _column(kb_ref[...]) * v


def _ffn1_body(hp_ref, hd_ref, w1_ref, w3_ref, a_ref, kb_ref, v0_ref, v1_ref, s_ref,
               op_ref, od_ref, s_out_ref, w1bf_ref, w3bf_ref, *, dk, n_blocks):
    _cast_once(w1_ref, w1bf_ref)
    _cast_once(w3_ref, w3bf_ref)
    blk = jnp.minimum(pl.program_id(0) * pl.num_programs(1) + pl.program_id(1), n_blocks - 1)
    rows_of_first_pair = dk - (blk * s_ref.shape[0]) % dk

    def swiglu(h_ref, o_ref):
        h = h_ref[...]
        o_ref[...] = (_silu(_dot(h, w1bf_ref[...])) * _dot(h, w3bf_ref[...])).astype(o_ref.dtype)
        _state_job(a_ref, kb_ref, v0_ref, v1_ref, s_ref, s_out_ref, rows_of_first_pair)

    _row_steps(functools.partial(swiglu, hp_ref, op_ref), functools.partial(swiglu, hd_ref, od_ref))


def _ffn1(hp, hd, w1, w3, tm, tn, a, kb, vg, state):
    mp, kdim = hp.shape
    md = hd.shape[0]
    n = w1.shape[1]
    mt = mp // tm
    pr = _prow(mt)
    n_seq, n_heads, dk, dv = state.shape
    n_rows = n_seq * n_heads * dk
    rb = _state_block_rows(n_rows, dk, (n // tn) * (mt + 1))
    n_blocks = n_rows // rb
    block = lambda j, i: jnp.minimum(j * (mt + 1) + i, n_blocks - 1)

    def value_row(which):
        def index(j, i):
            p = jnp.minimum(block(j, i) * rb // dk + which, n_seq * n_heads - 1)
            return p // n_heads, 0, p % n_heads
        return pl.BlockSpec((None, 1, dv), index)

    row_vec = pl.BlockSpec((None, 1, rb), lambda j, i: (block(j, i), 0, 0))
    st = pl.BlockSpec((rb, dv), lambda j, i: (block(j, i), 0))
    wspec = pl.BlockSpec((kdim, tn), lambda j, i: (0, j))
    act_p, act_d, state_new = pl.pallas_call(
        functools.partial(_ffn1_body, dk=dk, n_blocks=n_blocks),
        grid=(n // tn, mt + 1),
        in_specs=[pl.BlockSpec((tm, kdim), lambda j, i: (pr(j, i), 0)),
                  pl.BlockSpec((md, kdim), lambda j, i: (0, 0)),
                  wspec, wspec, row_vec, row_vec, value_row(0), value_row(1), st],
        out_specs=[pl.BlockSpec((tm, tn), lambda j, i: (pr(j, i), j)),
                   pl.BlockSpec((md, tn), lambda j, i: (0, j)),
                   st],
        out_shape=[jax.ShapeDtypeStruct((mp, n), BF16), jax.ShapeDtypeStruct((md, n), BF16),
                   jax.ShapeDtypeStruct((n_rows, dv), F32)],
        scratch_shapes=[pltpu.VMEM((kdim, tn), BF16), pltpu.VMEM((kdim, tn), BF16)],
        compiler_params=_params(2),
        name="ffn1",
    )(hp, hd, w1, w3, a.reshape(n_blocks, 1, rb), kb.reshape(n_blocks, 1, rb), vg, vg,
      state.reshape(n_rows, dv))
    return act_p, act_d, state_new.reshape(state.shape)


FFN2_WEIGHT_CHUNKS = 8
FFN2_VMEM_LIMIT = 62 * 1024 * 1024


def _ffn2_body(ap_ref, ad_ref, w_hbm, xp_ref, xd_ref, gp_ref, gd_ref, outp_ref, outd_ref,
               wbf_ref, stage_ref, sem, *, mt, tpb):
    j, i = pl.program_id(0), pl.program_id(1)
    kc, tn = stage_ref.shape[1], stage_ref.shape[2]
    n_chunks = wbf_ref.shape[1] // kc
    slot = j % 2

    def chunk(col_tile, c):
        return pltpu.make_async_copy(w_hbm.at[pl.ds(c * kc, kc), pl.ds(col_tile * tn, tn)],
                                     stage_ref.at[c % 2], sem.at[c % 2])

    def land(dst_slot, c):
        wbf_ref[dst_slot, pl.ds(c * kc, kc), :] = stage_ref[c % 2].astype(BF16)

    @pl.when((j == 0) & (i == 0))
    def _():
        for c in range(n_chunks):
            chunk(0, c).start()
            chunk(0, c).wait()
            land(0, c)

    @pl.when(j + 1 < pl.num_programs(0))
    def _():
        @pl.when(i == 0)
        def _():
            chunk(j + 1, 0).start()

        for c in range(n_chunks):
            @pl.when(i == c + 1)
            def _(c=c):
                chunk(j + 1, c).wait()
                if c + 1 < n_chunks:
                    chunk(j + 1, c + 1).start()
                land(1 - slot, c)

    seq = _prompt_seq(mt, tpb)

    def prompt():
        outp_ref[...] = xp_ref[...] + gp_ref[pl.ds(seq, 1), :] * _dot(ap_ref[...], wbf_ref[slot])

    def decode():
        outd_ref[...] = xd_ref[...] + gd_ref[...] * _dot(ad_ref[...], wbf_ref[slot])

    _row_steps(prompt, decode)


def _ffn2(ap, ad, w, xp, xd, mod, gt_chunk, tm, tn, tpb):
    mp, kdim = ap.shape
    md = ad.shape[0]
    n = w.shape[1]
    mt = mp // tm
    per = n // tn
    pr = _prow(mt)
    kc = kdim // FFN2_WEIGHT_CHUNKS
    assert kc * FFN2_WEIGHT_CHUNKS == kdim and kc % 16 == 0 and mt >= FFN2_WEIGHT_CHUNKS
    gcol = lambda j, i: gt_chunk * per + j
    return pl.pallas_call(
        functools.partial(_ffn2_body, mt=mt, tpb=tpb),
        grid=(per, mt + 1),
        in_specs=[pl.BlockSpec((tm, kdim), lambda j, i: (pr(j, i), 0)),
                  pl.BlockSpec((md, kdim), lambda j, i: (0, 0), pipeline_mode=pl.Buffered(1)),
                  pl.BlockSpec(memory_space=pl.ANY),
                  pl.BlockSpec((tm, tn), lambda j, i: (pr(j, i), j)),
                  pl.BlockSpec((md, tn), lambda j, i: (0, j)),
                  _prompt_mod_spec(md, tn, gcol),
                  _decode_mod_spec(md, tn, gcol)],
        out_specs=[pl.BlockSpec((tm, tn), lambda j, i: (pr(j, i), j)),
                   pl.BlockSpec((md, tn), lambda j, i: (0, j))],
        out_shape=[jax.ShapeDtypeStruct((mp, n), F32), jax.ShapeDtypeStruct((md, n), F32)],
        scratch_shapes=[pltpu.VMEM((2, kdim, tn), BF16), pltpu.VMEM((2, kc, tn), F32),
                        pltpu.SemaphoreType.DMA((2,))],
        compiler_params=_params(2, FFN2_VMEM_LIMIT),
        name="ffn2",
    )(ap, ad, w, xp, xd, mod, mod)


def _pool_group_matmul(d, wp_ref, ps_ref, n_groups, gw):
    outs = []
    for gi in range(n_groups):
        dg = d[:, gi * gw:(gi + 1) * gw].astype(BF16)
        outs.append(_dot(dg, wp_ref[gi].astype(BF16)))
    return jnp.concatenate(outs, axis=-1) * ps_ref[...]


def _pool_prompt_tile(t, n_tiles, u_ref, wp_ref, ps_ref, y_ref, st_ref, buf_ref):
    tm = u_ref.shape[0]
    halo = POOL_STATE + 1
    n_groups = len(POOL_WINDOWS)
    gw = u_ref.shape[1] // n_groups

    @pl.when(t == 0)
    def _():
        buf_ref[:halo, :] = jnp.zeros((halo, buf_ref.shape[1]), F32)

    u = u_ref[...]
    buf_ref[halo:, :] = u
    pos = t * tm + lax.broadcasted_iota(jnp.int32, (tm, 1), 0)
    parts = []
    for gi, w in enumerate(POOL_WINDOWS):
        lo, hi = gi * gw, (gi + 1) * gw
        win = u[:, lo:hi]
        for s in range(1, w):
            win = win + buf_ref[halo - s:halo - s + tm, lo:hi]
        cnt = jnp.minimum(w, pos + 1).astype(F32)
        parts.append(win / cnt - u[:, lo:hi])
    d = jnp.concatenate(parts, axis=-1)
    y_ref[...] = _pool_group_matmul(d, wp_ref, ps_ref, n_groups, gw).astype(y_ref.dtype)

    @pl.when(t == n_tiles - 1)
    def _():
        st_ref[...] = buf_ref[halo + tm - POOL_STATE:, :]

    buf_ref[:halo, :] = buf_ref[tm:, :]


def _pool_decode_body(u_ref, sp_ref, wp_ref, ps_ref, y_ref, st_ref):
    n_groups = len(POOL_WINDOWS)
    gw = u_ref.shape[1] // n_groups
    u = u_ref[...]
    parts = []
    for gi, w in enumerate(POOL_WINDOWS):
        lo, hi = gi * gw, (gi + 1) * gw
        win = u[:, lo:hi]
        for s in range(1, w):
            win = win + sp_ref[POOL_STATE - s, :, lo:hi]
        cnt = float(min(w, PAST_LEN + 1))
        parts.append(win / cnt - u[:, lo:hi])
    d = jnp.concatenate(parts, axis=-1)
    y_ref[...] = _pool_group_matmul(d, wp_ref, ps_ref, n_groups, gw).astype(y_ref.dtype)
    for r in range(POOL_STATE - 1):
        st_ref[r] = sp_ref[r + 1]
    st_ref[POOL_STATE - 1] = u


def _pool_decode(u, ucol, state_t, w_pool, pool_scale, tb):
    nb = u.shape[0]
    g, gw, _ = w_pool.shape
    dp = g * gw
    return pl.pallas_call(
        _pool_decode_body,
        grid=(nb // tb,),
        in_specs=[pl.BlockSpec((tb, dp), lambda b: (b, ucol)),
                  pl.BlockSpec((POOL_STATE, tb, dp), lambda b: (0, b, 0)),
                  pl.BlockSpec((g, gw, gw), lambda b: (0, 0, 0)),
                  pl.BlockSpec((1, dp), lambda b: (0, 0))],
        out_specs=[pl.BlockSpec((tb, dp), lambda b: (b, 0)),
                   pl.BlockSpec((POOL_STATE, tb, dp), lambda b: (0, b, 0))],
        out_shape=[jax.ShapeDtypeStruct((nb, dp), BF16),
                   jax.ShapeDtypeStruct((POOL_STATE, nb, dp), F32)],
        compiler_params=_params(1),
        name="pool_decode",
    )(u, state_t, w_pool, pool_scale)


def _head_out(o, g, gain):
    return _silu(g) * _rms(o, gain)


def _column(row_vec):
    return jnp.broadcast_to(row_vec, (LANES, row_vec.shape[1])).T[:, :1]


def _cumsum_rows(x):
    n = x.shape[0]
    tri = (lax.broadcasted_iota(jnp.int32, (n, n), 0) >= lax.broadcasted_iota(jnp.int32, (n, n), 1))
    tri = jnp.where(tri, 1.0, 0.0).astype(BF16)
    hi = x.astype(BF16)
    r1 = x - hi.astype(F32)
    mid = r1.astype(BF16)
    lo = (r1 - mid.astype(F32)).astype(BF16)
    return _dot(tri, hi) + _dot(tri, mid) + _dot(tri, lo)


def _exact_scores(b_scr, q_scr, k_scr, a_scr):
    rows = b_scr.shape[0]

    def block(i, carry):
        r0 = pl.multiple_of(i * GLA_SUB, GLA_SUB)
        b_i = b_scr[pl.ds(r0, GLA_SUB), :]
        q_i = q_scr[pl.ds(r0, GLA_SUB), :]
        b_first = b_scr[pl.ds(r0, 1), :]
        qt = (q_i * jnp.exp(b_i - b_first)).astype(BF16)
        kt = (k_scr[...] * jnp.exp(jnp.minimum(b_first - b_scr[...], 0.0))).astype(BF16)
        panel = _dot(qt, kt, NT)
        col = lax.broadcasted_iota(jnp.int32, (GLA_SUB, rows), 1)
        row = lax.broadcasted_iota(jnp.int32, (GLA_SUB, rows), 0) + r0
        diag = jnp.zeros((GLA_SUB, rows), F32)
        for j in range(GLA_SUB):
            b_j = b_scr[pl.ds(r0 + j, 1), :]
            k_j = k_scr[pl.ds(r0 + j, 1), :]
            p = q_i * k_j * jnp.exp(jnp.minimum(b_i - b_j, 0.0))
            diag = jnp.where(col == r0 + j, jnp.sum(p, axis=-1, keepdims=True), diag)
        a_scr[pl.ds(r0, GLA_SUB), :] = jnp.where(col < r0, panel, jnp.where(col <= row, diag, 0.0))
        return carry

    lax.fori_loop(0, rows // GLA_SUB, block, 0)


def _gla_prompt_body(q_ref, k_ref, v_ref, g_ref, alr_ref, wa_ref, ba_ref, gn_ref, u_ref, wp_ref, ps_ref,
                     o_ref, s_ref, py_ref, pst_ref,
                     b_all, qe_all, b_scr, q_scr, k_scr, a_scr, pbuf_ref, *, n_chunks):
    _pool_prompt_tile(pl.program_id(1), n_chunks, u_ref, wp_ref, ps_ref, py_ref, pst_ref, pbuf_ref)
    rows = q_ref.shape[0]
    dk = q_ref.shape[1] // GLA_HEADS
    dv = v_ref.shape[1] // GLA_HEADS
    scale = dk ** -0.5
    heads = [(h, slice(h * dk, (h + 1) * dk), slice(h * dv, (h + 1) * dv)) for h in range(GLA_HEADS)]

    @pl.when(pl.program_id(1) == 0)
    def _():
        s_ref[...] = jnp.zeros(s_ref.shape, F32)

    x = _dot(alr_ref[...].astype(BF16), wa_ref[...].astype(BF16)) + ba_ref[...]
    b_all[...] = _cumsum_rows(_log_sigmoid(x) / GATE_NORM)
    qe_all[...] = (q_ref[...] * scale * jnp.exp(b_all[...])).astype(BF16)

    mild = jnp.max(-b_all[rows - 1:rows, :]) < GLA_SAFE_DECAY

    @pl.when(mild)
    def _():
        causal = (lax.broadcasted_iota(jnp.int32, (rows, rows), 1)
                  <= lax.broadcasted_iota(jnp.int32, (rows, rows), 0))
        for h, ks, _ in heads:
            kinv = (k_ref[:, ks] * jnp.exp(-b_all[:, ks])).astype(BF16)
            a_scr[h] = jnp.where(causal, _dot(qe_all[:, ks], kinv, NT), 0.0)

    @pl.when(jnp.logical_not(mild))
    def _():
        for h, ks, _ in heads:
            b_scr[...] = b_all[:, ks]
            q_scr[...] = q_ref[:, ks] * scale
            k_scr[...] = k_ref[:, ks]
            _exact_scores(b_scr, q_scr, k_scr, a_scr.at[h])

    for h, ks, vs in heads:
        b = b_all[:, ks]
        b_last = b_all[rows - 1:rows, ks]
        vb = v_ref[:, vs]
        s_old = s_ref[h]
        o = _dot(qe_all[:, ks], s_old.astype(BF16)) + _dot(a_scr[h].astype(BF16), vb)
        o_ref[:, vs] = _head_out(o, g_ref[:, vs].astype(F32), gn_ref[h]).astype(o_ref.dtype)
        kd = (k_ref[:, ks] * jnp.exp(b_last - b)).astype(BF16)
        s_ref[h] = _column(jnp.exp(b_last)) * s_old + _dot(kd, vb, TN)


def _mixers_prompt(qku, ucol, vg, alr, wa, ba, gn, w_pool, pool_scale, nb, t_len):
    dkk = wa.shape[1]
    dk, dv = dkk // GLA_HEADS, gn.shape[2]
    dvv = GLA_HEADS * dv
    g, gw, _ = w_pool.shape
    dp = g * gw
    c = GLA_CHUNK
    n_chunks = t_len // c
    tok = lambda w, blk=0: pl.BlockSpec((c, w), lambda b, i: (b * n_chunks + i, blk))
    return pl.pallas_call(
        functools.partial(_gla_prompt_body, n_chunks=n_chunks),
        grid=(nb, n_chunks),
        in_specs=[tok(dkk), tok(dkk, 1), tok(dvv), tok(dvv, 1), tok(LANES),
                  pl.BlockSpec((LANES, dkk), lambda b, i: (0, 0)),
                  pl.BlockSpec((1, dkk), lambda b, i: (0, 0)),
                  pl.BlockSpec((GLA_HEADS, 1, dv), lambda b, i: (0, 0, 0)),
                  tok(dp, ucol),
                  pl.BlockSpec((g, gw, gw), lambda b, i: (0, 0, 0)),
                  pl.BlockSpec((1, dp), lambda b, i: (0, 0))],
        out_specs=[tok(dvv),
                   pl.BlockSpec((None, GLA_HEADS, dk, dv), lambda b, i: (b, 0, 0, 0)),
                   tok(dp),
                   pl.BlockSpec((None, POOL_STATE, dp), lambda b, i: (b, 0, 0))],
        out_shape=[jax.ShapeDtypeStruct((nb * t_len, dvv), BF16),
                   jax.ShapeDtypeStruct((nb, GLA_HEADS, dk, dv), F32),
                   jax.ShapeDtypeStruct((nb * t_len, dp), BF16),
                   jax.ShapeDtypeStruct((nb, POOL_STATE, dp), F32)],
        scratch_shapes=[pltpu.VMEM((c, dkk), F32), pltpu.VMEM((c, dkk), BF16),
                        pltpu.VMEM((c, dk), F32), pltpu.VMEM((c, dk), F32), pltpu.VMEM((c, dk), F32),
                        pltpu.VMEM((GLA_HEADS, c, c), F32),
                        pltpu.VMEM((POOL_STATE + 1 + c, dp), F32)],
        compiler_params=_params(2),
        name="mixers_prompt",
    )(qku, qku, vg, vg, alr, wa, ba, gn, qku, w_pool, pool_scale)


def _gla_decode_body(q_ref, k_ref, v_ref, g_ref, alr_ref, wa_ref, ba_ref, gn_ref, s_ref, o_ref, a_ref, kb_ref):
    dk = q_ref.shape[2] // GLA_HEADS
    dv = v_ref.shape[2] // GLA_HEADS
    for r in range(q_ref.shape[0]):
        alr = alr_ref[r].astype(BF16)
        for h in range(GLA_HEADS):
            ks, vs = slice(h * dk, (h + 1) * dk), slice(h * dv, (h + 1) * dv)
            q = q_ref[r, :, ks] * (dk ** -0.5)
            kb = k_ref[r, :, ks].astype(BF16).astype(F32)
            vb = v_ref[r, :, vs].astype(F32)
            x = _dot(alr, wa_ref[:, ks].astype(BF16)) + ba_ref[:, ks]
            a = jnp.exp(_log_sigmoid(x) / GATE_NORM)
            a_ref[r, :, ks] = a
            kb_ref[r, :, ks] = kb
            o = jnp.sum(_column(q * a) * s_ref[r, h], axis=0, keepdims=True)
            o += jnp.sum(q * kb, axis=-1, keepdims=True) * vb
            o_ref[r, :, vs] = _head_out(o, g_ref[r, :, vs].astype(F32), gn_ref[h]).astype(o_ref.dtype)


def _gla_decode(qku, vg, alr, wa, ba, gn, state, tb=4):
    nb = qku.shape[0]
    dkk = wa.shape[1]
    dk, dv = dkk // GLA_HEADS, gn.shape[2]
    dvv = GLA_HEADS * dv
    tok = lambda w, blk=0: pl.BlockSpec((tb, 1, w), lambda b: (b, 0, blk))
    return pl.pallas_call(
        _gla_decode_body,
        grid=(nb // tb,),
        in_specs=[tok(dkk), tok(dkk, 1), tok(dvv), tok(dvv, 1), tok(LANES),
                  pl.BlockSpec((LANES, dkk), lambda b: (0, 0)),
                  pl.BlockSpec((1, dkk), lambda b: (0, 0)),
                  pl.BlockSpec((GLA_HEADS, 1, dv), lambda b: (0, 0, 0)),
                  pl.BlockSpec((tb, GLA_HEADS, dk, dv), lambda b: (b, 0, 0, 0))],
        out_specs=[tok(dvv), tok(dkk), tok(dkk)],
        out_shape=[jax.ShapeDtypeStruct((nb, 1, dvv), BF16),
                   jax.ShapeDtypeStruct((nb, 1, dkk), F32),
                   jax.ShapeDtypeStruct((nb, 1, dkk), F32)],
        compiler_params=_params(1),
        name="gla_decode",
    )(qku, qku, vg, vg, alr, wa, ba, gn, state)


def _tiles(t_len):
    return min(1024, t_len), min(512, t_len), min(512, t_len)


def _layer(xp, xd, mod, lw, dims, nb, t_len, pool_state_t, gla_state):
    (g1, w_in_t, w_alr_t, w_pool, pool_scale, wa, ba, gn, w_out, g2, w_ff1, w_ff3, w_ff2) = lw
    d_pool, d_k, d_v = dims
    n_dec = xd.shape[0]
    tm, tms, tmn = _tiles(t_len)
    tpb, tpbs, tpbn = t_len // tm, t_len // tms, t_len // tmn
    tn = 512

    h1p, ap = _normmod(xp, g1, mod, n_dec, 1, 0, tmn, tpbn, w_alr_t)
    h1d, ad = _normmod(xd, g1, mod, n_dec, 1, 0, n_dec, None, w_alr_t)
    u_tiles, qk_tiles, vg_tiles = d_pool // tn, 2 * d_k // tn, 2 * d_v // tn
    qku_block = lambda j: jnp.where(j < qk_tiles, j + u_tiles, j - qk_tiles)
    qkup, qkud = _proj(h1p, h1d, w_in_t, qk_tiles + u_tiles, qku_block, tm, tn, F32)
    vgp, vgd = _proj(h1p, h1d, w_in_t, vg_tiles, lambda j: j + u_tiles + qk_tiles, min(2 * tm, t_len), tn, BF16,
                     FFN2_VMEM_LIMIT)
    ucol = 2 * d_k // d_pool

    op, gla_p, pyp, pool_p = _mixers_prompt(qkup, ucol, vgp, ap, wa, ba, gn, w_pool, pool_scale, nb, t_len)
    pyd, pool_d_t = _pool_decode(qkud, ucol, pool_state_t, w_pool, pool_scale, 32)
    r3 = lambda t: t.reshape(t.shape[0], 1, t.shape[1])
    vgd3 = r3(vgd)
    od, a_d, kb_d = _gla_decode(r3(qkud), vgd3, r3(ad), wa, ba, gn, gla_state)
    od = od.reshape(n_dec, d_v)

    x1p, x1d = _outproj(pyp, op, pyd, od, w_out, xp, xd, mod, 2, tm, tn, tpb)
    h2p = _normmod(x1p, g2, mod, n_dec, 4, 3, tmn, tpbn)
    h2d = _normmod(x1d, g2, mod, n_dec, 4, 3, n_dec, None)
    actp, actd, gla_d = _ffn1(h2p, h2d, w_ff1, w_ff3, tm, 256, a_d, kb_d, vgd3, gla_state)
    x2p, x2d = _ffn2(actp, actd, w_ff2, x1p, x1d, mod, 5, tms, tn, tpbs)
    return x2p, x2d, pool_p, gla_p, pool_d_t, gla_d


def kernel(x_prompt, x_sample, state_pool, state_gla, c_prompt, c_sample, w_ada, b_ada, g_norm1, w_in,
           w_pool, pool_scale, w_a2, b_a, g_gla_out, w_out, g_norm2, w_ff1, w_ff3, w_ff2, g_final):
    bp, tp, d = x_prompt.shape
    bs, ts, _ = x_sample.shape
    depth = w_ada.shape[0]
    assert ts == 1, "the decode path handles one new token per sequence"
    assert bs % SUBLANES == 0 and bp <= SUBLANES
    d_pool = w_pool.shape[1] * w_pool.shape[2]
    d_k = w_a2.shape[2]
    d_v = g_gla_out.shape[1] * g_gla_out.shape[2]
    dims = (d_pool, d_k, d_v)
    d_main = d_pool + 2 * d_k + 2 * d_v

    c_all = jnp.concatenate([c_sample, c_prompt, jnp.zeros((2 * SUBLANES - bp, d), F32)], axis=0)

    hp = x_prompt.reshape(bp * tp, d)
    hd = x_sample.reshape(bs, d)
    pools_p, glas_p, pools_d, glas_d = [], [], [], []
    for l in range(depth):
        mod = _ada(c_all, w_ada[l], b_ada[l].reshape(1, -1))
        w_in_t = w_in[l].T
        w_alr_t = jnp.pad(w_in_t[d_main:], ((0, LANES - GATE_RANK), (0, 0)))
        wa = jnp.pad(w_a2[l], ((0, LANES - GATE_RANK), (0, 0)))
        lw = (g_norm1[l].reshape(1, d), w_in_t, w_alr_t, w_pool[l], pool_scale[l].reshape(1, -1), wa,
              b_a[l].reshape(1, -1), g_gla_out[l].reshape(GLA_HEADS, 1, -1), w_out[l],
              g_norm2[l].reshape(1, d), w_ff1[l], w_ff3[l], w_ff2[l])
        pool_state_t = jnp.transpose(state_pool[l], (1, 0, 2))
        hp, hd, pool_p, gla_p, pool_d_t, gla_d = _layer(hp, hd, mod, lw, dims, bp, tp, pool_state_t,
                                                        state_gla[l])
        pools_p.append(pool_p)
        glas_p.append(gla_p)
        pools_d.append(jnp.transpose(pool_d_t, (1, 0, 2)))
        glas_d.append(gla_d)
    gf = g_final.reshape(1, d)
    y_p = _norm(hp, gf, _tiles(tp)[2]).reshape(bp, tp, d)
    y_d = _norm(hd, gf, bs).reshape(bs, ts, d)
    return (y_p, y_d, jnp.stack(pools_p), jnp.stack(glas_p), jnp.stack(pools_d), jnp.stack(glas_d))
```

```python
import functools

import jax
import jax.numpy as jnp
from jax import lax
from jax.experimental import pallas as pl
from jax.experimental.pallas import tpu as pltpu

F32 = jnp.float32
BF16 = jnp.bfloat16

POOL_WINDOWS = (2, 4, 8, 16)
POOL_STATE = max(POOL_WINDOWS) - 1
GLA_HEADS = 4
GATE_RANK = 16
GATE_NORM = 16.0
N_MOD = 6
EPS = 1e-6
PAST_LEN = 16384

LANES = 128
SUBLANES = 8
VMEM_LIMIT = 56 * 1024 * 1024
GLA_CHUNK = 256
GLA_SUB = 16
GLA_SAFE_DECAY = 60.0

NT = (((1,), (1,)), ((), ()))
TN = (((0,), (0,)), ((), ()))


def _params(n_axes, vmem_limit=VMEM_LIMIT):
    return pltpu.CompilerParams(dimension_semantics=("arbitrary",) * n_axes, vmem_limit_bytes=vmem_limit)


def _silu(x):
    return x * jax.nn.sigmoid(x)


def _log_sigmoid(x):
    return jnp.minimum(x, 0.0) - jnp.log(1.0 + jnp.exp(-jnp.abs(x)))


def _dot(a, b, dims=None):
    if dims is None:
        return jnp.dot(a, b, preferred_element_type=F32)
    return lax.dot_general(a, b, dims, preferred_element_type=F32)


def _ada_body(c_ref, w_ref, b_ref, o_ref):
    s = _silu(c_ref[...]).astype(BF16)
    o_ref[...] = _dot(s, w_ref[...].astype(BF16)) + b_ref[...]


def _ada(c, w, b, tn=1024):
    m, d = c.shape
    n = w.shape[1]
    return pl.pallas_call(
        _ada_body,
        grid=(n // tn,),
        in_specs=[pl.BlockSpec((m, d), lambda j: (0, 0)),
                  pl.BlockSpec((d, tn), lambda j: (0, j)),
                  pl.BlockSpec((1, tn), lambda j: (0, j))],
        out_specs=pl.BlockSpec((m, tn), lambda j: (0, j)),
        out_shape=jax.ShapeDtypeStruct((m, n), F32),
        compiler_params=_params(1),
        name="ada",
    )(c, w, b)


def _prompt_mod_spec(n_dec, width, col_of):
    return pl.BlockSpec((SUBLANES, width), lambda *g: (n_dec // SUBLANES, col_of(*g)))


def _decode_mod_spec(n_dec, width, col_of):
    return pl.BlockSpec((n_dec, width), lambda *g: (0, col_of(*g)))


def _rms(x, gain):
    return x * lax.rsqrt(jnp.mean(x * x, axis=-1, keepdims=True) + EPS) * gain


def _normmod_body(x_ref, g_ref, sc_ref, sh_ref, *rest, tpb):
    if tpb is None:
        sc, sh = sc_ref[...], sh_ref[...]
    else:
        b = pl.program_id(0) // tpb
        sc, sh = sc_ref[pl.ds(b, 1), :], sh_ref[pl.ds(b, 1), :]
    h = (_rms(x_ref[...], g_ref[...]) * (1.0 + sc) + sh).astype(BF16)
    if len(rest) == 1:
        (o_ref,) = rest
    else:
        wt_ref, o_ref, side_ref = rest
        side_ref[...] = _dot(h, wt_ref[...].astype(BF16), NT)
    o_ref[...] = h


def _normmod(x, gain, mod, n_dec, sc_chunk, sh_chunk, tm, tpb, side_wt=None):
    m, d = x.shape
    spec = _decode_mod_spec if tpb is None else _prompt_mod_spec
    in_specs = [pl.BlockSpec((tm, d), lambda i: (i, 0)),
                pl.BlockSpec((1, d), lambda i: (0, 0)),
                spec(n_dec, d, lambda i: sc_chunk),
                spec(n_dec, d, lambda i: sh_chunk)]
    out_specs = pl.BlockSpec((tm, d), lambda i: (i, 0))
    out_shape = jax.ShapeDtypeStruct((m, d), BF16)
    args = (x, gain, mod, mod)
    if side_wt is not None:
        ns = side_wt.shape[0]
        in_specs.append(pl.BlockSpec((ns, d), lambda i: (0, 0)))
        out_specs = [out_specs, pl.BlockSpec((tm, ns), lambda i: (i, 0))]
        out_shape = [out_shape, jax.ShapeDtypeStruct((m, ns), F32)]
        args += (side_wt,)
    return pl.pallas_call(
        functools.partial(_normmod_body, tpb=tpb),
        grid=(m // tm,),
        in_specs=in_specs,
        out_specs=out_specs,
        out_shape=out_shape,
        compiler_params=_params(1),
        name="normmod",
    )(*args)


def _norm_body(x_ref, g_ref, o_ref):
    o_ref[...] = _rms(x_ref[...], g_ref[...])


def _norm(x, gain, tm):
    m, d = x.shape
    return pl.pallas_call(
        _norm_body,
        grid=(m // tm,),
        in_specs=[pl.BlockSpec((tm, d), lambda i: (i, 0)),
                  pl.BlockSpec((1, d), lambda i: (0, 0))],
        out_specs=pl.BlockSpec((tm, d), lambda i: (i, 0)),
        out_shape=jax.ShapeDtypeStruct((m, d), F32),
        compiler_params=_params(1),
        name="finalnorm",
    )(x, gain)


def _cast_once(w_ref, wbf_ref):
    @pl.when(pl.program_id(1) == 0)
    def _():
        wbf_ref[...] = w_ref[...].astype(BF16)


def _row_steps(prompt_fn, decode_fn):
    i = pl.program_id(1)
    pl.when(i == 0)(decode_fn)
    pl.when(i > 0)(prompt_fn)


def _prow(mt):
    def tile(j, i):
        t = jnp.maximum(i - 1, 0)
        return jnp.where(j % 2 == 0, t, mt - 1 - t)
    return tile


def _prompt_seq(mt, tpb):
    return _prow(mt)(pl.program_id(0), pl.program_id(1)) // tpb


def _proj_body(ap_ref, ad_ref, wt_ref, op_ref, od_ref, wbf_ref):
    _cast_once(wt_ref, wbf_ref)

    def prompt():
        op_ref[...] = _dot(ap_ref[...], wbf_ref[...], NT).astype(op_ref.dtype)

    def decode():
        od_ref[...] = _dot(ad_ref[...], wbf_ref[...], NT).astype(od_ref.dtype)

    _row_steps(prompt, decode)


def _proj(ap, ad, wt, n_tiles, wblock, tm, tn, out_dtype, vmem_limit=VMEM_LIMIT):
    mp, kdim = ap.shape
    md = ad.shape[0]
    mt = mp // tm
    pr = _prow(mt)
    return pl.pallas_call(
        _proj_body,
        grid=(n_tiles, mt + 1),
        in_specs=[pl.BlockSpec((tm, kdim), lambda j, i: (pr(j, i), 0)),
                  pl.BlockSpec((md, kdim), lambda j, i: (0, 0), pipeline_mode=pl.Buffered(1)),
                  pl.BlockSpec((tn, kdim), lambda j, i: (wblock(j), 0))],
        out_specs=[pl.BlockSpec((tm, tn), lambda j, i: (pr(j, i), j)),
                   pl.BlockSpec((md, tn), lambda j, i: (0, j))],
        out_shape=[jax.ShapeDtypeStruct((mp, n_tiles * tn), out_dtype),
                   jax.ShapeDtypeStruct((md, n_tiles * tn), out_dtype)],
        scratch_shapes=[pltpu.VMEM((tn, kdim), BF16)],
        compiler_params=_params(2, vmem_limit),
        name="proj",
    )(ap, ad, wt)


def _outproj_body(pyp_ref, op_ref, pyd_ref, od_ref, w_ref, xp_ref, xd_ref, gp_ref, gd_ref,
                  outp_ref, outd_ref, wbf_ref, *, mt, tpb, d_pool):
    _cast_once(w_ref, wbf_ref)
    seq = _prompt_seq(mt, tpb)

    def mix(py_ref, o_ref):
        return _dot(py_ref[...], wbf_ref[:d_pool, :]) + _dot(o_ref[...], wbf_ref[d_pool:, :])

    def prompt():
        outp_ref[...] = xp_ref[...] + gp_ref[pl.ds(seq, 1), :] * mix(pyp_ref, op_ref)

    def decode():
        outd_ref[...] = xd_ref[...] + gd_ref[...] * mix(pyd_ref, od_ref)

    _row_steps(prompt, decode)


def _outproj(pyp, op, pyd, od, w, xp, xd, mod, gt_chunk, tm, tn, tpb):
    mp, d_pool = pyp.shape
    md = pyd.shape[0]
    d_v = op.shape[1]
    kdim, n = w.shape
    mt = mp // tm
    per = n // tn
    pr = _prow(mt)
    gcol = lambda j, i: gt_chunk * per + j
    return pl.pallas_call(
        functools.partial(_outproj_body, mt=mt, tpb=tpb, d_pool=d_pool),
        grid=(per, mt + 1),
        in_specs=[pl.BlockSpec((tm, d_pool), lambda j, i: (pr(j, i), 0)),
                  pl.BlockSpec((tm, d_v), lambda j, i: (pr(j, i), 0)),
                  pl.BlockSpec((md, d_pool), lambda j, i: (0, 0)),
                  pl.BlockSpec((md, d_v), lambda j, i: (0, 0)),
                  pl.BlockSpec((kdim, tn), lambda j, i: (0, j)),
                  pl.BlockSpec((tm, tn), lambda j, i: (pr(j, i), j)),
                  pl.BlockSpec((md, tn), lambda j, i: (0, j)),
                  _prompt_mod_spec(md, tn, gcol),
                  _decode_mod_spec(md, tn, gcol)],
        out_specs=[pl.BlockSpec((tm, tn), lambda j, i: (pr(j, i), j)),
                   pl.BlockSpec((md, tn), lambda j, i: (0, j))],
        out_shape=[jax.ShapeDtypeStruct((mp, n), F32), jax.ShapeDtypeStruct((md, n), F32)],
        scratch_shapes=[pltpu.VMEM((kdim, tn), BF16)],
        compiler_params=_params(2),
        name="outproj",
    )(pyp, op, pyd, od, w, xp, xd, mod, mod)


def _state_block_rows(n_rows, dk, n_steps):
    rb = -(-n_rows // n_steps)
    rb += -rb % LANES
    while n_rows % rb:
        rb += LANES
    assert dk % LANES == 0 and rb <= dk + LANES, "a row block may touch at most two (sequence, head) pairs"
    return rb


def _state_job(a_ref, kb_ref, v0_ref, v1_ref, s_ref, s_out_ref, rows_of_first_pair):
    in_first = lax.broadcasted_iota(jnp.int32, (s_ref.shape[0], 1), 0) < rows_of_first_pair
    v = jnp.where(in_first, v0_ref[...].astype(F32), v1_ref[...].astype(F32))
    s_out_ref[...] = _column(a_ref[...]) * s_ref[...] + _column(kb_ref[...]) * v


FFN1_RING = 3


def _ffn1_body(hp_hbm, hd_ref, w1_ref, w3_ref, a_ref, kb_ref, v0_ref, v1_ref, s_ref,
               op_ref, od_ref, s_out_ref, w1bf_ref, w3bf_ref, ring_ref, sem, *, dk, n_blocks, mt):
    _cast_once(w1_ref, w1bf_ref)
    _cast_once(w3_ref, w3bf_ref)
    j, i = pl.program_id(0), pl.program_id(1)
    blk = jnp.minimum(j * pl.num_programs(1) + i, n_blocks - 1)
    rows_of_first_pair = dk - (blk * s_ref.shape[0]) % dk
    tm = ring_ref.shape[1]
    n_prompt_steps = pl.num_programs(0) * mt
    p = j * mt + i - 1

    def fetch(q):
        col, t = q // mt, q % mt
        tile = jnp.where(col % 2 == 0, t, mt - 1 - t)
        return pltpu.make_async_copy(hp_hbm.at[pl.ds(tile * tm, tm), :], ring_ref.at[q % FFN1_RING],
                                     sem.at[q % FFN1_RING])

    @pl.when((j == 0) & (i == 0))
    def _():
        for q in range(FFN1_RING - 1):
            fetch(q).start()

    def side_job():
        _state_job(a_ref, kb_ref, v0_ref, v1_ref, s_ref, s_out_ref, rows_of_first_pair)

    def prompt():
        fetch(p).wait()

        @pl.when(p + FFN1_RING - 1 < n_prompt_steps)
        def _():
            fetch(p + FFN1_RING - 1).start()

        h = ring_ref[p % FFN1_RING]
        op_ref[...] = (_silu(_dot(h, w1bf_ref[...])) * _dot(h, w3bf_ref[...])).astype(op_ref.dtype)
        side_job()

    def decode():
        h = hd_ref[...]
        od_ref[...] = (_silu(_dot(h, w1bf_ref[...])) * _dot(h, w3bf_ref[...])).astype(od_ref.dtype)
        side_job()

    _row_steps(prompt, decode)


def _ffn1(hp, hd, w1, w3, tm, tn, a, kb, vg, state):
    mp, kdim = hp.shape
    md = hd.shape[0]
    n = w1.shape[1]
    mt = mp // tm
    pr = _prow(mt)
    n_seq, n_heads, dk, dv = state.shape
    n_rows = n_seq * n_heads * dk
    rb = _state_block_rows(n_rows, dk, (n // tn) * (mt + 1))
    n_blocks = n_rows // rb
    block = lambda j, i: jnp.minimum(j * (mt + 1) + i, n_blocks - 1)

    def value_row(which):
        def index(j, i):
            p = jnp.minimum(block(j, i) * rb // dk + which, n_seq * n_heads - 1)
            return p // n_heads, 0, p % n_heads
        return pl.BlockSpec((None, 1, dv), index)

    row_vec = pl.BlockSpec((None, 1, rb), lambda j, i: (block(j, i), 0, 0))
    st = pl.BlockSpec((rb, dv), lambda j, i: (block(j, i), 0))
    wspec = pl.BlockSpec((kdim, tn), lambda j, i: (0, j))
    act_p, act_d, state_new = pl.pallas_call(
        functools.partial(_ffn1_body, dk=dk, n_blocks=n_blocks, mt=mt),
        grid=(n // tn, mt + 1),
        in_specs=[pl.BlockSpec(memory_space=pl.ANY),
                  pl.BlockSpec((md, kdim), lambda j, i: (0, 0)),
                  wspec, wspec, row_vec, row_vec, value_row(0), value_row(1), st],
        out_specs=[pl.BlockSpec((tm, tn), lambda j, i: (pr(j, i), j)),
                   pl.BlockSpec((md, tn), lambda j, i: (0, j)),
                   st],
        out_shape=[jax.ShapeDtypeStruct((mp, n), BF16), jax.ShapeDtypeStruct((md, n), BF16),
                   jax.ShapeDtypeStruct((n_rows, dv), F32)],
        scratch_shapes=[pltpu.VMEM((kdim, tn), BF16), pltpu.VMEM((kdim, tn), BF16),
                        pltpu.VMEM((FFN1_RING, tm, kdim), BF16), pltpu.SemaphoreType.DMA((FFN1_RING,))],
        compiler_params=_params(2, FFN2_VMEM_LIMIT),
        name="ffn1",
    )(hp, hd, w1, w3, a.reshape(n_blocks, 1, rb), kb.reshape(n_blocks, 1, rb), vg, vg,
      state.reshape(n_rows, dv))
    return act_p, act_d, state_new.reshape(state.shape)


FFN2_WEIGHT_CHUNKS = 8
FFN2_VMEM_LIMIT = 62 * 1024 * 1024


def _ffn2_body(ap_ref, ad_ref, w_hbm, xp_ref, xd_ref, gp_ref, gd_ref, outp_ref, outd_ref,
               wbf_ref, stage_ref, sem, *, mt, tpb):
    j, i = pl.program_id(0), pl.program_id(1)
    kc, tn = stage_ref.shape[1], stage_ref.shape[2]
    n_chunks = wbf_ref.shape[1] // kc
    slot = j % 2

    def chunk(col_tile, c):
        return pltpu.make_async_copy(w_hbm.at[pl.ds(c * kc, kc), pl.ds(col_tile * tn, tn)],
                                     stage_ref.at[c % 2], sem.at[c % 2])

    def land(dst_slot, c):
        wbf_ref[dst_slot, pl.ds(c * kc, kc), :] = stage_ref[c % 2].astype(BF16)

    @pl.when((j == 0) & (i == 0))
    def _():
        for c in range(n_chunks):
            chunk(0, c).start()
            chunk(0, c).wait()
            land(0, c)

    @pl.when(j + 1 < pl.num_programs(0))
    def _():
        @pl.when(i == 0)
        def _():
            chunk(j + 1, 0).start()

        for c in range(n_chunks):
            @pl.when(i == c + 1)
            def _(c=c):
                chunk(j + 1, c).wait()
                if c + 1 < n_chunks:
                    chunk(j + 1, c + 1).start()
                land(1 - slot, c)

    seq = _prompt_seq(mt, tpb)

    def prompt():
        outp_ref[...] = xp_ref[...] + gp_ref[pl.ds(seq, 1), :] * _dot(ap_ref[...], wbf_ref[slot])

    def decode():
        outd_ref[...] = xd_ref[...] + gd_ref[...] * _dot(ad_ref[...], wbf_ref[slot])

    _row_steps(prompt, decode)


def _ffn2(ap, ad, w, xp, xd, mod, gt_chunk, tm, tn, tpb):
    mp, kdim = ap.shape
    md = ad.shape[0]
    n = w.shape[1]
    mt = mp // tm
    per = n // tn
    pr = _prow(mt)
    kc = kdim // FFN2_WEIGHT_CHUNKS
    assert kc * FFN2_WEIGHT_CHUNKS == kdim and kc % 16 == 0 and mt >= FFN2_WEIGHT_CHUNKS
    gcol = lambda j, i: gt_chunk * per + j
    return pl.pallas_call(
        functools.partial(_ffn2_body, mt=mt, tpb=tpb),
        grid=(per, mt + 1),
        in_specs=[pl.BlockSpec((tm, kdim), lambda j, i: (pr(j, i), 0)),
                  pl.BlockSpec((md, kdim), lambda j, i: (0, 0), pipeline_mode=pl.Buffered(1)),
                  pl.BlockSpec(memory_space=pl.ANY),
                  pl.BlockSpec((tm, tn), lambda j, i: (pr(j, i), j)),
                  pl.BlockSpec((md, tn), lambda j, i: (0, j)),
                  _prompt_mod_spec(md, tn, gcol),
                  _decode_mod_spec(md, tn, gcol)],
        out_specs=[pl.BlockSpec((tm, tn), lambda j, i: (pr(j, i), j)),
                   pl.BlockSpec((md, tn), lambda j, i: (0, j))],
        out_shape=[jax.ShapeDtypeStruct((mp, n), F32), jax.ShapeDtypeStruct((md, n), F32)],
        scratch_shapes=[pltpu.VMEM((2, kdim, tn), BF16), pltpu.VMEM((2, kc, tn), F32),
                        pltpu.SemaphoreType.DMA((2,))],
        compiler_params=_params(2, FFN2_VMEM_LIMIT),
        name="ffn2",
    )(ap, ad, w, xp, xd, mod, mod)


def _pool_group_matmul(d, wp_ref, ps_ref, n_groups, gw):
    outs = []
    for gi in range(n_groups):
        dg = d[:, gi * gw:(gi + 1) * gw].astype(BF16)
        outs.append(_dot(dg, wp_ref[gi].astype(BF16)))
    return jnp.concatenate(outs, axis=-1) * ps_ref[...]


def _pool_prompt_tile(t, n_tiles, u_ref, wp_ref, ps_ref, y_ref, st_ref, buf_ref):
    tm = u_ref.shape[0]
    halo = POOL_STATE + 1
    n_groups = len(POOL_WINDOWS)
    gw = u_ref.shape[1] // n_groups

    @pl.when(t == 0)
    def _():
        buf_ref[:halo, :] = jnp.zeros((halo, buf_ref.shape[1]), F32)

    u = u_ref[...]
    buf_ref[halo:, :] = u
    pos = t * tm + lax.broadcasted_iota(jnp.int32, (tm, 1), 0)
    parts = []
    for gi, w in enumerate(POOL_WINDOWS):
        lo, hi = gi * gw, (gi + 1) * gw
        win = u[:, lo:hi]
        for s in range(1, w):
            win = win + buf_ref[halo - s:halo - s + tm, lo:hi]
        cnt = jnp.minimum(w, pos + 1).astype(F32)
        parts.append(win / cnt - u[:, lo:hi])
    d = jnp.concatenate(parts, axis=-1)
    y_ref[...] = _pool_group_matmul(d, wp_ref, ps_ref, n_groups, gw).astype(y_ref.dtype)

    @pl.when(t == n_tiles - 1)
    def _():
        st_ref[...] = buf_ref[halo + tm - POOL_STATE:, :]

    buf_ref[:halo, :] = buf_ref[tm:, :]


def _pool_decode_body(u_ref, sp_ref, wp_ref, ps_ref, y_ref, st_ref):
    n_groups = len(POOL_WINDOWS)
    gw = u_ref.shape[1] // n_groups
    u = u_ref[...]
    parts = []
    for gi, w in enumerate(POOL_WINDOWS):
        lo, hi = gi * gw, (gi + 1) * gw
        win = u[:, lo:hi]
        for s in range(1, w):
            win = win + sp_ref[POOL_STATE - s, :, lo:hi]
        cnt = float(min(w, PAST_LEN + 1))
        parts.append(win / cnt - u[:, lo:hi])
    d = jnp.concatenate(parts, axis=-1)
    y_ref[...] = _pool_group_matmul(d, wp_ref, ps_ref, n_groups, gw).astype(y_ref.dtype)
    for r in range(POOL_STATE - 1):
        st_ref[r] = sp_ref[r + 1]
    st_ref[POOL_STATE - 1] = u


def _pool_decode(u, ucol, state_t, w_pool, pool_scale, tb):
    nb = u.shape[0]
    g, gw, _ = w_pool.shape
    dp = g * gw
    return pl.pallas_call(
        _pool_decode_body,
        grid=(nb // tb,),
        in_specs=[pl.BlockSpec((tb, dp), lambda b: (b, ucol)),
                  pl.BlockSpec((POOL_STATE, tb, dp), lambda b: (0, b, 0)),
                  pl.BlockSpec((g, gw, gw), lambda b: (0, 0, 0)),
                  pl.BlockSpec((1, dp), lambda b: (0, 0))],
        out_specs=[pl.BlockSpec((tb, dp), lambda b: (b, 0)),
                   pl.BlockSpec((POOL_STATE, tb, dp), lambda b: (0, b, 0))],
        out_shape=[jax.ShapeDtypeStruct((nb, dp), BF16),
                   jax.ShapeDtypeStruct((POOL_STATE, nb, dp), F32)],
        compiler_params=_params(1),
        name="pool_decode",
    )(u, state_t, w_pool, pool_scale)


def _head_out(o, g, gain):
    return _silu(g) * _rms(o, gain)


def _column(row_vec):
    return jnp.broadcast_to(row_vec, (LANES, row_vec.shape[1])).T[:, :1]


def _cumsum_rows(x):
    n = x.shape[0]
    tri = (lax.broadcasted_iota(jnp.int32, (n, n), 0) >= lax.broadcasted_iota(jnp.int32, (n, n), 1))
    tri = jnp.where(tri, 1.0, 0.0).astype(BF16)
    hi = x.astype(BF16)
    r1 = x - hi.astype(F32)
    mid = r1.astype(BF16)
    lo = (r1 - mid.astype(F32)).astype(BF16)
    return _dot(tri, hi) + _dot(tri, mid) + _dot(tri, lo)


def _exact_scores(b_scr, q_scr, k_scr, a_scr):
    rows = b_scr.shape[0]

    def block(i, carry):
        r0 = pl.multiple_of(i * GLA_SUB, GLA_SUB)
        b_i = b_scr[pl.ds(r0, GLA_SUB), :]
        q_i = q_scr[pl.ds(r0, GLA_SUB), :]
        b_first = b_scr[pl.ds(r0, 1), :]
        qt = (q_i * jnp.exp(b_i - b_first)).astype(BF16)
        kt = (k_scr[...] * jnp.exp(jnp.minimum(b_first - b_scr[...], 0.0))).astype(BF16)
        panel = _dot(qt, kt, NT)
        col = lax.broadcasted_iota(jnp.int32, (GLA_SUB, rows), 1)
        row = lax.broadcasted_iota(jnp.int32, (GLA_SUB, rows), 0) + r0
        diag = jnp.zeros((GLA_SUB, rows), F32)
        for j in range(GLA_SUB):
            b_j = b_scr[pl.ds(r0 + j, 1), :]
            k_j = k_scr[pl.ds(r0 + j, 1), :]
            p = q_i * k_j * jnp.exp(jnp.minimum(b_i - b_j, 0.0))
            diag = jnp.where(col == r0 + j, jnp.sum(p, axis=-1, keepdims=True), diag)
        a_scr[pl.ds(r0, GLA_SUB), :] = jnp.where(col < r0, panel, jnp.where(col <= row, diag, 0.0))
        return carry

    lax.fori_loop(0, rows // GLA_SUB, block, 0)


def _gla_prompt_body(q_ref, k_ref, v_ref, g_ref, alr_ref, wa_ref, ba_ref, gn_ref, u_ref, wp_ref, ps_ref,
                     o_ref, s_ref, py_ref, pst_ref,
                     b_all, qe_all, b_scr, q_scr, k_scr, a_scr, pbuf_ref, *, n_chunks):
    _pool_prompt_tile(pl.program_id(1), n_chunks, u_ref, wp_ref, ps_ref, py_ref, pst_ref, pbuf_ref)
    rows = q_ref.shape[0]
    dk = q_ref.shape[1] // GLA_HEADS
    dv = v_ref.shape[1] // GLA_HEADS
    scale = dk ** -0.5
    heads = [(h, slice(h * dk, (h + 1) * dk), slice(h * dv, (h + 1) * dv)) for h in range(GLA_HEADS)]

    @pl.when(pl.program_id(1) == 0)
    def _():
        s_ref[...] = jnp.zeros(s_ref.shape, F32)

    x = _dot(alr_ref[...].astype(BF16), wa_ref[...].astype(BF16)) + ba_ref[...]
    b_all[...] = _cumsum_rows(_log_sigmoid(x) / GATE_NORM)
    qe_all[...] = (q_ref[...] * scale * jnp.exp(b_all[...])).astype(BF16)

    mild = jnp.max(-b_all[rows - 1:rows, :]) < GLA_SAFE_DECAY

    @pl.when(mild)
    def _():
        causal = (lax.broadcasted_iota(jnp.int32, (rows, rows), 1)
                  <= lax.broadcasted_iota(jnp.int32, (rows, rows), 0))
        for h, ks, _ in heads:
            kinv = (k_ref[:, ks] * jnp.exp(-b_all[:, ks])).astype(BF16)
            a_scr[h] = jnp.where(causal, _dot(qe_all[:, ks], kinv, NT), 0.0)

    @pl.when(jnp.logical_not(mild))
    def _():
        for h, ks, _ in heads:
            b_scr[...] = b_all[:, ks]
            q_scr[...] = q_ref[:, ks] * scale
            k_scr[...] = k_ref[:, ks]
            _exact_scores(b_scr, q_scr, k_scr, a_scr.at[h])

    for h, ks, vs in heads:
        b = b_all[:, ks]
        b_last = b_all[rows - 1:rows, ks]
        vb = v_ref[:, vs]
        s_old = s_ref[h]
        o = _dot(qe_all[:, ks], s_old.astype(BF16)) + _dot(a_scr[h].astype(BF16), vb)
        o_ref[:, vs] = _head_out(o, g_ref[:, vs].astype(F32), gn_ref[h]).astype(o_ref.dtype)
        kd = (k_ref[:, ks] * jnp.exp(b_last - b)).astype(BF16)
        s_ref[h] = _column(jnp.exp(b_last)) * s_old + _dot(kd, vb, TN)


def _mixers_prompt(qku, ucol, vg, alr, wa, ba, gn, w_pool, pool_scale, nb, t_len):
    dkk = wa.shape[1]
    dk, dv = dkk // GLA_HEADS, gn.shape[2]
    dvv = GLA_HEADS * dv
    g, gw, _ = w_pool.shape
    dp = g * gw
    c = GLA_CHUNK
    n_chunks = t_len // c
    tok = lambda w, blk=0: pl.BlockSpec((c, w), lambda b, i: (b * n_chunks + i, blk))
    return pl.pallas_call(
        functools.partial(_gla_prompt_body, n_chunks=n_chunks),
        grid=(nb, n_chunks),
        in_specs=[tok(dkk), tok(dkk, 1), tok(dvv), tok(dvv, 1), tok(LANES),
                  pl.BlockSpec((LANES, dkk), lambda b, i: (0, 0)),
                  pl.BlockSpec((1, dkk), lambda b, i: (0, 0)),
                  pl.BlockSpec((GLA_HEADS, 1, dv), lambda b, i: (0, 0, 0)),
                  tok(dp, ucol),
                  pl.BlockSpec((g, gw, gw), lambda b, i: (0, 0, 0)),
                  pl.BlockSpec((1, dp), lambda b, i: (0, 0))],
        out_specs=[tok(dvv),
                   pl.BlockSpec((None, GLA_HEADS, dk, dv), lambda b, i: (b, 0, 0, 0)),
                   tok(dp),
                   pl.BlockSpec((None, POOL_STATE, dp), lambda b, i: (b, 0, 0))],
        out_shape=[jax.ShapeDtypeStruct((nb * t_len, dvv), BF16),
                   jax.ShapeDtypeStruct((nb, GLA_HEADS, dk, dv), F32),
                   jax.ShapeDtypeStruct((nb * t_len, dp), BF16),
                   jax.ShapeDtypeStruct((nb, POOL_STATE, dp), F32)],
        scratch_shapes=[pltpu.VMEM((c, dkk), F32), pltpu.VMEM((c, dkk), BF16),
                        pltpu.VMEM((c, dk), F32), pltpu.VMEM((c, dk), F32), pltpu.VMEM((c, dk), F32),
                        pltpu.VMEM((GLA_HEADS, c, c), F32),
                        pltpu.VMEM((POOL_STATE + 1 + c, dp), F32)],
        compiler_params=_params(2),
        name="mixers_prompt",
    )(qku, qku, vg, vg, alr, wa, ba, gn, qku, w_pool, pool_scale)


def _gla_decode_body(q_ref, k_ref, v_ref, g_ref, alr_ref, wa_ref, ba_ref, gn_ref, s_ref, o_ref, a_ref, kb_ref):
    dk = q_ref.shape[2] // GLA_HEADS
    dv = v_ref.shape[2] // GLA_HEADS
    for r in range(q_ref.shape[0]):
        alr = alr_ref[r].astype(BF16)
        for h in range(GLA_HEADS):
            ks, vs = slice(h * dk, (h + 1) * dk), slice(h * dv, (h + 1) * dv)
            q = q_ref[r, :, ks] * (dk ** -0.5)
            kb = k_ref[r, :, ks].astype(BF16).astype(F32)
            vb = v_ref[r, :, vs].astype(F32)
            x = _dot(alr, wa_ref[:, ks].astype(BF16)) + ba_ref[:, ks]
            a = jnp.exp(_log_sigmoid(x) / GATE_NORM)
            a_ref[r, :, ks] = a
            kb_ref[r, :, ks] = kb
            o = jnp.sum(_column(q * a) * s_ref[r, h], axis=0, keepdims=True)
            o += jnp.sum(q * kb, axis=-1, keepdims=True) * vb
            o_ref[r, :, vs] = _head_out(o, g_ref[r, :, vs].astype(F32), gn_ref[h]).astype(o_ref.dtype)


def _gla_decode(qku, vg, alr, wa, ba, gn, state, tb=4):
    nb = qku.shape[0]
    dkk = wa.shape[1]
    dk, dv = dkk // GLA_HEADS, gn.shape[2]
    dvv = GLA_HEADS * dv
    tok = lambda w, blk=0: pl.BlockSpec((tb, 1, w), lambda b: (b, 0, blk))
    return pl.pallas_call(
        _gla_decode_body,
        grid=(nb // tb,),
        in_specs=[tok(dkk), tok(dkk, 1), tok(dvv), tok(dvv, 1), tok(LANES),
                  pl.BlockSpec((LANES, dkk), lambda b: (0, 0)),
                  pl.BlockSpec((1, dkk), lambda b: (0, 0)),
                  pl.BlockSpec((GLA_HEADS, 1, dv), lambda b: (0, 0, 0)),
                  pl.BlockSpec((tb, GLA_HEADS, dk, dv), lambda b: (b, 0, 0, 0))],
        out_specs=[tok(dvv), tok(dkk), tok(dkk)],
        out_shape=[jax.ShapeDtypeStruct((nb, 1, dvv), BF16),
                   jax.ShapeDtypeStruct((nb, 1, dkk), F32),
                   jax.ShapeDtypeStruct((nb, 1, dkk), F32)],
        compiler_params=_params(1),
        name="gla_decode",
    )(qku, qku, vg, vg, alr, wa, ba, gn, state)


def _tiles(t_len):
    return min(1024, t_len), min(512, t_len), min(512, t_len)


def _layer(xp, xd, mod, lw, dims, nb, t_len, pool_state_t, gla_state):
    (g1, w_in_t, w_alr_t, w_pool, pool_scale, wa, ba, gn, w_out, g2, w_ff1, w_ff3, w_ff2) = lw
    d_pool, d_k, d_v = dims
    n_dec = xd.shape[0]
    tm, tms, tmn = _tiles(t_len)
    tpb, tpbs, tpbn = t_len // tm, t_len // tms, t_len // tmn
    tn = 512

    h1p, ap = _normmod(xp, g1, mod, n_dec, 1, 0, tmn, tpbn, w_alr_t)
    h1d, ad = _normmod(xd, g1, mod, n_dec, 1, 0, n_dec, None, w_alr_t)
    u_tiles, qk_tiles, vg_tiles = d_pool // tn, 2 * d_k // tn, 2 * d_v // tn
    qku_block = lambda j: jnp.where(j < qk_tiles, j + u_tiles, j - qk_tiles)
    qkup, qkud = _proj(h1p, h1d, w_in_t, qk_tiles + u_tiles, qku_block, tm, tn, F32)
    vgp, vgd = _proj(h1p, h1d, w_in_t, vg_tiles, lambda j: j + u_tiles + qk_tiles, min(2 * tm, t_len), tn, BF16,
                     FFN2_VMEM_LIMIT)
    ucol = 2 * d_k // d_pool

    op, gla_p, pyp, pool_p = _mixers_prompt(qkup, ucol, vgp, ap, wa, ba, gn, w_pool, pool_scale, nb, t_len)
    pyd, pool_d_t = _pool_decode(qkud, ucol, pool_state_t, w_pool, pool_scale, 32)
    r3 = lambda t: t.reshape(t.shape[0], 1, t.shape[1])
    vgd3 = r3(vgd)
    od, a_d, kb_d = _gla_decode(r3(qkud), vgd3, r3(ad), wa, ba, gn, gla_state)
    od = od.reshape(n_dec, d_v)

    x1p, x1d = _outproj(pyp, op, pyd, od, w_out, xp, xd, mod, 2, tm, tn, tpb)
    h2p = _normmod(x1p, g2, mod, n_dec, 4, 3, tmn, tpbn)
    h2d = _normmod(x1d, g2, mod, n_dec, 4, 3, n_dec, None)
    actp, actd, gla_d = _ffn1(h2p, h2d, w_ff1, w_ff3, tm, 256, a_d, kb_d, vgd3, gla_state)
    x2p, x2d = _ffn2(actp, actd, w_ff2, x1p, x1d, mod, 5, tms, tn, tpbs)
    return x2p, x2d, pool_p, gla_p, pool_d_t, gla_d


def kernel(x_prompt, x_sample, state_pool, state_gla, c_prompt, c_sample, w_ada, b_ada, g_norm1, w_in,
           w_pool, pool_scale, w_a2, b_a, g_gla_out, w_out, g_norm2, w_ff1, w_ff3, w_ff2, g_final):
    bp, tp, d = x_prompt.shape
    bs, ts, _ = x_sample.shape
    depth = w_ada.shape[0]
    assert ts == 1, "the decode path handles one new token per sequence"
    assert bs % SUBLANES == 0 and bp <= SUBLANES
    d_pool = w_pool.shape[1] * w_pool.shape[2]
    d_k = w_a2.shape[2]
    d_v = g_gla_out.shape[1] * g_gla_out.shape[2]
    dims = (d_pool, d_k, d_v)
    d_main = d_pool + 2 * d_k + 2 * d_v

    c_all = jnp.concatenate([c_sample, c_prompt, jnp.zeros((2 * SUBLANES - bp, d), F32)], axis=0)

    hp = x_prompt.reshape(bp * tp, d)
    hd = x_sample.reshape(bs, d)
    pools_p, glas_p, pools_d, glas_d = [], [], [], []
    for l in range(depth):
        mod = _ada(c_all, w_ada[l], b_ada[l].reshape(1, -1))
        w_in_t = w_in[l].T
        w_alr_t = jnp.pad(w_in_t[d_main:], ((0, LANES - GATE_RANK), (0, 0)))
        wa = jnp.pad(w_a2[l], ((0, LANES - GATE_RANK), (0, 0)))
        lw = (g_norm1[l].reshape(1, d), w_in_t, w_alr_t, w_pool[l], pool_scale[l].reshape(1, -1), wa,
              b_a[l].reshape(1, -1), g_gla_out[l].reshape(GLA_HEADS, 1, -1), w_out[l],
              g_norm2[l].reshape(1, d), w_ff1[l], w_ff3[l], w_ff2[l])
        pool_state_t = jnp.transpose(state_pool[l], (1, 0, 2))
        hp, hd, pool_p, gla_p, pool_d_t, gla_d = _layer(hp, hd, mod, lw, dims, bp, tp, pool_state_t,
                                                        state_gla[l])
        pools_p.append(pool_p)
        glas_p.append(gla_p)
        pools_d.append(jnp.transpose(pool_d_t, (1, 0, 2)))
        glas_d.append(gla_d)
    gf = g_final.reshape(1, d)
    y_p = _norm(hp, gf, _tiles(tp)[2]).reshape(bp, tp, d)
    y_d = _norm(hd, gf, bs).reshape(bs, ts, d)
    return (y_p, y_d, jnp.stack(pools_p), jnp.stack(glas_p), jnp.stack(pools_d), jnp.stack(glas_d))
```
